```python
import math
import jax
import jax.numpy as jnp
from jax import lax
import numpy as np

D_MODEL = 1024
BATCH = 16
SEQ = 2048
DEPTH = 1

PLE_DIM = 256
RMS_EPS = 1e-6

ATT_PATTERNS = ((128, 1), (512, 4), (2048, 16))
ATT_GROUPS = 3
ATT_HEADS_PER_GROUP = 8
ATT_HEADS = ATT_GROUPS * ATT_HEADS_PER_GROUP
ATT_HEAD_DIM = 64
ATT_WIDTH = ATT_HEADS * ATT_HEAD_DIM
ATT_OUT_WIDTH = ATT_HEADS_PER_GROUP * ATT_HEAD_DIM
ATT_BLOCK = 128
ALIBI_MAX_BIAS = 8.0

SSD_EXPAND = 2
SSD_INNER = SSD_EXPAND * D_MODEL
SSD_HEAD_DIM = 64
SSD_HEADS = SSD_INNER // SSD_HEAD_DIM
SSD_GROUPS = 8
SSD_HEADS_PER_GROUP = SSD_HEADS // SSD_GROUPS
SSD_STATE = 128
SSD_CONV = 4
SSD_CHUNK = 128
SSD_CONV_CH = SSD_INNER + 2 * SSD_GROUPS * SSD_STATE

MOE_GROUPS = 4
MOE_EXPERTS_PER_GROUP = 4
MOE_EXPERTS = MOE_GROUPS * MOE_EXPERTS_PER_GROUP
MOE_TOP_K = 2
MOE_HIDDEN = D_MODEL // 2

IN_COLS = 3 * ATT_WIDTH + SSD_INNER + SSD_CONV_CH + SSD_HEADS + 2 * D_MODEL

kernel_name = "hybrid_dilated_attn_ssd_hmoe_block"


def rms_norm(x, g):
    xf = x.astype(jnp.float32)
    y = xf * lax.rsqrt(jnp.mean(xf * xf, axis=-1, keepdims=True) + RMS_EPS)
    return (y * g.astype(jnp.float32)).astype(x.dtype)


def alibi_slopes():
    h = jnp.arange(1, ATT_HEADS + 1, dtype=jnp.float32)
    return jnp.exp2(-ALIBI_MAX_BIAS * h / ATT_HEADS).reshape(ATT_GROUPS, ATT_HEADS_PER_GROUP)


def dilated_window_attention(q, k, v, slopes, window, dilation):
    b, s, h, dh = q.shape
    span = window // dilation
    n_sub = s // dilation
    nb = -(-n_sub // ATT_BLOCK)
    n_pad = nb * ATT_BLOCK

    def to_blocks(t):
        t = t.reshape(b, n_sub, dilation, h, dh).transpose(0, 2, 3, 1, 4)
        t = jnp.pad(t, ((0, 0), (0, 0), (0, 0), (0, n_pad - n_sub), (0, 0)))
        return t.reshape(b, dilation, h, nb, ATT_BLOCK, dh)

    def with_prev(t):
        prev = jnp.pad(t, ((0, 0), (0, 0), (0, 0), (1, 0), (0, 0), (0, 0)))[:, :, :, :nb]
        return jnp.concatenate([prev, t], axis=4)

    qb = to_blocks(q)
    kb = with_prev(to_blocks(k))
    vb = with_prev(to_blocks(v))
    scores = jnp.einsum('brhnqc,brhnkc->brhnqk', qb, kb,
                        preferred_element_type=jnp.float32) * (dh ** -0.5)
    qi = jnp.arange(ATT_BLOCK)[:, None] + ATT_BLOCK
    kj = jnp.arange(2 * ATT_BLOCK)[None, :]
    delta = qi - kj
    key_idx = (jnp.arange(nb)[:, None, None] - 1) * ATT_BLOCK + kj[None]
    valid = (delta >= 0) & (delta <= span) & (key_idx >= 0)
    bias = -slopes.astype(jnp.float32)[:, None, None] * (delta * dilation).astype(jnp.float32)
    scores = jnp.where(valid[None, None, None], scores + bias[None, None, :, None], -jnp.inf)
    m = jnp.max(scores, axis=-1, keepdims=True)
    e = jnp.exp(scores - m)
    den = jnp.sum(e, axis=-1)
    o = jnp.einsum('brhnqk,brhnkc->brhnqc', e, vb.astype(jnp.float32)) / den[..., None]
    lse = m[..., 0] + jnp.log(den)
    o = o.reshape(b, dilation, h, n_pad, dh)[:, :, :, :n_sub].transpose(0, 3, 1, 2, 4).reshape(b, s, h, dh)
    lse = lse.reshape(b, dilation, h, n_pad)[..., :n_sub].transpose(0, 3, 1, 2).reshape(b, s, h)
    return o, lse


def dilated_mixture_attention(q, k, v):
    b, s = q.shape[:2]
    slopes = alibi_slopes()
    outs, lses = [], []
    for g, (window, dilation) in enumerate(ATT_PATTERNS):
        o, l = dilated_window_attention(q[:, :, g], k[:, :, g], v[:, :, g], slopes[g], window, dilation)
        outs.append(o)
        lses.append(l)
    w = jax.nn.softmax(jnp.stack(lses), axis=0)
    o = jnp.sum(w[..., None] * jnp.stack(outs), axis=0)
    return o.reshape(b, s, ATT_OUT_WIDTH).astype(q.dtype)


def causal_depthwise_conv(x, w, bias):
    k, c = w.shape
    y = lax.conv_general_dilated(x, w[:, None, :], window_strides=(1,), padding=((k - 1, 0),),
                                 dimension_numbers=('NWC', 'WIO', 'NWC'), feature_group_count=c)
    return y + bias


def ssd_chunked(xh, dt, a, bm, cm):
    b, s, g, j, p = xh.shape
    n = bm.shape[-1]
    l = SSD_CHUNK
    c = s // l
    x_dt = (xh * dt[..., None]).reshape(b, c, l, g, j, p)
    a_dt = (dt * a).reshape(b, c, l, g, j).transpose(0, 3, 4, 1, 2)
    bc = bm.reshape(b, c, l, g, n)
    cc = cm.reshape(b, c, l, g, n)
    a_cs = jnp.cumsum(a_dt, axis=-1)
    causal = jnp.tril(jnp.ones((l, l), dtype=bool))
    seg = a_cs[..., :, None] - a_cs[..., None, :]
    cb = jnp.einsum('bclgn,bcsgn->bgcls', cc, bc)
    mix = jnp.exp(jnp.where(causal, seg, -jnp.inf)) * cb[:, :, None]
    y_diag = jnp.einsum('bgjcls,bcsgjp->bclgjp', mix, x_dt)
    decay_states = jnp.exp(a_cs[..., -1:] - a_cs).transpose(0, 3, 4, 1, 2)
    states = jnp.einsum('bcsgn,bcsgjp->bcgjpn', bc, x_dt * decay_states[..., None])
    chunk_decay = jnp.exp(a_cs[..., -1])

    def step(carry, inp):
        st, dec = inp
        return carry * dec[..., None, None] + st, carry

    init = jnp.zeros((b, g, j, p, n), jnp.float32)
    _, prev = lax.scan(step, init, (states.transpose(1, 0, 2, 3, 4, 5), chunk_decay.transpose(3, 0, 1, 2)))
    state_decay = jnp.exp(a_cs).transpose(0, 3, 4, 1, 2)
    y_off = jnp.einsum('bclgn,cbgjpn->bclgjp', cc, prev) * state_decay[..., None]
    return (y_diag + y_off).reshape(b, s, g, j, p)


def mamba2_mixer(z, xbc, dt_raw, conv_w, conv_b, dt_bias, a_log, d_skip, norm_g):
    b, s, _ = z.shape
    f32 = jnp.float32
    xbc = jax.nn.silu(causal_depthwise_conv(xbc, conv_w, conv_b))
    xs, bm, cm = jnp.split(xbc, [SSD_INNER, SSD_INNER + SSD_GROUPS * SSD_STATE], axis=-1)
    dt = jax.nn.softplus(dt_raw.astype(f32) + dt_bias.astype(f32))
    a = -jnp.exp(a_log.astype(f32))
    xh = xs.astype(f32).reshape(b, s, SSD_GROUPS, SSD_HEADS_PER_GROUP, SSD_HEAD_DIM)
    y = ssd_chunked(xh,
                    dt.reshape(b, s, SSD_GROUPS, SSD_HEADS_PER_GROUP),
                    a.reshape(SSD_GROUPS, SSD_HEADS_PER_GROUP),
                    bm.astype(f32).reshape(b, s, SSD_GROUPS, SSD_STATE),
                    cm.astype(f32).reshape(b, s, SSD_GROUPS, SSD_STATE))
    y = y + xh * d_skip.astype(f32).reshape(SSD_GROUPS, SSD_HEADS_PER_GROUP)[:, :, None]
    y = y.reshape(b, s, SSD_INNER).astype(z.dtype)
    return rms_norm(y * jax.nn.silu(z), norm_g)


def hierarchical_moe(h, w_rg, b_rg, w_re, b_re, w_gate, w_up, w_down):
    bsz, s, d = h.shape
    t = h.reshape(bsz * s, d)
    g_logits = (t @ w_rg).astype(jnp.float32) + b_rg.astype(jnp.float32)
    g_prob = jax.nn.softmax(g_logits, axis=-1)
    g_val, g_idx = lax.top_k(g_prob, 1)
    e_logits = ((t @ w_re).astype(jnp.float32) + b_re.astype(jnp.float32)).reshape(
        -1, MOE_GROUPS, MOE_EXPERTS_PER_GROUP)
    within = jnp.einsum('tg,tge->te', jax.nn.one_hot(g_idx[:, 0], MOE_GROUPS, dtype=jnp.float32), e_logits)
    e_val, e_idx = lax.top_k(within, MOE_TOP_K)
    e_w = jax.nn.softmax(e_val, axis=-1)
    expert_id = g_idx * MOE_EXPERTS_PER_GROUP + e_idx
    combine = g_val * jnp.sum(e_w[..., None] * jax.nn.one_hot(expert_id, MOE_EXPERTS, dtype=jnp.float32), axis=1)
    combine = combine.astype(t.dtype)
    out = jnp.zeros_like(t)
    for e in range(MOE_EXPERTS):
        hid = jax.nn.silu(t @ w_gate[e]) * (t @ w_up[e])
        out = out + combine[:, e:e + 1] * (hid @ w_down[e])
    return out.reshape(bsz, s, d)


def setup_inputs(seed: int = 0) -> dict:
    key = jax.random.key(seed)
    ks = jax.random.split(key, 32)
    f32 = jnp.float32

    def nrm(k, shape, fan_in):
        return jax.random.normal(k, shape, f32) * (fan_in ** -0.5)

    def gain(k, shape):
        return 1.0 + 0.05 * jax.random.normal(k, shape, f32)

    dt0 = jnp.exp(jax.random.uniform(ks[6], (DEPTH, SSD_HEADS), f32, math.log(1e-3), math.log(1e-1)))
    return {
        'x': jax.random.normal(ks[0], (BATCH, SEQ, D_MODEL), f32),
        'p': jax.random.normal(ks[1], (DEPTH, BATCH, SEQ, PLE_DIM), f32),
        'norm_mix_g': gain(ks[2], (DEPTH, D_MODEL)),
        'w_in': nrm(ks[3], (DEPTH, D_MODEL, IN_COLS), D_MODEL),
        'conv_w': nrm(ks[4], (DEPTH, SSD_CONV, SSD_CONV_CH), SSD_CONV),
        'conv_b': 0.02 * jax.random.normal(ks[5], (DEPTH, SSD_CONV_CH), f32),
        'dt_bias': dt0 + jnp.log(-jnp.expm1(-dt0)),
        'a_log': jnp.log(jax.random.uniform(ks[7], (DEPTH, SSD_HEADS), f32, 1.0, 16.0)),
        'd_skip': gain(ks[8], (DEPTH, SSD_HEADS)),
        'ssd_norm_g': gain(ks[9], (DEPTH, SSD_INNER)),
        'w_att_branch': nrm(ks[10], (DEPTH, ATT_OUT_WIDTH, D_MODEL), ATT_OUT_WIDTH),
        'w_ssd_branch': nrm(ks[11], (DEPTH, SSD_INNER, D_MODEL), SSD_INNER),
        'w_out': nrm(ks[12], (DEPTH, D_MODEL, D_MODEL), D_MODEL),
        'norm_ffn_g': gain(ks[13], (DEPTH, D_MODEL)),
        'w_router_group': nrm(ks[14], (DEPTH, D_MODEL, MOE_GROUPS), D_MODEL),
        'b_router_group': 0.01 * jax.random.normal(ks[15], (DEPTH, MOE_GROUPS), f32),
        'w_router_expert': nrm(ks[16], (DEPTH, D_MODEL, MOE_EXPERTS), D_MODEL),
        'b_router_expert': 0.01 * jax.random.normal(ks[17], (DEPTH, MOE_EXPERTS), f32),
        'w_exp_gate': nrm(ks[18], (DEPTH, MOE_EXPERTS, D_MODEL, MOE_HIDDEN), D_MODEL),
        'w_exp_up': nrm(ks[19], (DEPTH, MOE_EXPERTS, D_MODEL, MOE_HIDDEN), D_MODEL),
        'w_exp_down': nrm(ks[20], (DEPTH, MOE_EXPERTS, MOE_HIDDEN, D_MODEL), MOE_HIDDEN),
        'norm_ple_g': gain(ks[21], (DEPTH, D_MODEL)),
        'w_ple_gate': nrm(ks[22], (DEPTH, D_MODEL, D_MODEL), D_MODEL),
        'w_ple_proj': nrm(ks[23], (DEPTH, PLE_DIM, D_MODEL), PLE_DIM),
        'final_norm_g': gain(ks[24], (D_MODEL,)),
    }


def reference(x, p, norm_mix_g, w_in, conv_w, conv_b, dt_bias, a_log, d_skip, ssd_norm_g,
              w_att_branch, w_ssd_branch, w_out, norm_ffn_g, w_router_group, b_router_group,
              w_router_expert, b_router_expert, w_exp_gate, w_exp_up, w_exp_down,
              norm_ple_g, w_ple_gate, w_ple_proj, final_norm_g):
    b, s, _ = x.shape
    split_at = list(np.cumsum([ATT_WIDTH, ATT_WIDTH, ATT_WIDTH, SSD_INNER, SSD_CONV_CH, SSD_HEADS, D_MODEL]))
    att_shape = (b, s, ATT_GROUPS, ATT_HEADS_PER_GROUP, ATT_HEAD_DIM)
    for i in range(DEPTH):
        h = rms_norm(x, norm_mix_g[i])
        proj = h @ w_in[i]
        q, k, v, z, xbc, dt_raw, gate_a, gate_s = jnp.split(proj, [int(o) for o in split_at], axis=-1)
        att = dilated_mixture_attention(q.reshape(att_shape), k.reshape(att_shape), v.reshape(att_shape))
        y_att = att @ w_att_branch[i]
        ssd = mamba2_mixer(z, xbc, dt_raw, conv_w[i], conv_b[i], dt_bias[i], a_log[i], d_skip[i], ssd_norm_g[i])
        y_ssd = ssd @ w_ssd_branch[i]
        merged = jax.nn.sigmoid(gate_a) * y_att + jax.nn.sigmoid(gate_s) * y_ssd
        x = x + merged @ w_out[i]
        x = x + hierarchical_moe(rms_norm(x, norm_ffn_g[i]), w_router_group[i], b_router_group[i],
                                 w_router_expert[i], b_router_expert[i],
                                 w_exp_gate[i], w_exp_up[i], w_exp_down[i])
        ple_gate = jax.nn.sigmoid(rms_norm(x, norm_ple_g[i]) @ w_ple_gate[i])
        x = x + ple_gate * (p[i] @ w_ple_proj[i])
    return rms_norm(x, final_norm_g)
```

```python
import functools

import numpy as np
import jax
import jax.numpy as jnp
from jax import lax
from jax.experimental import pallas as pl
from jax.experimental.pallas import tpu as pltpu

F32 = jnp.float32
BF16 = jnp.bfloat16

D_MODEL = 1024
PLE_DIM = 256
RMS_EPS = 1e-6

ATT_PATTERNS = ((128, 1), (512, 4), (2048, 16))
ATT_GROUPS = 3
ATT_HPG = 8
ATT_HEAD_DIM = 64
ATT_WIDTH = ATT_GROUPS * ATT_HPG * ATT_HEAD_DIM
ATT_OUT = ATT_HPG * ATT_HEAD_DIM
ATT_BLOCK = 128
ALIBI_MAX_BIAS = 8.0
QKV_W = 3 * ATT_WIDTH

SSD_INNER = 2048
SSD_HEADS = 32
SSD_GROUPS = 8
SSD_HPG = 4
SSD_HEAD_DIM = 64
SSD_STATE = 128
SSD_CONV = 4
SSD_CHUNK = 128
SSD_CONV_CH = SSD_INNER + 2 * SSD_GROUPS * SSD_STATE
SSD_GW = SSD_HPG * SSD_HEAD_DIM

MOE_GROUPS = 4
MOE_EPG = 4
MOE_EXPERTS = 16
MOE_HIDDEN = 512
MOE_PAIRS = 6
MOE_CLASSES = MOE_GROUPS * MOE_PAIRS

LANES = 128
NEG_BIG = -1e30
VMEM_LIMIT = 56 * 1024 * 1024

TM_IN = 1024
TN_IN = 512
TM_POST = 512
TM_SCAT = 512
TM_EXP = 256
TM_FIN = 512


def _cparams(*sem):
    return pltpu.CompilerParams(dimension_semantics=sem, vmem_limit_bytes=VMEM_LIMIT)


def _sigmoid(x):
    return 1.0 / (1.0 + jnp.exp(-x))


def _rms(x, g):
    ms = jnp.mean(x * x, axis=-1, keepdims=True)
    return x * lax.rsqrt(ms + RMS_EPS) * g


_IN_SEG = (QKV_W // TN_IN, SSD_INNER // TN_IN, SSD_CONV_CH // TN_IN, 2 * D_MODEL // TN_IN)
_IN_START = tuple(int(v) for v in np.cumsum((0,) + _IN_SEG))


def _inproj_kernel(x_ref, g_ref, w_ref, wdt_ref, qkv_ref, z_ref, xbc_ref, gates_ref, dt_ref, h_ref):
    j = pl.program_id(1)

    @pl.when(j == 0)
    def _():
        hb = _rms(x_ref[...], g_ref[...]).astype(BF16)
        h_ref[...] = hb
        dt_ref[...] = jnp.dot(hb, wdt_ref[...], preferred_element_type=F32)

    res = jnp.dot(h_ref[...], w_ref[...], preferred_element_type=F32).astype(BF16)
    for out_ref, lo, hi in zip((qkv_ref, z_ref, xbc_ref, gates_ref), _IN_START[:-1], _IN_START[1:]):
        @pl.when((j >= lo) & (j < hi))
        def _(out_ref=out_ref):
            out_ref[...] = res


def _in_proj(x2d, g, w_main, w_dt):
    t = x2d.shape[0]
    nj = _IN_START[-1]

    def seg_map(lo, n):
        return lambda i, j: (i, jnp.clip(j - lo, 0, n - 1))

    out_shapes = (
        jax.ShapeDtypeStruct((t, QKV_W), BF16),
        jax.ShapeDtypeStruct((t, SSD_INNER), BF16),
        jax.ShapeDtypeStruct((t, SSD_CONV_CH), BF16),
        jax.ShapeDtypeStruct((t, 2 * D_MODEL), BF16),
        jax.ShapeDtypeStruct((t, LANES), F32),
    )
    out_specs = tuple(
        pl.BlockSpec((TM_IN, TN_IN), seg_map(lo, n)) for lo, n in zip(_IN_START[:-1], _IN_SEG)
    ) + (pl.BlockSpec((TM_IN, LANES), lambda i, j: (i, 0)),)
    return pl.pallas_call(
        _inproj_kernel,
        grid=(t // TM_IN, nj),
        in_specs=[
            pl.BlockSpec((TM_IN, D_MODEL), lambda i, j: (i, 0)),
            pl.BlockSpec((1, D_MODEL), lambda i, j: (0, 0)),
            pl.BlockSpec((D_MODEL, TN_IN), lambda i, j: (0, j)),
            pl.BlockSpec((D_MODEL, LANES), lambda i, j: (0, 0)),
        ],
        out_specs=out_specs,
        out_shape=out_shapes,
        scratch_shapes=[pltpu.VMEM((TM_IN, D_MODEL), BF16)],
        compiler_params=_cparams("arbitrary", "arbitrary"),
    )(x2d, g, w_main, w_dt)


def _att_bias_tables():
    h = np.arange(1, ATT_GROUPS * ATT_HPG + 1, dtype=np.float32)
    slopes = np.exp2(-ALIBI_MAX_BIAS * h / (ATT_GROUPS * ATT_HPG)).astype(np.float32).reshape(ATT_GROUPS, ATT_HPG)
    qi = np.arange(ATT_BLOCK)[:, None] + ATT_BLOCK
    kj = np.arange(2 * ATT_BLOCK)[None, :]
    delta = qi - kj
    tabs = []
    for g, (window, dil) in enumerate(ATT_PATTERNS):
        span = window // dil
        band = (delta >= 0) & (delta <= span)
        bias = (-slopes[g][:, None, None] * (delta * dil).astype(np.float32)[None]).astype(np.float32)
        with_prev = np.where(band[None], bias, np.float32(NEG_BIG))
        first = np.where((band & (kj >= ATT_BLOCK))[None], bias, np.float32(NEG_BIG))
        tabs.append(np.stack([first, with_prev]).astype(np.float32))
    return tabs


def _att_one_group(q, k2, v2, bias_fn, o_ref, l_ref):
    lane = lax.broadcasted_iota(jnp.int32, (1, LANES), 1)
    low = lane < ATT_HEAD_DIM
    scale = ATT_HEAD_DIM ** -0.5
    qmask = (jnp.where(low, scale, 0.0).astype(BF16), jnp.where(low, 0.0, scale).astype(BF16))
    for hp in range(ATT_HPG // 2):
        sl = slice(hp * LANES, (hp + 1) * LANES)
        qp, kp, vp = q[:, sl], k2[:, sl], v2[:, sl]
        outs, lses = [], []
        for hh in range(2):
            s = lax.dot_general(qp * qmask[hh], kp, (((1,), (1,)), ((), ())), preferred_element_type=F32)
            s = s + bias_fn(hp * 2 + hh)
            m = jnp.max(s, axis=-1, keepdims=True)
            e = jnp.exp(s - m)
            den = jnp.sum(e, axis=-1, keepdims=True)
            pv = jnp.dot(e.astype(BF16), vp, preferred_element_type=F32)
            outs.append(pv / den)
            lses.append(m + jnp.log(den))
        o_ref[0, :, sl] = jnp.where(low, outs[0], outs[1]).astype(BF16)
        l_ref[0, :, sl] = jnp.where(low, lses[0], lses[1])


def _att_kernel(q1, kc1, kp1, vc1, vp1, q2, kc2, kp2, vc2, vp2, q3, kc3, vc3, t1, t2, t3,
                o1, l1, o2, l2, o3, l3):
    u = pl.program_id(1)
    sel1 = jnp.minimum(u, 1)
    sel2 = jnp.minimum(u % 4, 1)
    _att_one_group(q1[0], jnp.concatenate([kp1[0], kc1[0]], axis=0), jnp.concatenate([vp1[0], vc1[0]], axis=0),
                   lambda h: t1[sel1, h], o1, l1)
    _att_one_group(q2[0], jnp.concatenate([kp2[0], kc2[0]], axis=0), jnp.concatenate([vp2[0], vc2[0]], axis=0),
                   lambda h: t2[sel2, h], o2, l2)
    _att_one_group(q3[0], kc3[0], vc3[0], lambda h: t3[h], o3, l3)


def _attention(qkv, bsz, seq):
    tabs = _att_bias_tables()
    units = seq // ATT_BLOCK
    ncol = QKV_W // ATT_OUT
    in_arrays, in_specs = [], []
    out_shapes, out_specs = [], []
    for g, (_, dil) in enumerate(ATT_PATTERNS):
        n_sub = seq // dil
        nb = n_sub // ATT_BLOCK
        view = qkv.reshape(bsz, n_sub, dil * QKV_W)

        def rn(u, nb=nb):
            return u // nb, u % nb

        def cur(sel, g=g, rn=rn):
            def f(b, u):
                r, n = rn(u)
                return (b, n, r * ncol + sel * ATT_GROUPS + g)
            return f

        def prev(sel, g=g, rn=rn):
            def f(b, u):
                r, n = rn(u)
                return (b, jnp.maximum(n - 1, 0), r * ncol + sel * ATT_GROUPS + g)
            return f

        blk = (1, ATT_BLOCK, ATT_OUT)
        specs = [pl.BlockSpec(blk, cur(0)), pl.BlockSpec(blk, cur(1))]
        if nb > 1:
            specs.append(pl.BlockSpec(blk, prev(1)))
        specs.append(pl.BlockSpec(blk, cur(2)))
        if nb > 1:
            specs.append(pl.BlockSpec(blk, prev(2)))
        in_arrays += [view] * len(specs)
        in_specs += specs

        def omap(b, u, rn=rn):
            r, n = rn(u)
            return (b, n, r)

        out_shapes += [jax.ShapeDtypeStruct((bsz, n_sub, dil * ATT_OUT), BF16),
                       jax.ShapeDtypeStruct((bsz, n_sub, dil * ATT_OUT), F32)]
        out_specs += [pl.BlockSpec(blk, omap), pl.BlockSpec(blk, omap)]
    t1, t2 = jnp.asarray(tabs[0]), jnp.asarray(tabs[1])
    t3 = jnp.asarray(tabs[2][1][:, :, ATT_BLOCK:])
    in_arrays += [t1, t2, t3]
    in_specs += [pl.BlockSpec(t1.shape, lambda b, u: (0, 0, 0, 0)),
                 pl.BlockSpec(t2.shape, lambda b, u: (0, 0, 0, 0)),
                 pl.BlockSpec(t3.shape, lambda b, u: (0, 0, 0))]
    outs = pl.pallas_call(
        _att_kernel,
        grid=(bsz, units),
        in_specs=in_specs,
        out_specs=out_specs,
        out_shape=out_shapes,
        compiler_params=_cparams("arbitrary", "arbitrary"),
    )(*in_arrays)
    return [o.reshape(bsz * seq, ATT_OUT) for o in outs]


def _softplus(x):
    return jnp.maximum(x, 0.0) + jnp.log1p(jnp.exp(-jnp.abs(x)))


def _per_head(vals, head_id):
    out = vals[3]
    for j in (2, 1, 0):
        out = jnp.where(head_id == j, vals[j], out)
    return out


def _ssd_kernel(x_ref, b_ref, c_ref, z_ref, dt_ref, wx_ref, wb_ref, wc_ref, bx_ref, bb_ref, bc_ref,
                dtb_ref, alog_ref, dsk_ref, u_ref, xf_ref, carry_ref, *, n_chunks):
    L = SSD_CHUNK
    pad = 8
    gw = SSD_GW
    xf_ref[0:pad, :] = jnp.zeros((pad, gw + 2 * SSD_STATE), F32)
    xf_ref[pad:, 0:gw] = x_ref[0].astype(F32)
    xf_ref[pad:, gw:gw + SSD_STATE] = b_ref[0].astype(F32)
    xf_ref[pad:, gw + SSD_STATE:] = c_ref[0].astype(F32)
    carry_ref[...] = jnp.zeros(carry_ref.shape, F32)

    w = jnp.concatenate([wx_ref[...], wb_ref[...], wc_ref[...]], axis=1)
    bias = jnp.concatenate([bx_ref[...], bb_ref[...], bc_ref[...]], axis=1)
    a_neg = -jnp.exp(alog_ref[0])
    dtb = dtb_ref[0]
    dsk = dsk_ref[0]
    ri = lax.broadcasted_iota(jnp.int32, (L, L), 0)
    ci = lax.broadcasted_iota(jnp.int32, (L, L), 1)
    upper_incl = (ri <= ci).astype(F32)
    causal = ri >= ci
    head_id = lax.broadcasted_iota(jnp.int32, (1, gw), 1) // SSD_HEAD_DIM

    def body(c, _):
        r0 = pl.multiple_of(c * L, L)
        win = xf_ref[pl.ds(r0, L + pad), :]
        acc = bias
        for kk in range(SSD_CONV):
            sh = SSD_CONV - 1 - kk
            acc = acc + w[kk:kk + 1, :] * win[pad - sh:pad - sh + L, :]
        xc = acc * _sigmoid(acc)
        xs, bm, cm = xc[:, :gw], xc[:, gw:gw + SSD_STATE], xc[:, gw + SSD_STATE:]

        dt = _softplus(dt_ref[0, 0, :, pl.ds(r0, L)] + dtb)
        rows8 = jnp.concatenate([dt * a_neg, dt], axis=0)
        cs8 = jnp.dot(rows8, upper_incl, preferred_element_type=F32, precision=lax.Precision.HIGHEST)
        acs_t = cs8[0:SSD_HPG]
        t8 = jnp.concatenate([dt, acs_t], axis=0)
        cols = jnp.concatenate([t8, jnp.zeros((L - 8, L), F32)], axis=0).T
        heads = range(SSD_HPG)
        dt_bc = _per_head([cols[:, j:j + 1] for j in heads], head_id)
        acs_bc = _per_head([cols[:, 4 + j:5 + j] for j in heads], head_id)
        atot_bc = _per_head([acs_t[j:j + 1, L - 1:L] for j in heads], head_id)

        xdt = xs * dt_bc
        bm_t = bm.T.astype(BF16)
        cm_b = cm.astype(BF16)
        cb = jnp.dot(cm_b, bm_t, preferred_element_type=F32)
        mixes = []
        for j in range(SSD_HPG):
            seg = cols[:, 4 + j:5 + j] - acs_t[j:j + 1, :]
            mixes.append((jnp.exp(jnp.where(causal, seg, -jnp.inf)) * cb).astype(BF16))
        ybig = jnp.dot(jnp.concatenate(mixes, axis=0), xdt.astype(BF16), preferred_element_type=F32)
        y = ybig[3 * L:4 * L]
        for j in (2, 1, 0):
            y = jnp.where(head_id == j, ybig[j * L:(j + 1) * L], y)

        st_new = jnp.dot(bm_t, (xdt * jnp.exp(atot_bc - acs_bc)).astype(BF16), preferred_element_type=F32)
        carry = carry_ref[...]
        y = y + jnp.dot(cm_b, carry.astype(BF16), preferred_element_type=F32) * jnp.exp(acs_bc)
        carry_ref[...] = carry * jnp.exp(atot_bc) + st_new
        y = y + xs * dsk
        zc = z_ref[0, pl.ds(r0, L), :].astype(F32)
        u_ref[0, pl.ds(r0, L), :] = (y * (zc * _sigmoid(zc))).astype(BF16)
        return 0

    lax.fori_loop(0, n_chunks, body, 0)


def _ssd(xbc, z, dt_t, conv_w, conv_b, dt_bias, a_log, d_skip, bsz, seq):
    gw = SSD_GW
    nxb = SSD_INNER // SSD_STATE
    dsk = jnp.repeat(d_skip.astype(F32), SSD_HEAD_DIM).reshape(SSD_GROUPS, 1, gw)
    dtb = dt_bias.astype(F32).reshape(SSD_GROUPS, SSD_HPG, 1)
    alog = a_log.astype(F32).reshape(SSD_GROUPS, SSD_HPG, 1)
    cb2 = conv_b.reshape(1, SSD_CONV_CH)
    x_map = lambda b, g: (b, 0, g)
    bm_map = lambda b, g: (b, 0, nxb + g)
    cm_map = lambda b, g: (b, 0, nxb + SSD_GROUPS + g)
    return pl.pallas_call(
        functools.partial(_ssd_kernel, n_chunks=seq // SSD_CHUNK),
        grid=(bsz, SSD_GROUPS),
        in_specs=[
            pl.BlockSpec((1, seq, gw), x_map),
            pl.BlockSpec((1, seq, SSD_STATE), bm_map),
            pl.BlockSpec((1, seq, SSD_STATE), cm_map),
            pl.BlockSpec((1, seq, gw), x_map),
            pl.BlockSpec((1, 1, SSD_HPG, seq), lambda b, g: (b, g, 0, 0)),
            pl.BlockSpec((SSD_CONV, gw), lambda b, g: (0, g)),
            pl.BlockSpec((SSD_CONV, SSD_STATE), lambda b, g: (0, nxb + g)),
            pl.BlockSpec((SSD_CONV, SSD_STATE), lambda b, g: (0, nxb + SSD_GROUPS + g)),
            pl.BlockSpec((1, gw), lambda b, g: (0, g)),
            pl.BlockSpec((1, SSD_STATE), lambda b, g: (0, nxb + g)),
            pl.BlockSpec((1, SSD_STATE), lambda b, g: (0, nxb + SSD_GROUPS + g)),
            pl.BlockSpec((1, SSD_HPG, 1), lambda b, g: (g, 0, 0)),
            pl.BlockSpec((1, SSD_HPG, 1), lambda b, g: (g, 0, 0)),
            pl.BlockSpec((1, 1, gw), lambda b, g: (g, 0, 0)),
        ],
        out_specs=pl.BlockSpec((1, seq, gw), x_map),
        out_shape=jax.ShapeDtypeStruct((bsz, seq, SSD_INNER), BF16),
        scratch_shapes=[pltpu.VMEM((seq + 8, gw + 2 * SSD_STATE), F32),
                        pltpu.VMEM((SSD_STATE, gw), F32)],
        compiler_params=_cparams("arbitrary", "arbitrary"),
    )(xbc, xbc, xbc, z, dt_t, conv_w, conv_w, conv_w, cb2, cb2, cb2, dtb, alog, dsk)


def _first_index_of_max(vals, lane_f):
    m = jnp.max(vals, axis=-1, keepdims=True)
    idx = jnp.min(jnp.where(vals == m, lane_f, float(LANES)), axis=-1, keepdims=True)
    return m, idx


def _post_kernel(o1, l1, o2, l2, o3, l3, u_ref, gates_ref, x_ref, watt_ref, wssd_ref, wout_ref,
                 gssd_ref, gffn_ref, wr_ref, br_ref, x1_ref, meta_ref, cnt_ref, run_ref):
    i = pl.program_id(0)

    @pl.when(i == 0)
    def _():
        run_ref[...] = jnp.zeros(run_ref.shape, F32)

    la, lb, lc = l1[...], l2[...], l3[...]
    lm = jnp.maximum(jnp.maximum(la, lb), lc)
    ea, eb, ec = jnp.exp(la - lm), jnp.exp(lb - lm), jnp.exp(lc - lm)
    att = (ea * o1[...].astype(F32) + eb * o2[...].astype(F32) + ec * o3[...].astype(F32)) / (ea + eb + ec)
    y_att = jnp.dot(att.astype(BF16), watt_ref[...], preferred_element_type=F32)

    ssd = _rms(u_ref[...].astype(F32), gssd_ref[...])
    y_ssd = jnp.dot(ssd.astype(BF16), wssd_ref[...], preferred_element_type=F32)

    gates = gates_ref[...].astype(F32)
    merged = _sigmoid(gates[:, :D_MODEL]) * y_att + _sigmoid(gates[:, D_MODEL:]) * y_ssd
    x1 = x_ref[...] + jnp.dot(merged.astype(BF16), wout_ref[...], preferred_element_type=F32)
    x1_ref[...] = x1

    h2 = _rms(x1, gffn_ref[...])
    logits = jnp.dot(h2, wr_ref[...], preferred_element_type=F32, precision=lax.Precision.HIGHEST) + br_ref[...]
    tm = logits.shape[0]
    lane = lax.broadcasted_iota(jnp.int32, (tm, LANES), 1)
    lane_f = lane.astype(F32)
    ninf = -jnp.inf
    gl = jnp.where(lane < MOE_GROUPS, logits, ninf)
    gmax, gidx = _first_index_of_max(gl, lane_f)
    g_val = 1.0 / jnp.sum(jnp.exp(gl - gmax), axis=-1, keepdims=True)
    base = MOE_GROUPS + MOE_EPG * gidx
    el = jnp.where((lane_f >= base) & (lane_f < base + MOE_EPG), logits, ninf)
    e1, i1 = _first_index_of_max(el, lane_f)
    e2, i2 = _first_index_of_max(jnp.where(lane_f == i1, ninf, el), lane_f)
    t2 = jnp.exp(e2 - e1)
    w1 = g_val / (1.0 + t2)
    w2 = g_val * t2 / (1.0 + t2)
    a1, a2 = i1 - base, i2 - base
    lo, hi = jnp.minimum(a1, a2), jnp.maximum(a1, a2)
    c_lo = jnp.where(a1 < a2, w1, w2)
    c_hi = jnp.where(a1 < a2, w2, w1)
    pair = lo * (7.0 - lo) * 0.5 + (hi - lo - 1.0)
    cls = gidx * MOE_PAIRS + pair

    onehot = (lane_f == cls)
    oh_b = jnp.where(onehot, 1.0, 0.0).astype(BF16)
    rr = lax.broadcasted_iota(jnp.int32, (tm, tm), 0)
    cc = lax.broadcasted_iota(jnp.int32, (tm, tm), 1)
    strict = jnp.where(rr > cc, 1.0, 0.0).astype(BF16)
    prefix = jnp.dot(strict, oh_b, preferred_element_type=F32) + run_ref[...]
    rank = jnp.sum(jnp.where(onehot, prefix, 0.0), axis=-1, keepdims=True)
    run = run_ref[...] + jnp.sum(oh_b.astype(F32), axis=0, keepdims=True)
    run_ref[...] = run
    cnt_ref[...] = jnp.broadcast_to(run, cnt_ref.shape)

    meta = jnp.where(lane == 0, cls, jnp.where(lane == 1, rank, jnp.where(lane == 2, c_lo,
                     jnp.where(lane == 3, c_hi, 0.0))))
    meta_ref[...] = meta


def _post(o1, l1, o2, l2, o3, l3, u, gates, x2d, w_att, w_ssd, w_out, g_ssd, g_ffn, w_r, b_r):
    t = x2d.shape[0]
    tm = TM_POST
    row = lambda w: pl.BlockSpec((tm, w), lambda i: (i, 0))
    const = lambda a: pl.BlockSpec(a.shape, lambda i: (0,) * a.ndim)
    return pl.pallas_call(
        _post_kernel,
        grid=(t // tm,),
        in_specs=[row(ATT_OUT)] * 6 + [row(SSD_INNER), row(2 * D_MODEL), row(D_MODEL),
                                       const(w_att), const(w_ssd), const(w_out), const(g_ssd), const(g_ffn),
                                       const(w_r), const(b_r)],
        out_specs=(row(D_MODEL), row(LANES), pl.BlockSpec((8, LANES), lambda i: (0, 0))),
        out_shape=(jax.ShapeDtypeStruct((t, D_MODEL), F32),
                   jax.ShapeDtypeStruct((t, LANES), F32),
                   jax.ShapeDtypeStruct((8, LANES), F32)),
        scratch_shapes=[pltpu.VMEM((1, LANES), F32)],
        compiler_params=_cparams("arbitrary"),
    )(o1, l1, o2, l2, o3, l3, u, gates, x2d, w_att, w_ssd, w_out, g_ssd, g_ffn, w_r, b_r)


def _scatter_kernel(pos_ref, x1_ref, g_ref, xs_in_ref, xs_ref, h_ref, sem):
    del xs_in_ref
    i = pl.program_id(0)
    tm = h_ref.shape[0]
    h_ref[...] = _rms(x1_ref[...], g_ref[...])

    def row_copy(r):
        return pltpu.make_async_copy(h_ref.at[pl.ds(r, 1)], xs_ref.at[pl.ds(pos_ref[i * tm + r], 1)], sem)

    def issue(r, _):
        row_copy(r).start()
        return 0

    lax.fori_loop(0, tm, issue, 0)
    pltpu.make_async_copy(h_ref, xs_ref.at[pl.ds(0, tm)], sem).wait()


def _scatter_rows(pos, x1, g_ffn, xs_init):
    t = x1.shape[0]
    tm = TM_SCAT
    grid_spec = pltpu.PrefetchScalarGridSpec(
        num_scalar_prefetch=1,
        grid=(t // tm,),
        in_specs=[pl.BlockSpec((tm, D_MODEL), lambda i, pos: (i, 0)),
                  pl.BlockSpec((1, D_MODEL), lambda i, pos: (0, 0)),
                  pl.BlockSpec(memory_space=pl.ANY)],
        out_specs=pl.BlockSpec(memory_space=pl.ANY),
        scratch_shapes=[pltpu.VMEM((tm, D_MODEL), F32), pltpu.SemaphoreType.DMA(())],
    )
    return pl.pallas_call(
        _scatter_kernel,
        grid_spec=grid_spec,
        out_shape=jax.ShapeDtypeStruct(xs_init.shape, F32),
        input_output_aliases={3: 0},
        compiler_params=_cparams("arbitrary"),
    )(pos, x1, g_ffn, xs_init)


def _pack_pair(a, b):
    ua = lax.bitcast_convert_type(a.astype(BF16).astype(F32), jnp.uint32)
    ub = lax.bitcast_convert_type(b.astype(BF16).astype(F32), jnp.uint32)
    return (ua & jnp.uint32(0xFFFF0000)) | (ub >> 16)


def _unpack_pair(p):
    a = lax.bitcast_convert_type(p & jnp.uint32(0xFFFF0000), F32)
    b = lax.bitcast_convert_type(p << 16, F32)
    return a, b


def _expert_kernel(nused_ref, elo_ref, ehi_ref, xs_ref, wg_lo, wu_lo, wd_lo, wg_hi, wu_hi, wd_hi, ys_ref):
    del elo_ref, ehi_ref
    i = pl.program_id(0)

    @pl.when(i < nused_ref[0])
    def _():
        xb = xs_ref[...].astype(BF16)

        def mlp(wg, wu, wd):
            gt = jnp.dot(xb, wg[0], preferred_element_type=F32)
            up = jnp.dot(xb, wu[0], preferred_element_type=F32)
            hid = (gt * _sigmoid(gt)) * up
            return jnp.dot(hid.astype(BF16), wd[0], preferred_element_type=F32)

        ys_ref[...] = _pack_pair(mlp(wg_lo, wu_lo, wd_lo), mlp(wg_hi, wu_hi, wd_hi))

    @pl.when(i >= nused_ref[0])
    def _():
        ys_ref[...] = jnp.zeros(ys_ref.shape, jnp.uint32)


def _experts(n_used, tile_elo, tile_ehi, xs, w_gate, w_up, w_down):
    p_rows = xs.shape[0]
    tm = TM_EXP
    n_tiles = p_rows // tm

    def row_map(i, nu, elo, ehi):
        return (jnp.minimum(i, nu[0] - 1), 0)

    def wmap(which):
        def f(i, nu, elo, ehi):
            e = (elo, ehi)[which]
            return (e[jnp.minimum(i, nu[0] - 1)], 0, 0)
        return f

    wspec_in = lambda which: pl.BlockSpec((1, D_MODEL, MOE_HIDDEN), wmap(which))
    wspec_out = lambda which: pl.BlockSpec((1, MOE_HIDDEN, D_MODEL), wmap(which))
    grid_spec = pltpu.PrefetchScalarGridSpec(
        num_scalar_prefetch=3,
        grid=(n_tiles,),
        in_specs=[pl.BlockSpec((tm, D_MODEL), row_map),
                  wspec_in(0), wspec_in(0), wspec_out(0), wspec_in(1), wspec_in(1), wspec_out(1)],
        out_specs=pl.BlockSpec((tm, D_MODEL), lambda i, nu, elo, ehi: (i, 0)),
    )
    return pl.pallas_call(
        _expert_kernel,
        grid_spec=grid_spec,
        out_shape=jax.ShapeDtypeStruct((p_rows, D_MODEL), jnp.uint32),
        compiler_params=_cparams("arbitrary"),
    )(n_used, tile_elo, tile_ehi, xs, w_gate, w_up, w_down, w_gate, w_up, w_down)


def _final_kernel(pos_ref, x1_ref, meta_ref, p_ref, ys_ref, gple_ref, wpg_ref, wpp_ref, gfin_ref,
                  out_ref, buf_ref, sem, *, n_steps):
    i = pl.program_id(0)
    tm = x1_ref.shape[0]

    def issue_tile(step, slot):
        def issue(r, _):
            pltpu.make_async_copy(ys_ref.at[pl.ds(pos_ref[step * tm + r], 1)],
                                  buf_ref.at[slot, pl.ds(r, 1)], sem.at[slot]).start()
            return 0
        lax.fori_loop(0, tm, issue, 0)

    @pl.when(i == 0)
    def _():
        issue_tile(0, 0)

    slot = i % 2

    @pl.when(i + 1 < n_steps)
    def _():
        issue_tile(i + 1, 1 - slot)

    pltpu.make_async_copy(ys_ref.at[pl.ds(0, tm)], buf_ref.at[slot], sem.at[slot]).wait()

    y_lo, y_hi = _unpack_pair(buf_ref[slot])
    meta = meta_ref[...]
    x2 = x1_ref[...] + meta[:, 2:3] * y_lo + meta[:, 3:4] * y_hi
    hn = _rms(x2, gple_ref[...])
    gate = _sigmoid(jnp.dot(hn.astype(BF16), wpg_ref[...], preferred_element_type=F32))
    pp = jnp.dot(p_ref[...].astype(BF16), wpp_ref[...], preferred_element_type=F32)
    out_ref[...] = _rms(x2 + gate * pp, gfin_ref[...])


def _final(pos, x1, meta, p2d, ys, g_ple, w_pg, w_pp, g_fin):
    t = x1.shape[0]
    tm = TM_FIN
    n_steps = t // tm
    row = lambda w: pl.BlockSpec((tm, w), lambda i, pos: (i, 0))
    const = lambda a: pl.BlockSpec(a.shape, lambda i, pos: (0,) * a.ndim)
    grid_spec = pltpu.PrefetchScalarGridSpec(
        num_scalar_prefetch=1,
        grid=(n_steps,),
        in_specs=[row(D_MODEL), row(LANES), row(PLE_DIM), pl.BlockSpec(memory_space=pl.ANY),
                  const(g_ple), const(w_pg), const(w_pp), const(g_fin)],
        out_specs=row(D_MODEL),
        scratch_shapes=[pltpu.VMEM((2, tm, D_MODEL), jnp.uint32), pltpu.SemaphoreType.DMA((2,))],
    )
    return pl.pallas_call(
        functools.partial(_final_kernel, n_steps=n_steps),
        grid_spec=grid_spec,
        out_shape=jax.ShapeDtypeStruct((t, D_MODEL), F32),
        compiler_params=_cparams("arbitrary"),
    )(pos, x1, meta, p2d, ys, g_ple, w_pg, w_pp, g_fin)


_PAIR_LO = np.array([0, 0, 0, 1, 1, 2], np.int32)
_PAIR_HI = np.array([1, 2, 3, 2, 3, 3], np.int32)


def _routing_tables(meta, counts_f, n_tiles):
    cls = meta[:, 0].astype(jnp.int32)
    rank = meta[:, 1].astype(jnp.int32)
    counts = counts_f[0, :MOE_CLASSES].astype(jnp.int32)
    tiles_per = (counts + TM_EXP - 1) // TM_EXP
    tile_end = jnp.cumsum(tiles_per)
    tile_start = tile_end - tiles_per
    pos = (tile_start * TM_EXP)[cls] + rank
    n_used = tile_end[-1:]
    tile_cls = jnp.minimum(jnp.searchsorted(tile_end, jnp.arange(n_tiles, dtype=jnp.int32), side="right"),
                           MOE_CLASSES - 1).astype(jnp.int32)
    grp = tile_cls // MOE_PAIRS
    pair = tile_cls % MOE_PAIRS
    tile_elo = grp * MOE_EPG + jnp.asarray(_PAIR_LO)[pair]
    tile_ehi = grp * MOE_EPG + jnp.asarray(_PAIR_HI)[pair]
    return pos.astype(jnp.int32), n_used.astype(jnp.int32), tile_elo.astype(jnp.int32), tile_ehi.astype(jnp.int32)


def kernel(x, p, norm_mix_g, w_in, conv_w, conv_b, dt_bias, a_log, d_skip, ssd_norm_g, w_att_branch,
           w_ssd_branch, w_out, norm_ffn_g, w_router_group, b_router_group, w_router_expert,
           b_router_expert, w_exp_gate, w_exp_up, w_exp_down, norm_ple_g, w_ple_gate, w_ple_proj,
           final_norm_g):
    bsz, seq, _ = x.shape
    t = bsz * seq
    assert w_in.shape[0] == 1, "single-layer block"
    assert seq // ATT_PATTERNS[-1][1] == ATT_BLOCK and t % TM_IN == 0
    x2d = x.reshape(t, D_MODEL)

    wi = w_in[0]
    c_dt = QKV_W + SSD_INNER + SSD_CONV_CH
    w_main = jnp.concatenate([wi[:, :c_dt], wi[:, c_dt + SSD_HEADS:]], axis=1).astype(BF16)
    w_dt = jnp.pad(wi[:, c_dt:c_dt + SSD_HEADS], ((0, 0), (0, LANES - SSD_HEADS))).astype(BF16)
    row = lambda v: v.reshape(1, -1).astype(F32)

    qkv, z, xbc, gates, dt_raw = _in_proj(x2d, row(norm_mix_g[0]), w_main, w_dt)

    o1, l1, o2, l2, o3, l3 = _attention(qkv.reshape(bsz, seq, QKV_W), bsz, seq)

    dt_t = dt_raw[:, :SSD_HEADS].reshape(bsz, seq, SSD_GROUPS, SSD_HPG).transpose(0, 2, 3, 1)
    u = _ssd(xbc.reshape(bsz, seq, SSD_CONV_CH), z.reshape(bsz, seq, SSD_INNER), dt_t,
             conv_w[0], conv_b[0], dt_bias[0], a_log[0], d_skip[0], bsz, seq).reshape(t, SSD_INNER)

    w_r = jnp.pad(jnp.concatenate([w_router_group[0], w_router_expert[0]], axis=1),
                  ((0, 0), (0, LANES - MOE_GROUPS - MOE_EXPERTS))).astype(F32)
    b_r = jnp.pad(jnp.concatenate([b_router_group[0], b_router_expert[0]]),
                  (0, LANES - MOE_GROUPS - MOE_EXPERTS)).reshape(1, LANES).astype(F32)
    x1, meta, counts = _post(o1, l1, o2, l2, o3, l3, u, gates, x2d,
                             w_att_branch[0].astype(BF16), w_ssd_branch[0].astype(BF16), w_out[0].astype(BF16),
                             row(ssd_norm_g[0]), row(norm_ffn_g[0]), w_r, b_r)

    n_tiles = t // TM_EXP + MOE_CLASSES
    pos, n_used, tile_elo, tile_ehi = _routing_tables(meta, counts, n_tiles)
    xs = _scatter_rows(pos, x1, row(norm_ffn_g[0]), jnp.zeros((n_tiles * TM_EXP, D_MODEL), F32))
    ys = _experts(n_used, tile_elo, tile_ehi, xs,
                  w_exp_gate[0].astype(BF16), w_exp_up[0].astype(BF16), w_exp_down[0].astype(BF16))
    out = _final(pos, x1, meta, p[0].reshape(t, PLE_DIM), ys, row(norm_ple_g[0]),
                 w_ple_gate[0].astype(BF16), w_ple_proj[0].astype(BF16), row(final_norm_g))
    return out.reshape(bsz, seq, D_MODEL)
```

```python
import functools

import numpy as np
import jax
import jax.numpy as jnp
from jax import lax
from jax.experimental import pallas as pl
from jax.experimental.pallas import tpu as pltpu

F32 = jnp.float32
BF16 = jnp.bfloat16

D_MODEL = 1024
PLE_DIM = 256
RMS_EPS = 1e-6

ATT_PATTERNS = ((128, 1), (512, 4), (2048, 16))
ATT_GROUPS = 3
ATT_HPG = 8
ATT_HEAD_DIM = 64
ATT_WIDTH = ATT_GROUPS * ATT_HPG * ATT_HEAD_DIM
ATT_OUT = ATT_HPG * ATT_HEAD_DIM
ATT_BLOCK = 128
ALIBI_MAX_BIAS = 8.0
QKV_W = 3 * ATT_WIDTH

SSD_INNER = 2048
SSD_HEADS = 32
SSD_GROUPS = 8
SSD_HPG = 4
SSD_HEAD_DIM = 64
SSD_STATE = 128
SSD_CONV = 4
SSD_CHUNK = 128
SSD_CONV_CH = SSD_INNER + 2 * SSD_GROUPS * SSD_STATE
SSD_GW = SSD_HPG * SSD_HEAD_DIM

MOE_GROUPS = 4
MOE_EPG = 4
MOE_EXPERTS = 16
MOE_HIDDEN = 512
MOE_PAIRS = 6
MOE_CLASSES = MOE_GROUPS * MOE_PAIRS

LANES = 128
NEG_BIG = -1e30
VMEM_LIMIT = 56 * 1024 * 1024

TM_IN = 1024
TN_IN = 512
TM_POST = 512
TM_SCAT = 512
TM_EXP = 256
TM_FIN = 512


def _cparams(*sem):
    return pltpu.CompilerParams(dimension_semantics=sem, vmem_limit_bytes=VMEM_LIMIT)


def _sigmoid(x):
    return 1.0 / (1.0 + jnp.exp(-x))


def _rms(x, g):
    ms = jnp.mean(x * x, axis=-1, keepdims=True)
    return x * lax.rsqrt(ms + RMS_EPS) * g


_IN_SEG = (3, 3, 3, SSD_INNER // TN_IN, SSD_CONV_CH // TN_IN, 2 * D_MODEL // TN_IN)
_IN_START = tuple(int(v) for v in np.cumsum((0,) + _IN_SEG))


def _inproj_kernel(x_ref, g_ref, w_ref, wdt_ref, qkv1_ref, qkv2_ref, qkv3_ref, z_ref, xbc_ref, gates_ref,
                   dt_ref, h_ref, hcol_ref):
    j = pl.program_id(1)
    tm = x_ref.shape[0]

    @pl.when(j == 0)
    def _():
        h = _rms(x_ref[...], g_ref[...])
        hb = h.astype(BF16)
        h_ref[0] = hb
        dt_ref[...] = jnp.dot(hb, wdt_ref[...], preferred_element_type=F32)
        ncb = hcol_ref.shape[0]
        for c in range(ncb):
            hcol_ref[c] = h[:, c * LANES:(c + 1) * LANES]
        for slot, (_, dil) in enumerate(ATT_PATTERNS[1:], start=1):
            rows = tm // dil
            for r in range(dil):
                for c in range(ncb):
                    h_ref[slot, r * rows:(r + 1) * rows, c * LANES:(c + 1) * LANES] = (
                        hcol_ref[c, pl.ds(r, rows, stride=dil), :].astype(BF16))

    slot = jnp.where(j < _IN_START[1], 0, jnp.where(j < _IN_START[2], 1, jnp.where(j < _IN_START[3], 2, 0)))
    res = jnp.dot(h_ref[slot], w_ref[...], preferred_element_type=F32).astype(BF16)

    def store(k, fn):
        @pl.when((j >= _IN_START[k]) & (j < _IN_START[k + 1]))
        def _():
            fn()

    def store_grouped(ref, dil):
        def fn():
            ref[0] = res.reshape(dil, tm // dil, res.shape[1])
        return fn

    def store_plain(ref):
        def fn():
            ref[...] = res
        return fn

    store(0, store_plain(qkv1_ref))
    store(1, store_grouped(qkv2_ref, ATT_PATTERNS[1][1]))
    store(2, store_grouped(qkv3_ref, ATT_PATTERNS[2][1]))
    store(3, store_plain(z_ref))
    store(4, store_plain(xbc_ref))
    store(5, store_plain(gates_ref))


def _in_proj(x2d, g, w_main, w_dt, bsz, seq):
    t = x2d.shape[0]
    nj = _IN_START[-1]
    per_seq = seq // TM_IN
    d2, d3 = ATT_PATTERNS[1][1], ATT_PATTERNS[2][1]

    def seg_map(k):
        return lambda i, j: (i, jnp.clip(j - _IN_START[k], 0, _IN_SEG[k] - 1))

    def grp_map(k):
        return lambda i, j: (i // per_seq, 0, i % per_seq, jnp.clip(j - _IN_START[k], 0, _IN_SEG[k] - 1))

    out_shapes = (
        jax.ShapeDtypeStruct((t, ATT_WIDTH), BF16),
        jax.ShapeDtypeStruct((bsz, d2, seq // d2, ATT_WIDTH), BF16),
        jax.ShapeDtypeStruct((bsz, d3, seq // d3, ATT_WIDTH), BF16),
        jax.ShapeDtypeStruct((t, SSD_INNER), BF16),
        jax.ShapeDtypeStruct((t, SSD_CONV_CH), BF16),
        jax.ShapeDtypeStruct((t, 2 * D_MODEL), BF16),
        jax.ShapeDtypeStruct((t, LANES), F32),
    )
    out_specs = (
        pl.BlockSpec((TM_IN, TN_IN), seg_map(0)),
        pl.BlockSpec((1, d2, TM_IN // d2, TN_IN), grp_map(1)),
        pl.BlockSpec((1, d3, TM_IN // d3, TN_IN), grp_map(2)),
        pl.BlockSpec((TM_IN, TN_IN), seg_map(3)),
        pl.BlockSpec((TM_IN, TN_IN), seg_map(4)),
        pl.BlockSpec((TM_IN, TN_IN), seg_map(5)),
        pl.BlockSpec((TM_IN, LANES), lambda i, j: (i, 0)),
    )
    return pl.pallas_call(
        _inproj_kernel,
        grid=(t // TM_IN, nj),
        in_specs=[
            pl.BlockSpec((TM_IN, D_MODEL), lambda i, j: (i, 0)),
            pl.BlockSpec((1, D_MODEL), lambda i, j: (0, 0)),
            pl.BlockSpec((D_MODEL, TN_IN), lambda i, j: (0, j)),
            pl.BlockSpec((D_MODEL, LANES), lambda i, j: (0, 0)),
        ],
        out_specs=out_specs,
        out_shape=out_shapes,
        scratch_shapes=[pltpu.VMEM((3, TM_IN, D_MODEL), BF16),
                        pltpu.VMEM((D_MODEL // LANES, TM_IN, LANES), F32)],
        compiler_params=_cparams("arbitrary", "arbitrary"),
    )(x2d, g, w_main, w_dt)


def _att_bias_tables():
    h = np.arange(1, ATT_GROUPS * ATT_HPG + 1, dtype=np.float32)
    slopes = np.exp2(-ALIBI_MAX_BIAS * h / (ATT_GROUPS * ATT_HPG)).astype(np.float32).reshape(ATT_GROUPS, ATT_HPG)
    qi = np.arange(ATT_BLOCK)[:, None] + ATT_BLOCK
    kj = np.arange(2 * ATT_BLOCK)[None, :]
    delta = qi - kj
    tabs = []
    for g, (window, dil) in enumerate(ATT_PATTERNS):
        span = window // dil
        band = (delta >= 0) & (delta <= span)
        bias = (-slopes[g][:, None, None] * (delta * dil).astype(np.float32)[None]).astype(np.float32)
        with_prev = np.where(band[None], bias, np.float32(NEG_BIG))
        first = np.where((band & (kj >= ATT_BLOCK))[None], bias, np.float32(NEG_BIG))
        tabs.append(np.stack([first, with_prev]).astype(np.float32))
    return tabs[0], tabs[1], tabs[2][1][:, :, ATT_BLOCK:]


def _att_unit(q, k2, v2, tab_fn):
    lane = lax.broadcasted_iota(jnp.int32, (1, LANES), 1)
    low = lane < ATT_HEAD_DIM
    scale = ATT_HEAD_DIM ** -0.5
    qmask = (jnp.where(low, scale, 0.0).astype(BF16), jnp.where(low, 0.0, scale).astype(BF16))
    outs, lses = [], []
    for hh in range(2):
        s = lax.dot_general(q * qmask[hh], k2, (((1,), (1,)), ((), ())), preferred_element_type=F32)
        s = s + tab_fn(hh)
        m = jnp.max(s, axis=-1, keepdims=True)
        e = jnp.exp(s - m)
        den = jnp.sum(e, axis=-1, keepdims=True)
        pv = jnp.dot(e.astype(BF16), v2, preferred_element_type=F32)
        outs.append(pv / den)
        lses.append(m + jnp.log(den))
    return jnp.where(low, outs[0], outs[1]), jnp.where(low, lses[0], lses[1])


def _att_kernel(q1, k1, v1, q2, k2, v2, q3, k3, v3, t1, t2, t3, out_ref,
                o1, l1, o2p, l2p, o2n, l2n, o3p, l3p, o3n, l3n, *, seq):
    blk = ATT_BLOCK
    d2, d3 = ATT_PATTERNS[1][1], ATT_PATTERNS[2][1]
    nb1, nb2 = seq // blk, seq // d2 // blk

    def rows(n):
        return pl.ds(pl.multiple_of(n * blk, blk), blk)

    def g1_body(n, _):
        cur, prv = rows(n), rows(jnp.maximum(n - 1, 0))
        sel = jnp.minimum(n, 1)
        kk = jnp.concatenate([k1[0, prv, :], k1[0, cur, :]], axis=0)
        vv = jnp.concatenate([v1[0, prv, :], v1[0, cur, :]], axis=0)
        o, l = _att_unit(q1[0, cur, :], kk, vv, lambda hh: t1[sel, hh])
        o1[cur, :] = o
        l1[cur, :] = l
        return 0

    lax.fori_loop(0, nb1, g1_body, 0)

    def g2_body(u, _):
        r, n = u // nb2, u % nb2
        cur, prv = rows(n), rows(jnp.maximum(n - 1, 0))
        sel = jnp.minimum(n, 1)
        kk = jnp.concatenate([k2[0, r, prv, :], k2[0, r, cur, :]], axis=0)
        vv = jnp.concatenate([v2[0, r, prv, :], v2[0, r, cur, :]], axis=0)
        o, l = _att_unit(q2[0, r, cur, :], kk, vv, lambda hh: t2[sel, hh])
        o2p[rows(u), :] = o
        l2p[rows(u), :] = l
        return 0

    lax.fori_loop(0, d2 * nb2, g2_body, 0)

    def g3_body(r, _):
        o, l = _att_unit(q3[0, r], k3[0, r], v3[0, r], lambda hh: t3[hh])
        o3p[rows(r), :] = o
        l3p[rows(r), :] = l
        return 0

    lax.fori_loop(0, d3, g3_body, 0)

    for dil, pairs in ((d2, ((o2p, o2n), (l2p, l2n))), (d3, ((o3p, o3n), (l3p, l3n)))):
        n_sub = seq // dil
        for r in range(dil):
            for src, dst in pairs:
                dst[pl.ds(r, n_sub, stride=dil), :] = src[r * n_sub:(r + 1) * n_sub, :]

    mrows = 2 * blk

    def merge(c, _):
        rr = pl.ds(pl.multiple_of(c * mrows, mrows), mrows)
        la, lb, lc = l1[rr, :], l2n[rr, :], l3n[rr, :]
        lm = jnp.maximum(jnp.maximum(la, lb), lc)
        ea, eb, ec = jnp.exp(la - lm), jnp.exp(lb - lm), jnp.exp(lc - lm)
        att = (ea * o1[rr, :] + eb * o2n[rr, :] + ec * o3n[rr, :]) / (ea + eb + ec)
        out_ref[0, rr, :] = att.astype(BF16)
        return 0

    lax.fori_loop(0, seq // mrows, merge, 0)


def _attention(qkv1, qkv2, qkv3, bsz, seq):
    t1, t2, t3 = (jnp.asarray(t) for t in _att_bias_tables())
    d2, d3 = ATT_PATTERNS[1][1], ATT_PATTERNS[2][1]
    npair = ATT_HPG // 2
    in_arrays, in_specs = [], []
    for arr, lead in ((qkv1, ()), (qkv2, (d2,)), (qkv3, (d3,))):
        n_rows = arr.shape[-2]
        for sel in range(3):
            zeros = (0,) * len(lead)
            in_arrays.append(arr)
            in_specs.append(pl.BlockSpec((1,) + lead + (n_rows, LANES),
                                         lambda hp, b, sel=sel, zeros=zeros: (b,) + zeros + (0, sel * npair + hp)))
    in_arrays += [t1, t2, t3]
    in_specs += [pl.BlockSpec((2, 2, ATT_BLOCK, 2 * ATT_BLOCK), lambda hp, b: (0, hp, 0, 0)),
                 pl.BlockSpec((2, 2, ATT_BLOCK, 2 * ATT_BLOCK), lambda hp, b: (0, hp, 0, 0)),
                 pl.BlockSpec((2, ATT_BLOCK, ATT_BLOCK), lambda hp, b: (hp, 0, 0))]
    return pl.pallas_call(
        functools.partial(_att_kernel, seq=seq),
        grid=(npair, bsz),
        in_specs=in_specs,
        out_specs=pl.BlockSpec((1, seq, LANES), lambda hp, b: (b, 0, hp)),
        out_shape=jax.ShapeDtypeStruct((bsz, seq, ATT_OUT), BF16),
        scratch_shapes=[pltpu.VMEM((seq, LANES), F32)] * 10,
        compiler_params=_cparams("arbitrary", "arbitrary"),
    )(*in_arrays)


def _softplus(x):
    return jnp.maximum(x, 0.0) + jnp.log1p(jnp.exp(-jnp.abs(x)))


def _per_head(vals, head_id):
    out = vals[3]
    for j in (2, 1, 0):
        out = jnp.where(head_id == j, vals[j], out)
    return out


def _ssd_kernel(x_ref, b_ref, c_ref, z_ref, dt_ref, wx_ref, wb_ref, wc_ref, bx_ref, bb_ref, bc_ref,
                dtb_ref, alog_ref, dsk_ref, u_ref, xf_ref, carry_ref, *, n_chunks):
    L = SSD_CHUNK
    pad = 8
    gw = SSD_GW
    xf_ref[0:pad, :] = jnp.zeros((pad, gw + 2 * SSD_STATE), F32)
    xf_ref[pad:, 0:gw] = x_ref[0].astype(F32)
    xf_ref[pad:, gw:gw + SSD_STATE] = b_ref[0].astype(F32)
    xf_ref[pad:, gw + SSD_STATE:] = c_ref[0].astype(F32)
    carry_ref[...] = jnp.zeros(carry_ref.shape, F32)

    w = jnp.concatenate([wx_ref[...], wb_ref[...], wc_ref[...]], axis=1)
    bias = jnp.concatenate([bx_ref[...], bb_ref[...], bc_ref[...]], axis=1)
    a_neg = -jnp.exp(alog_ref[0])
    dtb = dtb_ref[0]
    dsk = dsk_ref[0]
    ri = lax.broadcasted_iota(jnp.int32, (L, L), 0)
    ci = lax.broadcasted_iota(jnp.int32, (L, L), 1)
    upper_incl = (ri <= ci).astype(F32)
    causal = ri >= ci
    head_id = lax.broadcasted_iota(jnp.int32, (1, gw), 1) // SSD_HEAD_DIM

    def body(c, _):
        r0 = pl.multiple_of(c * L, L)
        win = xf_ref[pl.ds(r0, L + pad), :]
        acc = bias
        for kk in range(SSD_CONV):
            sh = SSD_CONV - 1 - kk
            acc = acc + w[kk:kk + 1, :] * win[pad - sh:pad - sh + L, :]
        xc = acc * _sigmoid(acc)
        xs, bm, cm = xc[:, :gw], xc[:, gw:gw + SSD_STATE], xc[:, gw + SSD_STATE:]

        dt = _softplus(dt_ref[0, 0, :, pl.ds(r0, L)] + dtb)
        rows8 = jnp.concatenate([dt * a_neg, dt], axis=0)
        cs8 = jnp.dot(rows8, upper_incl, preferred_element_type=F32, precision=lax.Precision.HIGHEST)
        acs_t = cs8[0:SSD_HPG]
        t8 = jnp.concatenate([dt, acs_t], axis=0)
        cols = jnp.concatenate([t8, jnp.zeros((L - 8, L), F32)], axis=0).T
        heads = range(SSD_HPG)
        dt_bc = _per_head([cols[:, j:j + 1] for j in heads], head_id)
        acs_bc = _per_head([cols[:, 4 + j:5 + j] for j in heads], head_id)
        atot_bc = _per_head([acs_t[j:j + 1, L - 1:L] for j in heads], head_id)

        xdt = xs * dt_bc
        bm_t = bm.T.astype(BF16)
        cm_b = cm.astype(BF16)
        cb = jnp.dot(cm_b, bm_t, preferred_element_type=F32)
        mixes = []
        for j in range(SSD_HPG):
            seg = cols[:, 4 + j:5 + j] - acs_t[j:j + 1, :]
            mixes.append((jnp.exp(jnp.where(causal, seg, -jnp.inf)) * cb).astype(BF16))
        ybig = jnp.dot(jnp.concatenate(mixes, axis=0), xdt.astype(BF16), preferred_element_type=F32)
        y = ybig[3 * L:4 * L]
        for j in (2, 1, 0):
            y = jnp.where(head_id == j, ybig[j * L:(j + 1) * L], y)

        st_new = jnp.dot(bm_t, (xdt * jnp.exp(atot_bc - acs_bc)).astype(BF16), preferred_element_type=F32)
        carry = carry_ref[...]
        y = y + jnp.dot(cm_b, carry.astype(BF16), preferred_element_type=F32) * jnp.exp(acs_bc)
        carry_ref[...] = carry * jnp.exp(atot_bc) + st_new
        y = y + xs * dsk
        zc = z_ref[0, pl.ds(r0, L), :].astype(F32)
        u_ref[0, pl.ds(r0, L), :] = (y * (zc * _sigmoid(zc))).astype(BF16)
        return 0

    lax.fori_loop(0, n_chunks, body, 0, unroll=2)


def _ssd(xbc, z, dt_t, conv_w, conv_b, dt_bias, a_log, d_skip, bsz, seq):
    gw = SSD_GW
    nxb = SSD_INNER // SSD_STATE
    dsk = jnp.repeat(d_skip.astype(F32), SSD_HEAD_DIM).reshape(SSD_GROUPS, 1, gw)
    dtb = dt_bias.astype(F32).reshape(SSD_GROUPS, SSD_HPG, 1)
    alog = a_log.astype(F32).reshape(SSD_GROUPS, SSD_HPG, 1)
    cb2 = conv_b.reshape(1, SSD_CONV_CH)
    x_map = lambda b, g: (b, 0, g)
    bm_map = lambda b, g: (b, 0, nxb + g)
    cm_map = lambda b, g: (b, 0, nxb + SSD_GROUPS + g)
    return pl.pallas_call(
        functools.partial(_ssd_kernel, n_chunks=seq // SSD_CHUNK),
        grid=(bsz, SSD_GROUPS),
        in_specs=[
            pl.BlockSpec((1, seq, gw), x_map),
            pl.BlockSpec((1, seq, SSD_STATE), bm_map),
            pl.BlockSpec((1, seq, SSD_STATE), cm_map),
            pl.BlockSpec((1, seq, gw), x_map),
            pl.BlockSpec((1, 1, SSD_HPG, seq), lambda b, g: (b, g, 0, 0)),
            pl.BlockSpec((SSD_CONV, gw), lambda b, g: (0, g)),
            pl.BlockSpec((SSD_CONV, SSD_STATE), lambda b, g: (0, nxb + g)),
            pl.BlockSpec((SSD_CONV, SSD_STATE), lambda b, g: (0, nxb + SSD_GROUPS + g)),
            pl.BlockSpec((1, gw), lambda b, g: (0, g)),
            pl.BlockSpec((1, SSD_STATE), lambda b, g: (0, nxb + g)),
            pl.BlockSpec((1, SSD_STATE), lambda b, g: (0, nxb + SSD_GROUPS + g)),
            pl.BlockSpec((1, SSD_HPG, 1), lambda b, g: (g, 0, 0)),
            pl.BlockSpec((1, SSD_HPG, 1), lambda b, g: (g, 0, 0)),
            pl.BlockSpec((1, 1, gw), lambda b, g: (g, 0, 0)),
        ],
        out_specs=pl.BlockSpec((1, seq, gw), x_map),
        out_shape=jax.ShapeDtypeStruct((bsz, seq, SSD_INNER), BF16),
        scratch_shapes=[pltpu.VMEM((seq + 8, gw + 2 * SSD_STATE), F32),
                        pltpu.VMEM((SSD_STATE, gw), F32)],
        compiler_params=_cparams("arbitrary", "arbitrary"),
    )(xbc, xbc, xbc, z, dt_t, conv_w, conv_w, conv_w, cb2, cb2, cb2, dtb, alog, dsk)


def _first_index_of_max(vals, lane_f):
    m = jnp.max(vals, axis=-1, keepdims=True)
    idx = jnp.min(jnp.where(vals == m, lane_f, float(LANES)), axis=-1, keepdims=True)
    return m, idx


def _post_kernel(att_ref, u_ref, gates_ref, x_ref, watt_ref, wssd_ref, wout_ref,
                 gssd_ref, gffn_ref, wr_ref, br_ref, x1_ref, meta_ref, cnt_ref, run_ref):
    i = pl.program_id(0)

    @pl.when(i == 0)
    def _():
        run_ref[...] = jnp.zeros(run_ref.shape, F32)

    y_att = jnp.dot(att_ref[...], watt_ref[...], preferred_element_type=F32)

    ssd = _rms(u_ref[...].astype(F32), gssd_ref[...])
    y_ssd = jnp.dot(ssd.astype(BF16), wssd_ref[...], preferred_element_type=F32)

    gates = gates_ref[...].astype(F32)
    merged = _sigmoid(gates[:, :D_MODEL]) * y_att + _sigmoid(gates[:, D_MODEL:]) * y_ssd
    x1 = x_ref[...] + jnp.dot(merged.astype(BF16), wout_ref[...], preferred_element_type=F32)
    x1_ref[...] = x1

    h2 = _rms(x1, gffn_ref[...])
    logits = jnp.dot(h2, wr_ref[...], preferred_element_type=F32, precision=lax.Precision.HIGHEST) + br_ref[...]
    tm = logits.shape[0]
    lane = lax.broadcasted_iota(jnp.int32, (tm, LANES), 1)
    lane_f = lane.astype(F32)
    ninf = -jnp.inf
    gl = jnp.where(lane < MOE_GROUPS, logits, ninf)
    gmax, gidx = _first_index_of_max(gl, lane_f)
    g_val = 1.0 / jnp.sum(jnp.exp(gl - gmax), axis=-1, keepdims=True)
    base = MOE_GROUPS + MOE_EPG * gidx
    el = jnp.where((lane_f >= base) & (lane_f < base + MOE_EPG), logits, ninf)
    e1, i1 = _first_index_of_max(el, lane_f)
    e2, i2 = _first_index_of_max(jnp.where(lane_f == i1, ninf, el), lane_f)
    t2 = jnp.exp(e2 - e1)
    w1 = g_val / (1.0 + t2)
    w2 = g_val * t2 / (1.0 + t2)
    a1, a2 = i1 - base, i2 - base
    lo, hi = jnp.minimum(a1, a2), jnp.maximum(a1, a2)
    c_lo = jnp.where(a1 < a2, w1, w2)
    c_hi = jnp.where(a1 < a2, w2, w1)
    pair = lo * (7.0 - lo) * 0.5 + (hi - lo - 1.0)
    cls = gidx * MOE_PAIRS + pair

    onehot = (lane_f == cls)
    oh_b = jnp.where(onehot, 1.0, 0.0).astype(BF16)
    rr = lax.broadcasted_iota(jnp.int32, (tm, tm), 0)
    cc = lax.broadcasted_iota(jnp.int32, (tm, tm), 1)
    strict = jnp.where(rr > cc, 1.0, 0.0).astype(BF16)
    prefix = jnp.dot(strict, oh_b, preferred_element_type=F32) + run_ref[...]
    rank = jnp.sum(jnp.where(onehot, prefix, 0.0), axis=-1, keepdims=True)
    run = run_ref[...] + jnp.sum(oh_b.astype(F32), axis=0, keepdims=True)
    run_ref[...] = run
    cnt_ref[...] = jnp.broadcast_to(run, cnt_ref.shape)

    meta = jnp.where(lane == 0, cls, jnp.where(lane == 1, rank, jnp.where(lane == 2, c_lo,
                     jnp.where(lane == 3, c_hi, 0.0))))
    meta_ref[...] = meta


def _post(att, u, gates, x2d, w_att, w_ssd, w_out, g_ssd, g_ffn, w_r, b_r):
    t = x2d.shape[0]
    tm = TM_POST
    row = lambda w: pl.BlockSpec((tm, w), lambda i: (i, 0))
    const = lambda a: pl.BlockSpec(a.shape, lambda i: (0,) * a.ndim)
    return pl.pallas_call(
        _post_kernel,
        grid=(t // tm,),
        in_specs=[row(ATT_OUT), row(SSD_INNER), row(2 * D_MODEL), row(D_MODEL),
                  const(w_att), const(w_ssd), const(w_out), const(g_ssd), const(g_ffn), const(w_r), const(b_r)],
        out_specs=(row(D_MODEL), row(LANES), pl.BlockSpec((8, LANES), lambda i: (0, 0))),
        out_shape=(jax.ShapeDtypeStruct((t, D_MODEL), F32),
                   jax.ShapeDtypeStruct((t, LANES), F32),
                   jax.ShapeDtypeStruct((8, LANES), F32)),
        scratch_shapes=[pltpu.VMEM((1, LANES), F32)],
        compiler_params=_cparams("arbitrary"),
    )(att, u, gates, x2d, w_att, w_ssd, w_out, g_ssd, g_ffn, w_r, b_r)


def _scatter_kernel(pos_ref, x1_ref, g_ref, xs_in_ref, xs_ref, h_ref, sem):
    del xs_in_ref
    i = pl.program_id(0)
    tm = h_ref.shape[0]
    h_ref[...] = _rms(x1_ref[...], g_ref[...])

    def row_copy(r):
        return pltpu.make_async_copy(h_ref.at[pl.ds(r, 1)], xs_ref.at[pl.ds(pos_ref[i * tm + r], 1)], sem)

    def issue(r, _):
        row_copy(r).start()
        return 0

    lax.fori_loop(0, tm, issue, 0)
    pltpu.make_async_copy(h_ref, xs_ref.at[pl.ds(0, tm)], sem).wait()


def _scatter_rows(pos, x1, g_ffn, xs_init):
    t = x1.shape[0]
    tm = TM_SCAT
    grid_spec = pltpu.PrefetchScalarGridSpec(
        num_scalar_prefetch=1,
        grid=(t // tm,),
        in_specs=[pl.BlockSpec((tm, D_MODEL), lambda i, pos: (i, 0)),
                  pl.BlockSpec((1, D_MODEL), lambda i, pos: (0, 0)),
                  pl.BlockSpec(memory_space=pl.ANY)],
        out_specs=pl.BlockSpec(memory_space=pl.ANY),
        scratch_shapes=[pltpu.VMEM((tm, D_MODEL), F32), pltpu.SemaphoreType.DMA(())],
    )
    return pl.pallas_call(
        _scatter_kernel,
        grid_spec=grid_spec,
        out_shape=jax.ShapeDtypeStruct(xs_init.shape, F32),
        input_output_aliases={3: 0},
        compiler_params=_cparams("arbitrary"),
    )(pos, x1, g_ffn, xs_init)


def _pack_pair(a, b):
    ua = lax.bitcast_convert_type(a.astype(BF16).astype(F32), jnp.uint32)
    ub = lax.bitcast_convert_type(b.astype(BF16).astype(F32), jnp.uint32)
    return (ua & jnp.uint32(0xFFFF0000)) | (ub >> 16)


def _unpack_pair(p):
    a = lax.bitcast_convert_type(p & jnp.uint32(0xFFFF0000), F32)
    b = lax.bitcast_convert_type(p << 16, F32)
    return a, b


def _expert_kernel(nused_ref, elo_ref, ehi_ref, xs_ref, wg_lo, wu_lo, wd_lo, wg_hi, wu_hi, wd_hi, ys_ref):
    del elo_ref, ehi_ref
    i = pl.program_id(0)

    @pl.when(i < nused_ref[0])
    def _():
        xb = xs_ref[...].astype(BF16)

        def mlp(wg, wu, wd):
            gt = jnp.dot(xb, wg[0], preferred_element_type=F32)
            up = jnp.dot(xb, wu[0], preferred_element_type=F32)
            hid = (gt * _sigmoid(gt)) * up
            return jnp.dot(hid.astype(BF16), wd[0], preferred_element_type=F32)

        ys_ref[...] = _pack_pair(mlp(wg_lo, wu_lo, wd_lo), mlp(wg_hi, wu_hi, wd_hi))

    @pl.when(i >= nused_ref[0])
    def _():
        ys_ref[...] = jnp.zeros(ys_ref.shape, jnp.uint32)


def _experts(n_used, tile_elo, tile_ehi, xs, w_gate, w_up, w_down):
    p_rows = xs.shape[0]
    tm = TM_EXP
    n_tiles = p_rows // tm

    def row_map(i, nu, elo, ehi):
        return (jnp.minimum(i, nu[0] - 1), 0)

    def wmap(which):
        def f(i, nu, elo, ehi):
            e = (elo, ehi)[which]
            return (e[jnp.minimum(i, nu[0] - 1)], 0, 0)
        return f

    wspec_in = lambda which: pl.BlockSpec((1, D_MODEL, MOE_HIDDEN), wmap(which))
    wspec_out = lambda which: pl.BlockSpec((1, MOE_HIDDEN, D_MODEL), wmap(which))
    grid_spec = pltpu.PrefetchScalarGridSpec(
        num_scalar_prefetch=3,
        grid=(n_tiles,),
        in_specs=[pl.BlockSpec((tm, D_MODEL), row_map),
                  wspec_in(0), wspec_in(0), wspec_out(0), wspec_in(1), wspec_in(1), wspec_out(1)],
        out_specs=pl.BlockSpec((tm, D_MODEL), lambda i, nu, elo, ehi: (i, 0)),
    )
    return pl.pallas_call(
        _expert_kernel,
        grid_spec=grid_spec,
        out_shape=jax.ShapeDtypeStruct((p_rows, D_MODEL), jnp.uint32),
        compiler_params=_cparams("arbitrary"),
    )(n_used, tile_elo, tile_ehi, xs, w_gate, w_up, w_down, w_gate, w_up, w_down)


def _final_kernel(pos_ref, x1_ref, meta_ref, p_ref, ys_ref, gple_ref, wpg_ref, wpp_ref, gfin_ref,
                  out_ref, buf_ref, sem, *, n_steps):
    i = pl.program_id(0)
    tm = x1_ref.shape[0]

    def issue_tile(step, slot):
        def issue(r, _):
            pltpu.make_async_copy(ys_ref.at[pl.ds(pos_ref[step * tm + r], 1)],
                                  buf_ref.at[slot, pl.ds(r, 1)], sem.at[slot]).start()
            return 0
        lax.fori_loop(0, tm, issue, 0)

    @pl.when(i == 0)
    def _():
        issue_tile(0, 0)

    slot = i % 2

    @pl.when(i + 1 < n_steps)
    def _():
        issue_tile(i + 1, 1 - slot)

    pltpu.make_async_copy(ys_ref.at[pl.ds(0, tm)], buf_ref.at[slot], sem.at[slot]).wait()

    y_lo, y_hi = _unpack_pair(buf_ref[slot])
    meta = meta_ref[...]
    x2 = x1_ref[...] + meta[:, 2:3] * y_lo + meta[:, 3:4] * y_hi
    hn = _rms(x2, gple_ref[...])
    gate = _sigmoid(jnp.dot(hn.astype(BF16), wpg_ref[...], preferred_element_type=F32))
    pp = jnp.dot(p_ref[...].astype(BF16), wpp_ref[...], preferred_element_type=F32)
    out_ref[...] = _rms(x2 + gate * pp, gfin_ref[...])


def _final(pos, x1, meta, p2d, ys, g_ple, w_pg, w_pp, g_fin):
    t = x1.shape[0]
    tm = TM_FIN
    n_steps = t // tm
    row = lambda w: pl.BlockSpec((tm, w), lambda i, pos: (i, 0))
    const = lambda a: pl.BlockSpec(a.shape, lambda i, pos: (0,) * a.ndim)
    grid_spec = pltpu.PrefetchScalarGridSpec(
        num_scalar_prefetch=1,
        grid=(n_steps,),
        in_specs=[row(D_MODEL), row(LANES), row(PLE_DIM), pl.BlockSpec(memory_space=pl.ANY),
                  const(g_ple), const(w_pg), const(w_pp), const(g_fin)],
        out_specs=row(D_MODEL),
        scratch_shapes=[pltpu.VMEM((2, tm, D_MODEL), jnp.uint32), pltpu.SemaphoreType.DMA((2,))],
    )
    return pl.pallas_call(
        functools.partial(_final_kernel, n_steps=n_steps),
        grid_spec=grid_spec,
        out_shape=jax.ShapeDtypeStruct((t, D_MODEL), F32),
        compiler_params=_cparams("arbitrary"),
    )(pos, x1, meta, p2d, ys, g_ple, w_pg, w_pp, g_fin)


_PAIR_LO = np.array([0, 0, 0, 1, 1, 2], np.int32)
_PAIR_HI = np.array([1, 2, 3, 2, 3, 3], np.int32)


def _routing_tables(meta, counts_f, n_tiles):
    cls = meta[:, 0].astype(jnp.int32)
    rank = meta[:, 1].astype(jnp.int32)
    counts = counts_f[0, :MOE_CLASSES].astype(jnp.int32)
    tiles_per = (counts + TM_EXP - 1) // TM_EXP
    tile_end = jnp.cumsum(tiles_per)
    tile_start = tile_end - tiles_per
    class_ids = jnp.arange(MOE_CLASSES, dtype=jnp.int32)
    pos = jnp.sum(jnp.where(cls[:, None] == class_ids[None, :], (tile_start * TM_EXP)[None, :], 0), axis=1) + rank
    n_used = tile_end[-1:]
    tile_ids = jnp.arange(n_tiles, dtype=jnp.int32)
    tile_cls = jnp.minimum(jnp.sum((tile_end[None, :] <= tile_ids[:, None]).astype(jnp.int32), axis=1),
                           MOE_CLASSES - 1)
    grp = tile_cls // MOE_PAIRS
    pair = tile_cls % MOE_PAIRS
    tile_elo = grp * MOE_EPG + jnp.asarray(_PAIR_LO)[pair]
    tile_ehi = grp * MOE_EPG + jnp.asarray(_PAIR_HI)[pair]
    return pos.astype(jnp.int32), n_used.astype(jnp.int32), tile_elo.astype(jnp.int32), tile_ehi.astype(jnp.int32)


def kernel(x, p, norm_mix_g, w_in, conv_w, conv_b, dt_bias, a_log, d_skip, ssd_norm_g, w_att_branch,
           w_ssd_branch, w_out, norm_ffn_g, w_router_group, b_router_group, w_router_expert,
           b_router_expert, w_exp_gate, w_exp_up, w_exp_down, norm_ple_g, w_ple_gate, w_ple_proj,
           final_norm_g):
    bsz, seq, _ = x.shape
    t = bsz * seq
    assert w_in.shape[0] == 1, "single-layer block"
    assert seq // ATT_PATTERNS[-1][1] == ATT_BLOCK and seq % TM_IN == 0
    x2d = x.reshape(t, D_MODEL)

    wi = w_in[0]
    c_dt = QKV_W + SSD_INNER + SSD_CONV_CH
    qkv_cols = [wi[:, sel * ATT_WIDTH + g * ATT_OUT: sel * ATT_WIDTH + (g + 1) * ATT_OUT]
                for g in range(ATT_GROUPS) for sel in range(3)]
    w_main = jnp.concatenate(qkv_cols + [wi[:, QKV_W:c_dt], wi[:, c_dt + SSD_HEADS:]], axis=1).astype(BF16)
    w_dt = jnp.pad(wi[:, c_dt:c_dt + SSD_HEADS], ((0, 0), (0, LANES - SSD_HEADS))).astype(BF16)
    row = lambda v: v.reshape(1, -1).astype(F32)

    qkv1, qkv2, qkv3, z, xbc, gates, dt_raw = _in_proj(x2d, row(norm_mix_g[0]), w_main, w_dt, bsz, seq)

    att = _attention(qkv1.reshape(bsz, seq, ATT_WIDTH), qkv2, qkv3, bsz, seq).reshape(t, ATT_OUT)

    dt_t = dt_raw[:, :SSD_HEADS].reshape(bsz, seq, SSD_GROUPS, SSD_HPG).transpose(0, 2, 3, 1)
    u = _ssd(xbc.reshape(bsz, seq, SSD_CONV_CH), z.reshape(bsz, seq, SSD_INNER), dt_t,
             conv_w[0], conv_b[0], dt_bias[0], a_log[0], d_skip[0], bsz, seq).reshape(t, SSD_INNER)

    w_r = jnp.pad(jnp.concatenate([w_router_group[0], w_router_expert[0]], axis=1),
                  ((0, 0), (0, LANES - MOE_GROUPS - MOE_EXPERTS))).astype(F32)
    b_r = jnp.pad(jnp.concatenate([b_router_group[0], b_router_expert[0]]),
                  (0, LANES - MOE_GROUPS - MOE_EXPERTS)).reshape(1, LANES).astype(F32)
    x1, meta, counts = _post(att, u, gates, x2d,
                             w_att_branch[0].astype(BF16), w_ssd_branch[0].astype(BF16), w_out[0].astype(BF16),
                             row(ssd_norm_g[0]), row(norm_ffn_g[0]), w_r, b_r)

    n_tiles = t // TM_EXP + MOE_CLASSES
    pos, n_used, tile_elo, tile_ehi = _routing_tables(meta, counts, n_tiles)
    xs = _scatter_rows(pos, x1, row(norm_ffn_g[0]), jnp.zeros((n_tiles * TM_EXP, D_MODEL), F32))
    ys = _experts(n_used, tile_elo, tile_ehi, xs,
                  w_exp_gate[0].astype(BF16), w_exp_up[0].astype(BF16), w_exp_down[0].astype(BF16))
    out = _final(pos, x1, meta, p[0].reshape(t, PLE_DIM), ys, row(norm_ple_g[0]),
                 w_ple_gate[0].astype(BF16), w_ple_proj[0].astype(BF16), row(final_norm_g))
    return out.reshape(bsz, seq, D_MODEL)
```

```python
import functools

import numpy as np
import jax
import jax.numpy as jnp
from jax import lax
from jax.experimental import pallas as pl
from jax.experimental.pallas import tpu as pltpu

F32 = jnp.float32
BF16 = jnp.bfloat16

D_MODEL = 1024
PLE_DIM = 256
RMS_EPS = 1e-6

ATT_PATTERNS = ((128, 1), (512, 4), (2048, 16))
ATT_GROUPS = 3
ATT_HPG = 8
ATT_HEAD_DIM = 64
ATT_WIDTH = ATT_GROUPS * ATT_HPG * ATT_HEAD_DIM
ATT_OUT = ATT_HPG * ATT_HEAD_DIM
ATT_BLOCK = 128
ALIBI_MAX_BIAS = 8.0
QKV_W = 3 * ATT_WIDTH

SSD_INNER = 2048
SSD_HEADS = 32
SSD_GROUPS = 8
SSD_HPG = 4
SSD_HEAD_DIM = 64
SSD_STATE = 128
SSD_CONV = 4
SSD_CHUNK = 128
SSD_CONV_CH = SSD_INNER + 2 * SSD_GROUPS * SSD_STATE
SSD_GW = SSD_HPG * SSD_HEAD_DIM
SSD_LOCKSTEP = 4

MOE_GROUPS = 4
MOE_EPG = 4
MOE_EXPERTS = 16
MOE_HIDDEN = 512
MOE_PAIRS = 6
MOE_CLASSES = MOE_GROUPS * MOE_PAIRS

LANES = 128
NEG_BIG = -1e30
VMEM_LIMIT = 56 * 1024 * 1024

TM_IN = 1024
TN_IN = 512
TM_POST = 512
TM_SCAT = 512
TM_EXP = 256
TM_FIN = 512


def _cparams(*sem):
    return pltpu.CompilerParams(dimension_semantics=sem, vmem_limit_bytes=VMEM_LIMIT)


def _sigmoid(x):
    return 0.5 * jnp.tanh(0.5 * x) + 0.5


def _silu(x):
    h = 0.5 * x
    return h + h * jnp.tanh(h)


def _rms(x, g):
    ms = jnp.mean(x * x, axis=-1, keepdims=True)
    return x * lax.rsqrt(ms + RMS_EPS) * g


_IN_SEG = (3, 3, 3, SSD_INNER // TN_IN, SSD_CONV_CH // TN_IN, 2 * D_MODEL // TN_IN)
_IN_START = tuple(int(v) for v in np.cumsum((0,) + _IN_SEG))


def _inproj_kernel(x_ref, g_ref, w_ref, wdt_ref, qkv1_ref, qkv2_ref, qkv3_ref, z_ref, xbc_ref, gates_ref,
                   dt_ref, h_ref, hcol_ref):
    j = pl.program_id(1)
    tm = x_ref.shape[0]

    @pl.when(j == 0)
    def _():
        h = _rms(x_ref[...], g_ref[...])
        hb = h.astype(BF16)
        h_ref[0] = hb
        dt_ref[...] = jnp.dot(hb, wdt_ref[...], preferred_element_type=F32)
        ncb = hcol_ref.shape[0]
        for c in range(ncb):
            hcol_ref[c] = h[:, c * LANES:(c + 1) * LANES]
        for slot, (_, dil) in enumerate(ATT_PATTERNS[1:], start=1):
            rows = tm // dil
            for r in range(dil):
                for c in range(ncb):
                    h_ref[slot, r * rows:(r + 1) * rows, c * LANES:(c + 1) * LANES] = (
                        hcol_ref[c, pl.ds(r, rows, stride=dil), :].astype(BF16))

    slot = jnp.where(j < _IN_START[1], 0, jnp.where(j < _IN_START[2], 1, jnp.where(j < _IN_START[3], 2, 0)))
    res = jnp.dot(h_ref[slot], w_ref[...], preferred_element_type=F32).astype(BF16)

    def store(k, fn):
        @pl.when((j >= _IN_START[k]) & (j < _IN_START[k + 1]))
        def _():
            fn()

    def store_grouped(ref, dil):
        def fn():
            ref[0] = res.reshape(dil, tm // dil, res.shape[1])
        return fn

    def store_plain(ref):
        def fn():
            ref[...] = res
        return fn

    store(0, store_plain(qkv1_ref))
    store(1, store_grouped(qkv2_ref, ATT_PATTERNS[1][1]))
    store(2, store_grouped(qkv3_ref, ATT_PATTERNS[2][1]))
    store(3, store_plain(z_ref))
    store(4, store_plain(xbc_ref))
    store(5, store_plain(gates_ref))


def _in_proj(x2d, g, w_main, w_dt, bsz, seq):
    t = x2d.shape[0]
    nj = _IN_START[-1]
    per_seq = seq // TM_IN
    d2, d3 = ATT_PATTERNS[1][1], ATT_PATTERNS[2][1]

    def seg_map(k):
        return lambda i, j: (i, jnp.clip(j - _IN_START[k], 0, _IN_SEG[k] - 1))

    def grp_map(k):
        return lambda i, j: (i // per_seq, 0, i % per_seq, jnp.clip(j - _IN_START[k], 0, _IN_SEG[k] - 1))

    out_shapes = (
        jax.ShapeDtypeStruct((t, ATT_WIDTH), BF16),
        jax.ShapeDtypeStruct((bsz, d2, seq // d2, ATT_WIDTH), BF16),
        jax.ShapeDtypeStruct((bsz, d3, seq // d3, ATT_WIDTH), BF16),
        jax.ShapeDtypeStruct((t, SSD_INNER), BF16),
        jax.ShapeDtypeStruct((t, SSD_CONV_CH), BF16),
        jax.ShapeDtypeStruct((t, 2 * D_MODEL), BF16),
        jax.ShapeDtypeStruct((t, LANES), F32),
    )
    out_specs = (
        pl.BlockSpec((TM_IN, TN_IN), seg_map(0)),
        pl.BlockSpec((1, d2, TM_IN // d2, TN_IN), grp_map(1)),
        pl.BlockSpec((1, d3, TM_IN // d3, TN_IN), grp_map(2)),
        pl.BlockSpec((TM_IN, TN_IN), seg_map(3)),
        pl.BlockSpec((TM_IN, TN_IN), seg_map(4)),
        pl.BlockSpec((TM_IN, TN_IN), seg_map(5)),
        pl.BlockSpec((TM_IN, LANES), lambda i, j: (i, 0)),
    )
    return pl.pallas_call(
        _inproj_kernel,
        grid=(t // TM_IN, nj),
        in_specs=[
            pl.BlockSpec((TM_IN, D_MODEL), lambda i, j: (i, 0)),
            pl.BlockSpec((1, D_MODEL), lambda i, j: (0, 0)),
            pl.BlockSpec((D_MODEL, TN_IN), lambda i, j: (0, j)),
            pl.BlockSpec((D_MODEL, LANES), lambda i, j: (0, 0)),
        ],
        out_specs=out_specs,
        out_shape=out_shapes,
        scratch_shapes=[pltpu.VMEM((3, TM_IN, D_MODEL), BF16),
                        pltpu.VMEM((D_MODEL // LANES, TM_IN, LANES), F32)],
        compiler_params=_cparams("arbitrary", "arbitrary"),
    )(x2d, g, w_main, w_dt)


def _att_bias_tables():
    h = np.arange(1, ATT_GROUPS * ATT_HPG + 1, dtype=np.float32)
    slopes = np.exp2(-ALIBI_MAX_BIAS * h / (ATT_GROUPS * ATT_HPG)).astype(np.float32).reshape(ATT_GROUPS, ATT_HPG)
    qi = np.arange(ATT_BLOCK)[:, None] + ATT_BLOCK
    kj = np.arange(2 * ATT_BLOCK)[None, :]
    delta = qi - kj
    tabs = []
    for g, (window, dil) in enumerate(ATT_PATTERNS):
        span = window // dil
        band = (delta >= 0) & (delta <= span)
        bias = (-slopes[g][:, None, None] * (delta * dil).astype(np.float32)[None]).astype(np.float32)
        with_prev = np.where(band[None], bias, np.float32(NEG_BIG))
        first = np.where((band & (kj >= ATT_BLOCK))[None], bias, np.float32(NEG_BIG))
        tabs.append(np.stack([first, with_prev]).astype(np.float32))
    return tabs[0], tabs[1], tabs[2][1][:, :, ATT_BLOCK:]


def _att_unit(q, k2, v2, tab_fn):
    lane = lax.broadcasted_iota(jnp.int32, (1, LANES), 1)
    low = lane < ATT_HEAD_DIM
    scale = ATT_HEAD_DIM ** -0.5
    qmask = (jnp.where(low, scale, 0.0).astype(BF16), jnp.where(low, 0.0, scale).astype(BF16))
    outs, lses = [], []
    for hh in range(2):
        s = lax.dot_general(q * qmask[hh], k2, (((1,), (1,)), ((), ())), preferred_element_type=F32)
        s = s + tab_fn(hh)
        m = jnp.max(s, axis=-1, keepdims=True)
        e = jnp.exp(s - m)
        den = jnp.sum(e, axis=-1, keepdims=True)
        pv = jnp.dot(e.astype(BF16), v2, preferred_element_type=F32)
        outs.append(pv / den)
        lses.append(m + jnp.log(den))
    return jnp.where(low, outs[0], outs[1]), jnp.where(low, lses[0], lses[1])


def _att_kernel(q1, k1, v1, q2, k2, v2, q3, k3, v3, t1, t2, t3, out_ref,
                o1, l1, o2p, l2p, o2n, l2n, o3p, l3p, o3n, l3n, *, seq):
    blk = ATT_BLOCK
    d2, d3 = ATT_PATTERNS[1][1], ATT_PATTERNS[2][1]
    nb1, nb2 = seq // blk, seq // d2 // blk

    def rows(n):
        return pl.ds(pl.multiple_of(n * blk, blk), blk)

    assert nb1 == d2 * nb2 == d3

    def unit_body(u, _):
        cur, prv = rows(u), rows(jnp.maximum(u - 1, 0))
        sel = jnp.minimum(u, 1)
        kk = jnp.concatenate([k1[0, prv, :], k1[0, cur, :]], axis=0)
        vv = jnp.concatenate([v1[0, prv, :], v1[0, cur, :]], axis=0)
        o, l = _att_unit(q1[0, cur, :], kk, vv, lambda hh: t1[sel, hh])
        o1[cur, :] = o
        l1[cur, :] = l

        r, n = u // nb2, u % nb2
        cur2, prv2 = rows(n), rows(jnp.maximum(n - 1, 0))
        sel2 = jnp.minimum(n, 1)
        kk = jnp.concatenate([k2[0, r, prv2, :], k2[0, r, cur2, :]], axis=0)
        vv = jnp.concatenate([v2[0, r, prv2, :], v2[0, r, cur2, :]], axis=0)
        o, l = _att_unit(q2[0, r, cur2, :], kk, vv, lambda hh: t2[sel2, hh])
        o2p[cur, :] = o
        l2p[cur, :] = l

        o, l = _att_unit(q3[0, u], k3[0, u], v3[0, u], lambda hh: t3[hh])
        o3p[cur, :] = o
        l3p[cur, :] = l
        return 0

    lax.fori_loop(0, nb1, unit_body, 0, unroll=2)

    for dil, pairs in ((d2, ((o2p, o2n), (l2p, l2n))), (d3, ((o3p, o3n), (l3p, l3n)))):
        n_sub = seq // dil
        for r in range(dil):
            for src, dst in pairs:
                dst[pl.ds(r, n_sub, stride=dil), :] = src[r * n_sub:(r + 1) * n_sub, :]

    mrows = 2 * blk

    def merge(c, _):
        rr = pl.ds(pl.multiple_of(c * mrows, mrows), mrows)
        la, lb, lc = l1[rr, :], l2n[rr, :], l3n[rr, :]
        lm = jnp.maximum(jnp.maximum(la, lb), lc)
        ea, eb, ec = jnp.exp(la - lm), jnp.exp(lb - lm), jnp.exp(lc - lm)
        att = (ea * o1[rr, :] + eb * o2n[rr, :] + ec * o3n[rr, :]) / (ea + eb + ec)
        out_ref[0, rr, :] = att.astype(BF16)
        return 0

    lax.fori_loop(0, seq // mrows, merge, 0)


def _attention(qkv1, qkv2, qkv3, bsz, seq):
    t1, t2, t3 = (jnp.asarray(t) for t in _att_bias_tables())
    d2, d3 = ATT_PATTERNS[1][1], ATT_PATTERNS[2][1]
    npair = ATT_HPG // 2
    in_arrays, in_specs = [], []
    for arr, lead in ((qkv1, ()), (qkv2, (d2,)), (qkv3, (d3,))):
        n_rows = arr.shape[-2]
        for sel in range(3):
            zeros = (0,) * len(lead)
            in_arrays.append(arr)
            in_specs.append(pl.BlockSpec((1,) + lead + (n_rows, LANES),
                                         lambda hp, b, sel=sel, zeros=zeros: (b,) + zeros + (0, sel * npair + hp)))
    in_arrays += [t1, t2, t3]
    in_specs += [pl.BlockSpec((2, 2, ATT_BLOCK, 2 * ATT_BLOCK), lambda hp, b: (0, hp, 0, 0)),
                 pl.BlockSpec((2, 2, ATT_BLOCK, 2 * ATT_BLOCK), lambda hp, b: (0, hp, 0, 0)),
                 pl.BlockSpec((2, ATT_BLOCK, ATT_BLOCK), lambda hp, b: (hp, 0, 0))]
    return pl.pallas_call(
        functools.partial(_att_kernel, seq=seq),
        grid=(npair, bsz),
        in_specs=in_specs,
        out_specs=pl.BlockSpec((1, seq, LANES), lambda hp, b: (b, 0, hp)),
        out_shape=jax.ShapeDtypeStruct((bsz, seq, ATT_OUT), BF16),
        scratch_shapes=[pltpu.VMEM((seq, LANES), F32)] * 10,
        compiler_params=_cparams("arbitrary", "arbitrary"),
    )(*in_arrays)


def _softplus(x):
    return jnp.maximum(x, 0.0) + jnp.log1p(jnp.exp(-jnp.abs(x)))


def _conv_shift_table():
    L = SSD_CHUNK
    l = np.arange(L)[:, None]
    j = np.arange(2 * L)[None, :]
    tab = np.zeros((2, SSD_CONV, L, 2 * L), np.float32)
    for kk in range(SSD_CONV):
        sh = SSD_CONV - 1 - kk
        tab[0, kk] = (j == l - sh)
        tab[1, kk] = (j == L + l - sh)
    return tab.reshape(2, SSD_CONV * L, 2 * L)


def _head_spread_table():
    gw, n_src = SSD_GW, LANES
    tab = np.zeros((2 * n_src, 2 * gw), np.float32)
    for half in range(2):
        for blk, src0 in enumerate((4, 8)):
            for col in range(gw):
                tab[half * n_src + src0 + col // SSD_HEAD_DIM, blk * gw + col] = 1.0
    return tab


def _ssd_kernel(x_ref, b_ref, c_ref, z_ref, dt_ref, shift_ref, spread_ref, wx_ref, wb_ref, wc_ref,
                bx_ref, bb_ref, bc_ref, dtb_ref, alog_ref, dsk_ref, u_ref, carry_ref, *, n_chunks):
    L = SSD_CHUNK
    gw = SSD_GW
    carry_ref[...] = jnp.zeros(carry_ref.shape, F32)

    w = jnp.concatenate([wx_ref[...], wb_ref[...], wc_ref[...]], axis=1)
    bias = jnp.concatenate([bx_ref[...], bb_ref[...], bc_ref[...]], axis=1)
    a_neg = -jnp.exp(alog_ref[0])
    dtb = dtb_ref[0]
    dsk = dsk_ref[0]
    ri = lax.broadcasted_iota(jnp.int32, (L, L), 0)
    ci = lax.broadcasted_iota(jnp.int32, (L, L), 1)
    upper_incl = (ri <= ci).astype(F32)
    causal = ri >= ci
    lane = lax.broadcasted_iota(jnp.int32, (1, LANES), 1)
    low = lane < SSD_HEAD_DIM
    heads = range(SSD_HPG)

    def conv_taps(s):
        c = s["c"]
        rs = pl.multiple_of(jnp.maximum(c - 1, 0) * L, L)
        win = jnp.concatenate([x_ref[0, pl.ds(rs, 2 * L), :], b_ref[0, pl.ds(rs, 2 * L), :],
                               c_ref[0, pl.ds(rs, 2 * L), :]], axis=1)
        s["taps"] = jnp.dot(shift_ref[jnp.minimum(c, 1)], win, preferred_element_type=F32)

    def conv_act(s):
        taps = s.pop("taps")
        acc = bias
        for kk in range(SSD_CONV):
            acc = acc + w[kk:kk + 1, :] * taps[kk * L:(kk + 1) * L]
        xc = _silu(acc)
        s["xs"], s["bm"], s["cm_b"] = xc[:, :gw], xc[:, gw:gw + SSD_STATE], xc[:, gw + SSD_STATE:].astype(BF16)

    def decay_cumsum(s):
        dt = _softplus(dt_ref[0, 0, :, pl.ds(s["r0"], L)] + dtb)
        rows8 = jnp.concatenate([dt * a_neg, dt], axis=0)
        cs8 = jnp.dot(rows8, upper_incl, preferred_element_type=F32, precision=lax.Precision.HIGHEST)
        s["dt"], s["acs_t"] = dt, cs8[0:SSD_HPG]

    def decay_spread(s):
        dt, acs_t = s["dt"], s["acs_t"]
        t16 = jnp.concatenate([acs_t, acs_t, acs_t, dt], axis=0)
        cols = jnp.concatenate([t16, jnp.zeros((L - 16, L), F32)], axis=0).T
        dt_at8 = pltpu.roll(cols, LANES - 4, 1)
        comb = jnp.where(lane < 8, jnp.exp(cols), jnp.exp(cols[L - 1:L, :] - cols) * dt_at8)
        hi = comb.astype(BF16)
        lo = (comb - hi.astype(F32)).astype(BF16)
        s["spread"] = jnp.dot(jnp.concatenate([hi, lo], axis=1), spread_ref[...], preferred_element_type=F32)
        s["acs_cols"] = cols

    def scores(s):
        s["bm_t"] = s.pop("bm").T.astype(BF16)
        s["cb"] = jnp.dot(s["cm_b"], s["bm_t"], preferred_element_type=F32)

    def intra(s):
        spread, acs_t, dt, cb, cols = s["spread"], s.pop("acs_t"), s.pop("dt"), s.pop("cb"), s.pop("acs_cols")
        xs = s["xs"]
        xs_b = xs.astype(BF16)
        mixes = []
        for j in heads:
            seg = cols[:, j:j + 1] - acs_t[j:j + 1, :]
            mixes.append((jnp.exp(jnp.where(causal, seg, -jnp.inf)) * (cb * dt[j:j + 1, :])).astype(BF16))
        halves = []
        for hp in range(SSD_HPG // 2):
            yy = jnp.dot(jnp.concatenate(mixes[2 * hp:2 * hp + 2], axis=0), xs_b[:, hp * LANES:(hp + 1) * LANES],
                         preferred_element_type=F32)
            halves.append(jnp.where(low, yy[:L], yy[L:]))
        s["y"] = jnp.concatenate(halves, axis=1)
        s["st_new"] = jnp.dot(s.pop("bm_t"), (xs * spread[:, gw:]).astype(BF16),
                              preferred_element_type=F32)

    def inter(s):
        eacs_bc = s.pop("spread")[:, :gw]
        carry = carry_ref[...]
        y = s.pop("y") + jnp.dot(s.pop("cm_b"), carry.astype(BF16), preferred_element_type=F32) * eacs_bc
        carry_ref[...] = carry * eacs_bc[L - 1:L, :] + s.pop("st_new")
        y = y + s.pop("xs") * dsk
        rows = pl.ds(s["r0"], L)
        u_ref[0, rows, :] = (y * _silu(z_ref[0, rows, :].astype(F32))).astype(BF16)

    def body(i, _):
        states = [{"c": i * SSD_LOCKSTEP + k, "r0": pl.multiple_of((i * SSD_LOCKSTEP + k) * L, L)}
                  for k in range(SSD_LOCKSTEP)]
        for stage in (conv_taps, decay_cumsum, conv_act, decay_spread, scores, intra, inter):
            for s in states:
                stage(s)
        return 0

    lax.fori_loop(0, n_chunks // SSD_LOCKSTEP, body, 0)


def _ssd(xbc, z, dt_t, conv_w, conv_b, dt_bias, a_log, d_skip, bsz, seq):
    gw = SSD_GW
    nxb = SSD_INNER // SSD_STATE
    dsk = jnp.repeat(d_skip.astype(F32), SSD_HEAD_DIM).reshape(SSD_GROUPS, 1, gw)
    dtb = dt_bias.astype(F32).reshape(SSD_GROUPS, SSD_HPG, 1)
    alog = a_log.astype(F32).reshape(SSD_GROUPS, SSD_HPG, 1)
    cb2 = conv_b.reshape(1, SSD_CONV_CH)
    shift = jnp.asarray(_conv_shift_table(), BF16)
    spread = jnp.asarray(_head_spread_table(), BF16)
    x_map = lambda b, g: (b, 0, g)
    bm_map = lambda b, g: (b, 0, nxb + g)
    cm_map = lambda b, g: (b, 0, nxb + SSD_GROUPS + g)
    return pl.pallas_call(
        functools.partial(_ssd_kernel, n_chunks=seq // SSD_CHUNK),
        grid=(bsz, SSD_GROUPS),
        in_specs=[
            pl.BlockSpec((1, seq, gw), x_map),
            pl.BlockSpec((1, seq, SSD_STATE), bm_map),
            pl.BlockSpec((1, seq, SSD_STATE), cm_map),
            pl.BlockSpec((1, seq, gw), x_map),
            pl.BlockSpec((1, 1, SSD_HPG, seq), lambda b, g: (b, g, 0, 0)),
            pl.BlockSpec(shift.shape, lambda b, g: (0, 0, 0)),
            pl.BlockSpec(spread.shape, lambda b, g: (0, 0)),
            pl.BlockSpec((SSD_CONV, gw), lambda b, g: (0, g)),
            pl.BlockSpec((SSD_CONV, SSD_STATE), lambda b, g: (0, nxb + g)),
            pl.BlockSpec((SSD_CONV, SSD_STATE), lambda b, g: (0, nxb + SSD_GROUPS + g)),
            pl.BlockSpec((1, gw), lambda b, g: (0, g)),
            pl.BlockSpec((1, SSD_STATE), lambda b, g: (0, nxb + g)),
            pl.BlockSpec((1, SSD_STATE), lambda b, g: (0, nxb + SSD_GROUPS + g)),
            pl.BlockSpec((1, SSD_HPG, 1), lambda b, g: (g, 0, 0)),
            pl.BlockSpec((1, SSD_HPG, 1), lambda b, g: (g, 0, 0)),
            pl.BlockSpec((1, 1, gw), lambda b, g: (g, 0, 0)),
        ],
        out_specs=pl.BlockSpec((1, seq, gw), x_map),
        out_shape=jax.ShapeDtypeStruct((bsz, seq, SSD_INNER), BF16),
        scratch_shapes=[pltpu.VMEM((SSD_STATE, gw), F32)],
        compiler_params=_cparams("arbitrary", "arbitrary"),
    )(xbc, xbc, xbc, z, dt_t, shift, spread, conv_w, conv_w, conv_w, cb2, cb2, cb2, dtb, alog, dsk)


def _first_index_of_max(vals, lane_f):
    m = jnp.max(vals, axis=-1, keepdims=True)
    idx = jnp.min(jnp.where(vals == m, lane_f, float(LANES)), axis=-1, keepdims=True)
    return m, idx


def _post_kernel(att_ref, u_ref, gates_ref, x_ref, watt_ref, wssd_ref, wout_ref,
                 gssd_ref, gffn_ref, wr_ref, br_ref, x1_ref, meta_ref, cnt_ref, run_ref):
    i = pl.program_id(0)

    @pl.when(i == 0)
    def _():
        run_ref[...] = jnp.zeros(run_ref.shape, F32)

    tm = x_ref.shape[0]
    nsub = 2
    subs = [{"rows": pl.ds(k * (tm // nsub), tm // nsub)} for k in range(nsub)]

    def att_branch(s):
        s["y_att"] = jnp.dot(att_ref[s["rows"], :], watt_ref[...], preferred_element_type=F32)

    def ssd_branch(s):
        ssd = _rms(u_ref[s["rows"], :].astype(F32), gssd_ref[...])
        s["y_ssd"] = jnp.dot(ssd.astype(BF16), wssd_ref[...], preferred_element_type=F32)

    def mix_out(s):
        gates = gates_ref[s["rows"], :].astype(F32)
        merged = _sigmoid(gates[:, :D_MODEL]) * s.pop("y_att") + _sigmoid(gates[:, D_MODEL:]) * s.pop("y_ssd")
        x1 = x_ref[s["rows"], :] + jnp.dot(merged.astype(BF16), wout_ref[...], preferred_element_type=F32)
        x1_ref[s["rows"], :] = x1
        s["h2"] = _rms(x1, gffn_ref[...])

    def router(s):
        h2 = s.pop("h2")
        hi = h2.astype(BF16)
        lo = (h2 - hi.astype(F32)).astype(BF16)
        s["logits"] = jnp.dot(jnp.concatenate([hi, lo, hi], axis=1), wr_ref[...], preferred_element_type=F32)

    for stage in (att_branch, ssd_branch, mix_out, router):
        for s in subs:
            stage(s)
    logits = jnp.concatenate([s["logits"] for s in subs], axis=0) + br_ref[...]
    lane = lax.broadcasted_iota(jnp.int32, (tm, LANES), 1)
    lane_f = lane.astype(F32)
    ninf = -jnp.inf
    gl = jnp.where(lane < MOE_GROUPS, logits, ninf)
    gmax, gidx = _first_index_of_max(gl, lane_f)
    g_val = 1.0 / jnp.sum(jnp.exp(gl - gmax), axis=-1, keepdims=True)
    base = MOE_GROUPS + MOE_EPG * gidx
    el = jnp.where((lane_f >= base) & (lane_f < base + MOE_EPG), logits, ninf)
    e1, i1 = _first_index_of_max(el, lane_f)
    e2, i2 = _first_index_of_max(jnp.where(lane_f == i1, ninf, el), lane_f)
    t2 = jnp.exp(e2 - e1)
    w1 = g_val / (1.0 + t2)
    w2 = g_val * t2 / (1.0 + t2)
    a1, a2 = i1 - base, i2 - base
    lo, hi = jnp.minimum(a1, a2), jnp.maximum(a1, a2)
    c_lo = jnp.where(a1 < a2, w1, w2)
    c_hi = jnp.where(a1 < a2, w2, w1)
    pair = lo * (7.0 - lo) * 0.5 + (hi - lo - 1.0)
    cls = gidx * MOE_PAIRS + pair

    onehot = (lane_f == cls)
    oh_b = jnp.where(onehot, 1.0, 0.0).astype(BF16)
    rr = lax.broadcasted_iota(jnp.int32, (tm, tm), 0)
    cc = lax.broadcasted_iota(jnp.int32, (tm, tm), 1)
    strict = jnp.where(rr > cc, 1.0, 0.0).astype(BF16)
    prefix = jnp.dot(strict, oh_b, preferred_element_type=F32) + run_ref[...]
    rank = jnp.sum(jnp.where(onehot, prefix, 0.0), axis=-1, keepdims=True)
    run = run_ref[...] + jnp.sum(oh_b.astype(F32), axis=0, keepdims=True)
    run_ref[...] = run
    cnt_ref[...] = jnp.broadcast_to(run, cnt_ref.shape)

    meta = jnp.where(lane == 0, cls, jnp.where(lane == 1, rank, jnp.where(lane == 2, c_lo,
                     jnp.where(lane == 3, c_hi, 0.0))))
    meta_ref[...] = meta


def _post(att, u, gates, x2d, w_att, w_ssd, w_out, g_ssd, g_ffn, w_r, b_r):
    t = x2d.shape[0]
    tm = TM_POST
    row = lambda w: pl.BlockSpec((tm, w), lambda i: (i, 0))
    const = lambda a: pl.BlockSpec(a.shape, lambda i: (0,) * a.ndim)
    return pl.pallas_call(
        _post_kernel,
        grid=(t // tm,),
        in_specs=[row(ATT_OUT), row(SSD_INNER), row(2 * D_MODEL), row(D_MODEL),
                  const(w_att), const(w_ssd), const(w_out), const(g_ssd), const(g_ffn), const(w_r), const(b_r)],
        out_specs=(row(D_MODEL), row(LANES), pl.BlockSpec((8, LANES), lambda i: (0, 0))),
        out_shape=(jax.ShapeDtypeStruct((t, D_MODEL), F32),
                   jax.ShapeDtypeStruct((t, LANES), F32),
                   jax.ShapeDtypeStruct((8, LANES), F32)),
        scratch_shapes=[pltpu.VMEM((1, LANES), F32)],
        compiler_params=_cparams("arbitrary"),
    )(att, u, gates, x2d, w_att, w_ssd, w_out, g_ssd, g_ffn, w_r, b_r)


def _scatter_kernel(pos_ref, x1_ref, g_ref, xs_in_ref, xs_ref, h_ref, sem):
    del xs_in_ref
    i = pl.program_id(0)
    tm = h_ref.shape[0]
    h_ref[...] = _rms(x1_ref[...], g_ref[...])

    for r in range(tm):
        pltpu.make_async_copy(h_ref.at[pl.ds(r, 1)], xs_ref.at[pl.ds(pos_ref[i * tm + r], 1)], sem).start()
    pltpu.make_async_copy(h_ref, xs_ref.at[pl.ds(0, tm)], sem).wait()


def _scatter_rows(pos, x1, g_ffn, xs_init):
    t = x1.shape[0]
    tm = TM_SCAT
    grid_spec = pltpu.PrefetchScalarGridSpec(
        num_scalar_prefetch=1,
        grid=(t // tm,),
        in_specs=[pl.BlockSpec((tm, D_MODEL), lambda i, pos: (i, 0)),
                  pl.BlockSpec((1, D_MODEL), lambda i, pos: (0, 0)),
                  pl.BlockSpec(memory_space=pl.ANY)],
        out_specs=pl.BlockSpec(memory_space=pl.ANY),
        scratch_shapes=[pltpu.VMEM((tm, D_MODEL), F32), pltpu.SemaphoreType.DMA(())],
    )
    return pl.pallas_call(
        _scatter_kernel,
        grid_spec=grid_spec,
        out_shape=jax.ShapeDtypeStruct(xs_init.shape, F32),
        input_output_aliases={3: 0},
        compiler_params=_cparams("arbitrary"),
    )(pos, x1, g_ffn, xs_init)


def _pack_pair(a, b):
    ua = lax.bitcast_convert_type(a.astype(BF16).astype(F32), jnp.uint32)
    ub = lax.bitcast_convert_type(b.astype(BF16).astype(F32), jnp.uint32)
    return (ua & jnp.uint32(0xFFFF0000)) | (ub >> 16)


def _unpack_pair(p):
    a = lax.bitcast_convert_type(p & jnp.uint32(0xFFFF0000), F32)
    b = lax.bitcast_convert_type(p << 16, F32)
    return a, b


def _expert_kernel(nused_ref, elo_ref, ehi_ref, xs_ref, wg_lo, wu_lo, wd_lo, wg_hi, wu_hi, wd_hi, ys_ref):
    del elo_ref, ehi_ref
    i = pl.program_id(0)

    @pl.when(i < nused_ref[0])
    def _():
        xb = xs_ref[...].astype(BF16)

        def mlp(wg, wu, wd):
            gt = jnp.dot(xb, wg[0], preferred_element_type=F32)
            up = jnp.dot(xb, wu[0], preferred_element_type=F32)
            hid = _silu(gt) * up
            return jnp.dot(hid.astype(BF16), wd[0], preferred_element_type=F32)

        ys_ref[...] = _pack_pair(mlp(wg_lo, wu_lo, wd_lo), mlp(wg_hi, wu_hi, wd_hi))

    @pl.when(i >= nused_ref[0])
    def _():
        ys_ref[...] = jnp.zeros(ys_ref.shape, jnp.uint32)


def _experts(n_used, tile_elo, tile_ehi, xs, w_gate, w_up, w_down):
    p_rows = xs.shape[0]
    tm = TM_EXP
    n_tiles = p_rows // tm

    def last_used(i, nu):
        return jnp.minimum(i, jnp.maximum(nu[0] - 1, 0))

    def row_map(i, nu, elo, ehi):
        return (last_used(i, nu), 0)

    def wmap(which):
        def f(i, nu, elo, ehi):
            return ((elo, ehi)[which][last_used(i, nu)], 0, 0)
        return f

    wspec_in = lambda which: pl.BlockSpec((1, D_MODEL, MOE_HIDDEN), wmap(which))
    wspec_out = lambda which: pl.BlockSpec((1, MOE_HIDDEN, D_MODEL), wmap(which))
    grid_spec = pltpu.PrefetchScalarGridSpec(
        num_scalar_prefetch=3,
        grid=(n_tiles,),
        in_specs=[pl.BlockSpec((tm, D_MODEL), row_map),
                  wspec_in(0), wspec_in(0), wspec_out(0), wspec_in(1), wspec_in(1), wspec_out(1)],
        out_specs=pl.BlockSpec((tm, D_MODEL), lambda i, nu, elo, ehi: (i, 0)),
    )
    return pl.pallas_call(
        _expert_kernel,
        grid_spec=grid_spec,
        out_shape=jax.ShapeDtypeStruct((p_rows, D_MODEL), jnp.uint32),
        compiler_params=_cparams("arbitrary"),
    )(n_used, tile_elo, tile_ehi, xs, w_gate, w_up, w_down, w_gate, w_up, w_down)


def _final_kernel(pos_ref, x1_ref, meta_ref, p_ref, ys_ref, gple_ref, wpg_ref, wpp_ref, gfin_ref,
                  out_ref, buf_ref, sem, *, n_steps):
    i = pl.program_id(0)
    tm = x1_ref.shape[0]

    def issue_tile(step, slot):
        for r in range(tm):
            pltpu.make_async_copy(ys_ref.at[pl.ds(pos_ref[step * tm + r], 1)],
                                  buf_ref.at[slot, pl.ds(r, 1)], sem.at[slot]).start()

    @pl.when(i == 0)
    def _():
        issue_tile(0, 0)

    slot = i % 2
    for nxt in range(2):
        @pl.when((i + 1 < n_steps) & (1 - slot == nxt))
        def _(nxt=nxt):
            issue_tile(i + 1, nxt)

    pltpu.make_async_copy(ys_ref.at[pl.ds(0, tm)], buf_ref.at[slot], sem.at[slot]).wait()

    y_lo, y_hi = _unpack_pair(buf_ref[slot])
    meta = meta_ref[...]
    x2 = x1_ref[...] + meta[:, 2:3] * y_lo + meta[:, 3:4] * y_hi
    hn = _rms(x2, gple_ref[...])
    gate = _sigmoid(jnp.dot(hn.astype(BF16), wpg_ref[...], preferred_element_type=F32))
    pp = jnp.dot(p_ref[...].astype(BF16), wpp_ref[...], preferred_element_type=F32)
    out_ref[...] = _rms(x2 + gate * pp, gfin_ref[...])


def _final(pos, x1, meta, p2d, ys, g_ple, w_pg, w_pp, g_fin):
    t = x1.shape[0]
    tm = TM_FIN
    n_steps = t // tm
    row = lambda w: pl.BlockSpec((tm, w), lambda i, pos: (i, 0))
    const = lambda a: pl.BlockSpec(a.shape, lambda i, pos: (0,) * a.ndim)
    grid_spec = pltpu.PrefetchScalarGridSpec(
        num_scalar_prefetch=1,
        grid=(n_steps,),
        in_specs=[row(D_MODEL), row(LANES), row(PLE_DIM), pl.BlockSpec(memory_space=pl.ANY),
                  const(g_ple), const(w_pg), const(w_pp), const(g_fin)],
        out_specs=row(D_MODEL),
        scratch_shapes=[pltpu.VMEM((2, tm, D_MODEL), jnp.uint32), pltpu.SemaphoreType.DMA((2,))],
    )
    return pl.pallas_call(
        functools.partial(_final_kernel, n_steps=n_steps),
        grid_spec=grid_spec,
        out_shape=jax.ShapeDtypeStruct((t, D_MODEL), F32),
        compiler_params=_cparams("arbitrary"),
    )(pos, x1, meta, p2d, ys, g_ple, w_pg, w_pp, g_fin)


_PAIR_LO = np.array([0, 0, 0, 1, 1, 2], np.int32)
_PAIR_HI = np.array([1, 2, 3, 2, 3, 3], np.int32)


def _routing_tables(meta, counts_f, n_tiles):
    cls = meta[:, 0].astype(jnp.int32)
    rank = meta[:, 1].astype(jnp.int32)
    counts = counts_f[0, :MOE_CLASSES].astype(jnp.int32)
    tiles_per = (counts + TM_EXP - 1) // TM_EXP
    tile_end = jnp.cumsum(tiles_per)
    tile_start = tile_end - tiles_per
    class_ids = jnp.arange(MOE_CLASSES, dtype=jnp.int32)
    pos = jnp.sum(jnp.where(cls[:, None] == class_ids[None, :], (tile_start * TM_EXP)[None, :], 0), axis=1) + rank
    n_used = tile_end[-1:]
    tile_ids = jnp.arange(n_tiles, dtype=jnp.int32)
    tile_cls = jnp.minimum(jnp.sum((tile_end[None, :] <= tile_ids[:, None]).astype(jnp.int32), axis=1),
                           MOE_CLASSES - 1)
    grp = tile_cls // MOE_PAIRS
    pair = tile_cls % MOE_PAIRS
    tile_elo = grp * MOE_EPG + jnp.asarray(_PAIR_LO)[pair]
    tile_ehi = grp * MOE_EPG + jnp.asarray(_PAIR_HI)[pair]
    return pos.astype(jnp.int32), n_used.astype(jnp.int32), tile_elo.astype(jnp.int32), tile_ehi.astype(jnp.int32)


def kernel(x, p, norm_mix_g, w_in, conv_w, conv_b, dt_bias, a_log, d_skip, ssd_norm_g, w_att_branch,
           w_ssd_branch, w_out, norm_ffn_g, w_router_group, b_router_group, w_router_expert,
           b_router_expert, w_exp_gate, w_exp_up, w_exp_down, norm_ple_g, w_ple_gate, w_ple_proj,
           final_norm_g):
    bsz, seq, _ = x.shape
    t = bsz * seq
    assert w_in.shape[0] == 1, "single-layer block"
    assert seq // ATT_PATTERNS[-1][1] == ATT_BLOCK and seq % TM_IN == 0
    x2d = x.reshape(t, D_MODEL)

    wi = w_in[0]
    c_dt = QKV_W + SSD_INNER + SSD_CONV_CH
    qkv_cols = [wi[:, sel * ATT_WIDTH + g * ATT_OUT: sel * ATT_WIDTH + (g + 1) * ATT_OUT]
                for g in range(ATT_GROUPS) for sel in range(3)]
    w_main = jnp.concatenate(qkv_cols + [wi[:, QKV_W:c_dt], wi[:, c_dt + SSD_HEADS:]], axis=1).astype(BF16)
    w_dt = jnp.pad(wi[:, c_dt:c_dt + SSD_HEADS], ((0, 0), (0, LANES - SSD_HEADS))).astype(BF16)
    row = lambda v: v.reshape(1, -1).astype(F32)

    qkv1, qkv2, qkv3, z, xbc, gates, dt_raw = _in_proj(x2d, row(norm_mix_g[0]), w_main, w_dt, bsz, seq)

    att = _attention(qkv1.reshape(bsz, seq, ATT_WIDTH), qkv2, qkv3, bsz, seq).reshape(t, ATT_OUT)

    dt_t = dt_raw[:, :SSD_HEADS].reshape(bsz, seq, SSD_GROUPS, SSD_HPG).transpose(0, 2, 3, 1)
    u = _ssd(xbc.reshape(bsz, seq, SSD_CONV_CH), z.reshape(bsz, seq, SSD_INNER), dt_t,
             conv_w[0], conv_b[0], dt_bias[0], a_log[0], d_skip[0], bsz, seq).reshape(t, SSD_INNER)

    w_r32 = jnp.pad(jnp.concatenate([w_router_group[0], w_router_expert[0]], axis=1),
                    ((0, 0), (0, LANES - MOE_GROUPS - MOE_EXPERTS))).astype(F32)
    w_r_hi = w_r32.astype(BF16)
    w_r_lo = (w_r32 - w_r_hi.astype(F32)).astype(BF16)
    w_r = jnp.concatenate([w_r_hi, w_r_hi, w_r_lo], axis=0)
    b_r = jnp.pad(jnp.concatenate([b_router_group[0], b_router_expert[0]]),
                  (0, LANES - MOE_GROUPS - MOE_EXPERTS)).reshape(1, LANES).astype(F32)
    x1, meta, counts = _post(att, u, gates, x2d,
                             w_att_branch[0].astype(BF16), w_ssd_branch[0].astype(BF16), w_out[0].astype(BF16),
                             row(ssd_norm_g[0]), row(norm_ffn_g[0]), w_r, b_r)

    n_tiles = t // TM_EXP + MOE_CLASSES
    pos, n_used, tile_elo, tile_ehi = _routing_tables(meta, counts, n_tiles)
    xs = _scatter_rows(pos, x1, row(norm_ffn_g[0]), jnp.zeros((n_tiles * TM_EXP, D_MODEL), F32))
    ys = _experts(n_used, tile_elo, tile_ehi, xs,
                  w_exp_gate[0].astype(BF16), w_exp_up[0].astype(BF16), w_exp_down[0].astype(BF16))
    out = _final(pos, x1, meta, p[0].reshape(t, PLE_DIM), ys, row(norm_ple_g[0]),
                 w_ple_gate[0].astype(BF16), w_ple_proj[0].astype(BF16), row(final_norm_g))
    return out.reshape(bsz, seq, D_MODEL)
```

```python
import functools

import numpy as np
import jax
import jax.numpy as jnp
from jax import lax
from jax.experimental import pallas as pl
from jax.experimental.pallas import tpu as pltpu

F32 = jnp.float32
BF16 = jnp.bfloat16

D_MODEL = 1024
PLE_DIM = 256
RMS_EPS = 1e-6

ATT_PATTERNS = ((128, 1), (512, 4), (2048, 16))
ATT_GROUPS = 3
ATT_HPG = 8
ATT_HEAD_DIM = 64
ATT_WIDTH = ATT_GROUPS * ATT_HPG * ATT_HEAD_DIM
ATT_OUT = ATT_HPG * ATT_HEAD_DIM
ATT_BLOCK = 128
ATT_LOCKSTEP = 2
ALIBI_MAX_BIAS = 8.0
QKV_W = 3 * ATT_WIDTH

SSD_INNER = 2048
SSD_HEADS = 32
SSD_GROUPS = 8
SSD_HPG = 4
SSD_HEAD_DIM = 64
SSD_STATE = 128
SSD_CONV = 4
SSD_CHUNK = 128
SSD_CONV_CH = SSD_INNER + 2 * SSD_GROUPS * SSD_STATE
SSD_GW = SSD_HPG * SSD_HEAD_DIM
SSD_LOCKSTEP = 4

MOE_GROUPS = 4
MOE_EPG = 4
MOE_EXPERTS = 16
MOE_HIDDEN = 512
MOE_PAIRS = 6
MOE_CLASSES = MOE_GROUPS * MOE_PAIRS

LANES = 128
NEG_BIG = -1e30
VMEM_LIMIT = 56 * 1024 * 1024

TM_IN = 1024
TN_IN = 512
TM_POST = 512
TM_SCAT = 512
TM_EXP = 256
TM_FIN = 512


def _cparams(*sem):
    return pltpu.CompilerParams(dimension_semantics=sem, vmem_limit_bytes=VMEM_LIMIT)


def _sigmoid(x):
    return 0.5 * jnp.tanh(0.5 * x) + 0.5


def _silu(x):
    h = 0.5 * x
    return h + h * jnp.tanh(h)


def _rms(x, g):
    ms = jnp.mean(x * x, axis=-1, keepdims=True)
    return x * lax.rsqrt(ms + RMS_EPS) * g


_IN_SEG = (3, 3, 3, SSD_INNER // TN_IN, SSD_CONV_CH // TN_IN, 2 * D_MODEL // TN_IN)
_IN_START = tuple(int(v) for v in np.cumsum((0,) + _IN_SEG))


def _inproj_kernel(x_ref, g_ref, w_ref, wdt_ref, qkv1_ref, qkv2_ref, qkv3_ref, z_ref, xbc_ref, gates_ref,
                   dt_ref, h_ref, hcol_ref):
    j = pl.program_id(1)
    tm = x_ref.shape[0]

    @pl.when(j == 0)
    def _():
        h = _rms(x_ref[...], g_ref[...])
        hb = h.astype(BF16)
        h_ref[0] = hb
        dt_ref[...] = jnp.dot(hb, wdt_ref[...], preferred_element_type=F32)
        ncb = hcol_ref.shape[0]
        for c in range(ncb):
            hcol_ref[c] = h[:, c * LANES:(c + 1) * LANES]
        for slot, (_, dil) in enumerate(ATT_PATTERNS[1:], start=1):
            rows = tm // dil
            for r in range(dil):
                for c in range(ncb):
                    h_ref[slot, r * rows:(r + 1) * rows, c * LANES:(c + 1) * LANES] = (
                        hcol_ref[c, pl.ds(r, rows, stride=dil), :].astype(BF16))

    slot = jnp.where(j < _IN_START[1], 0, jnp.where(j < _IN_START[2], 1, jnp.where(j < _IN_START[3], 2, 0)))
    res = jnp.dot(h_ref[slot], w_ref[...], preferred_element_type=F32).astype(BF16)

    def store(k, fn):
        @pl.when((j >= _IN_START[k]) & (j < _IN_START[k + 1]))
        def _():
            fn()

    def store_grouped(ref, dil):
        def fn():
            ref[0] = res.reshape(dil, tm // dil, res.shape[1])
        return fn

    def store_plain(ref):
        def fn():
            ref[...] = res
        return fn

    store(0, store_plain(qkv1_ref))
    store(1, store_grouped(qkv2_ref, ATT_PATTERNS[1][1]))
    store(2, store_grouped(qkv3_ref, ATT_PATTERNS[2][1]))
    store(3, store_plain(z_ref))
    store(4, store_plain(xbc_ref))
    store(5, store_plain(gates_ref))


def _in_proj(x2d, g, w_main, w_dt, bsz, seq):
    t = x2d.shape[0]
    nj = _IN_START[-1]
    per_seq = seq // TM_IN
    d2, d3 = ATT_PATTERNS[1][1], ATT_PATTERNS[2][1]

    def seg_map(k):
        return lambda i, j: (i, jnp.clip(j - _IN_START[k], 0, _IN_SEG[k] - 1))

    def grp_map(k):
        return lambda i, j: (i // per_seq, 0, i % per_seq, jnp.clip(j - _IN_START[k], 0, _IN_SEG[k] - 1))

    out_shapes = (
        jax.ShapeDtypeStruct((t, ATT_WIDTH), BF16),
        jax.ShapeDtypeStruct((bsz, d2, seq // d2, ATT_WIDTH), BF16),
        jax.ShapeDtypeStruct((bsz, d3, seq // d3, ATT_WIDTH), BF16),
        jax.ShapeDtypeStruct((t, SSD_INNER), BF16),
        jax.ShapeDtypeStruct((t, SSD_CONV_CH), BF16),
        jax.ShapeDtypeStruct((t, 2 * D_MODEL), BF16),
        jax.ShapeDtypeStruct((t, LANES), F32),
    )
    out_specs = (
        pl.BlockSpec((TM_IN, TN_IN), seg_map(0)),
        pl.BlockSpec((1, d2, TM_IN // d2, TN_IN), grp_map(1)),
        pl.BlockSpec((1, d3, TM_IN // d3, TN_IN), grp_map(2)),
        pl.BlockSpec((TM_IN, TN_IN), seg_map(3)),
        pl.BlockSpec((TM_IN, TN_IN), seg_map(4)),
        pl.BlockSpec((TM_IN, TN_IN), seg_map(5)),
        pl.BlockSpec((TM_IN, LANES), lambda i, j: (i, 0)),
    )
    return pl.pallas_call(
        _inproj_kernel,
        grid=(t // TM_IN, nj),
        in_specs=[
            pl.BlockSpec((TM_IN, D_MODEL), lambda i, j: (i, 0)),
            pl.BlockSpec((1, D_MODEL), lambda i, j: (0, 0)),
            pl.BlockSpec((D_MODEL, TN_IN), lambda i, j: (0, j)),
            pl.BlockSpec((D_MODEL, LANES), lambda i, j: (0, 0)),
        ],
        out_specs=out_specs,
        out_shape=out_shapes,
        scratch_shapes=[pltpu.VMEM((3, TM_IN, D_MODEL), BF16),
                        pltpu.VMEM((D_MODEL // LANES, TM_IN, LANES), F32)],
        compiler_params=_cparams("arbitrary", "arbitrary"),
    )(x2d, g, w_main, w_dt)


def _att_bias_tables():
    h = np.arange(1, ATT_GROUPS * ATT_HPG + 1, dtype=np.float32)
    slopes = np.exp2(-ALIBI_MAX_BIAS * h / (ATT_GROUPS * ATT_HPG)).astype(np.float32).reshape(ATT_GROUPS, ATT_HPG)
    qi = np.arange(ATT_BLOCK)[:, None] + ATT_BLOCK
    kj = np.arange(2 * ATT_BLOCK)[None, :]
    delta = qi - kj
    tabs = []
    for g, (window, dil) in enumerate(ATT_PATTERNS):
        span = window // dil
        band = (delta >= 0) & (delta <= span)
        bias = (-slopes[g][:, None, None] * (delta * dil).astype(np.float32)[None]).astype(np.float32)
        with_prev = np.where(band[None], bias, np.float32(NEG_BIG))
        first = np.where((band & (kj >= ATT_BLOCK))[None], bias, np.float32(NEG_BIG))
        tabs.append(np.stack([first, with_prev]).astype(np.float32))
    return tabs[0], tabs[1], tabs[2][1][:, :, ATT_BLOCK:]


def _att_units(units):
    lane = lax.broadcasted_iota(jnp.int32, (1, LANES), 1)
    low = lane < ATT_HEAD_DIM
    scale = ATT_HEAD_DIM ** -0.5
    qmask = (jnp.where(low, scale, 0.0).astype(BF16), jnp.where(low, 0.0, scale).astype(BF16))
    heads = [(q, k2, v2, tab_fn, hh) for q, k2, v2, tab_fn in units for hh in range(2)]
    scores = [lax.dot_general(q * qmask[hh], k2, (((1,), (1,)), ((), ())), preferred_element_type=F32)
              for q, k2, _, _, hh in heads]
    probs = []
    for s, (_, _, _, tab_fn, hh) in zip(scores, heads):
        s = s + tab_fn(hh)
        m = jnp.max(s, axis=-1, keepdims=True)
        e = jnp.exp(s - m)
        probs.append((e.astype(BF16), m, jnp.sum(e, axis=-1, keepdims=True)))
    pvs = [jnp.dot(e, v2, preferred_element_type=F32) for (e, _, _), (_, _, v2, _, _) in zip(probs, heads)]
    outs = [pv / den for pv, (_, _, den) in zip(pvs, probs)]
    lses = [m + jnp.log(den) for _, m, den in probs]
    return [(jnp.where(low, outs[2 * i], outs[2 * i + 1]), jnp.where(low, lses[2 * i], lses[2 * i + 1]))
            for i in range(len(units))]


def _att_kernel(q1, k1, v1, q2, k2, v2, q3, k3, v3, t1, t2, t3, out_ref,
                o1, l1, o2p, l2p, o2n, l2n, o3p, l3p, o3n, l3n, *, seq):
    blk = ATT_BLOCK
    d2, d3 = ATT_PATTERNS[1][1], ATT_PATTERNS[2][1]
    nb1, nb2 = seq // blk, seq // d2 // blk

    def rows(n):
        return pl.ds(pl.multiple_of(n * blk, blk), blk)

    assert nb1 == d2 * nb2 == d3

    def unit_body(i, _):
        units, dests = [], []
        for k in range(ATT_LOCKSTEP):
            u = i * ATT_LOCKSTEP + k
            cur, prv = rows(u), rows(jnp.maximum(u - 1, 0))
            sel = jnp.minimum(u, 1)
            units.append((q1[0, cur, :], jnp.concatenate([k1[0, prv, :], k1[0, cur, :]], axis=0),
                          jnp.concatenate([v1[0, prv, :], v1[0, cur, :]], axis=0),
                          lambda hh, sel=sel: t1[sel, hh]))
            r, n = u // nb2, u % nb2
            cur2, prv2 = rows(n), rows(jnp.maximum(n - 1, 0))
            sel2 = jnp.minimum(n, 1)
            units.append((q2[0, r, cur2, :], jnp.concatenate([k2[0, r, prv2, :], k2[0, r, cur2, :]], axis=0),
                          jnp.concatenate([v2[0, r, prv2, :], v2[0, r, cur2, :]], axis=0),
                          lambda hh, sel2=sel2: t2[sel2, hh]))
            units.append((q3[0, u], k3[0, u], v3[0, u], lambda hh: t3[hh]))
            dests += [(o1, l1, cur), (o2p, l2p, cur), (o3p, l3p, cur)]
        for (o, l), (o_ref, l_ref, where) in zip(_att_units(units), dests):
            o_ref[where, :] = o
            l_ref[where, :] = l
        return 0

    lax.fori_loop(0, nb1 // ATT_LOCKSTEP, unit_body, 0)

    for dil, pairs in ((d2, ((o2p, o2n), (l2p, l2n))), (d3, ((o3p, o3n), (l3p, l3n)))):
        n_sub = seq // dil
        for r in range(dil):
            for src, dst in pairs:
                dst[pl.ds(r, n_sub, stride=dil), :] = src[r * n_sub:(r + 1) * n_sub, :]

    mrows = 2 * blk

    def merge(c, _):
        rr = pl.ds(pl.multiple_of(c * mrows, mrows), mrows)
        la, lb, lc = l1[rr, :], l2n[rr, :], l3n[rr, :]
        lm = jnp.maximum(jnp.maximum(la, lb), lc)
        ea, eb, ec = jnp.exp(la - lm), jnp.exp(lb - lm), jnp.exp(lc - lm)
        att = (ea * o1[rr, :] + eb * o2n[rr, :] + ec * o3n[rr, :]) / (ea + eb + ec)
        out_ref[0, rr, :] = att.astype(BF16)
        return 0

    lax.fori_loop(0, seq // mrows, merge, 0)


def _attention(qkv1, qkv2, qkv3, bsz, seq):
    t1, t2, t3 = (jnp.asarray(t) for t in _att_bias_tables())
    d2, d3 = ATT_PATTERNS[1][1], ATT_PATTERNS[2][1]
    npair = ATT_HPG // 2
    in_arrays, in_specs = [], []
    for arr, lead in ((qkv1, ()), (qkv2, (d2,)), (qkv3, (d3,))):
        n_rows = arr.shape[-2]
        for sel in range(3):
            zeros = (0,) * len(lead)
            in_arrays.append(arr)
            in_specs.append(pl.BlockSpec((1,) + lead + (n_rows, LANES),
                                         lambda hp, b, sel=sel, zeros=zeros: (b,) + zeros + (0, sel * npair + hp)))
    in_arrays += [t1, t2, t3]
    in_specs += [pl.BlockSpec((2, 2, ATT_BLOCK, 2 * ATT_BLOCK), lambda hp, b: (0, hp, 0, 0)),
                 pl.BlockSpec((2, 2, ATT_BLOCK, 2 * ATT_BLOCK), lambda hp, b: (0, hp, 0, 0)),
                 pl.BlockSpec((2, ATT_BLOCK, ATT_BLOCK), lambda hp, b: (hp, 0, 0))]
    return pl.pallas_call(
        functools.partial(_att_kernel, seq=seq),
        grid=(npair, bsz),
        in_specs=in_specs,
        out_specs=pl.BlockSpec((1, seq, LANES), lambda hp, b: (b, 0, hp)),
        out_shape=jax.ShapeDtypeStruct((bsz, seq, ATT_OUT), BF16),
        scratch_shapes=[pltpu.VMEM((seq, LANES), F32)] * 10,
        compiler_params=_cparams("arbitrary", "arbitrary"),
    )(*in_arrays)


def _softplus(x):
    return jnp.maximum(x, 0.0) + jnp.log1p(jnp.exp(-jnp.abs(x)))


def _conv_shift_table():
    L = SSD_CHUNK
    l = np.arange(L)[:, None]
    j = np.arange(2 * L)[None, :]
    tab = np.zeros((2, SSD_CONV, L, 2 * L), np.float32)
    for kk in range(SSD_CONV):
        sh = SSD_CONV - 1 - kk
        tab[0, kk] = (j == l - sh)
        tab[1, kk] = (j == L + l - sh)
    return tab.reshape(2, SSD_CONV * L, 2 * L)


def _head_spread_table():
    gw, n_src = SSD_GW, LANES
    tab = np.zeros((2 * n_src, 2 * gw), np.float32)
    for half in range(2):
        for blk, src0 in enumerate((4, 8)):
            for col in range(gw):
                tab[half * n_src + src0 + col // SSD_HEAD_DIM, blk * gw + col] = 1.0
    return tab


def _ssd_kernel(x_ref, b_ref, c_ref, z_ref, dt_ref, shift_ref, spread_ref, wx_ref, wb_ref, wc_ref,
                bx_ref, bb_ref, bc_ref, dtb_ref, alog_ref, dsk_ref, u_ref, carry_ref, *, n_chunks):
    L = SSD_CHUNK
    gw = SSD_GW
    carry_ref[...] = jnp.zeros(carry_ref.shape, F32)

    w = jnp.concatenate([wx_ref[...], wb_ref[...], wc_ref[...]], axis=1)
    bias = jnp.concatenate([bx_ref[...], bb_ref[...], bc_ref[...]], axis=1)
    a_neg = -jnp.exp(alog_ref[0])
    dtb = dtb_ref[0]
    dsk = dsk_ref[0]
    ri = lax.broadcasted_iota(jnp.int32, (L, L), 0)
    ci = lax.broadcasted_iota(jnp.int32, (L, L), 1)
    upper_incl = (ri <= ci).astype(F32)
    causal = ri >= ci
    lane = lax.broadcasted_iota(jnp.int32, (1, LANES), 1)
    low = lane < SSD_HEAD_DIM
    heads = range(SSD_HPG)

    def conv_taps(s):
        c = s["c"]
        rs = pl.multiple_of(jnp.maximum(c - 1, 0) * L, L)
        win = jnp.concatenate([x_ref[0, pl.ds(rs, 2 * L), :], b_ref[0, pl.ds(rs, 2 * L), :],
                               c_ref[0, pl.ds(rs, 2 * L), :]], axis=1)
        s["taps"] = jnp.dot(shift_ref[jnp.minimum(c, 1)], win, preferred_element_type=F32)

    def conv_act(s):
        taps = s.pop("taps")
        acc = bias
        for kk in range(SSD_CONV):
            acc = acc + w[kk:kk + 1, :] * taps[kk * L:(kk + 1) * L]
        xc = _silu(acc)
        s["xs"], s["bm"], s["cm_b"] = xc[:, :gw], xc[:, gw:gw + SSD_STATE], xc[:, gw + SSD_STATE:].astype(BF16)

    def decay_cumsum(s):
        dt = _softplus(dt_ref[0, 0, :, pl.ds(s["r0"], L)] + dtb)
        rows8 = jnp.concatenate([dt * a_neg, dt], axis=0)
        cs8 = jnp.dot(rows8, upper_incl, preferred_element_type=F32, precision=lax.Precision.HIGHEST)
        s["dt"], s["acs_t"] = dt, cs8[0:SSD_HPG]

    def decay_spread(s):
        dt, acs_t = s["dt"], s["acs_t"]
        t16 = jnp.concatenate([acs_t, acs_t, acs_t, dt], axis=0)
        cols = jnp.concatenate([t16, jnp.zeros((L - 16, L), F32)], axis=0).T
        dt_at8 = pltpu.roll(cols, LANES - 4, 1)
        comb = jnp.where(lane < 8, jnp.exp(cols), jnp.exp(cols[L - 1:L, :] - cols) * dt_at8)
        hi = comb.astype(BF16)
        lo = (comb - hi.astype(F32)).astype(BF16)
        s["spread"] = jnp.dot(jnp.concatenate([hi, lo], axis=1), spread_ref[...], preferred_element_type=F32)
        s["acs_cols"] = cols

    def scores(s):
        s["bm_t"] = s.pop("bm").T.astype(BF16)
        s["cb"] = jnp.dot(s["cm_b"], s["bm_t"], preferred_element_type=F32)

    def intra(s):
        spread, acs_t, dt, cb, cols = s["spread"], s.pop("acs_t"), s.pop("dt"), s.pop("cb"), s.pop("acs_cols")
        xs = s["xs"]
        xs_b = xs.astype(BF16)
        mixes = []
        for j in heads:
            seg = cols[:, j:j + 1] - acs_t[j:j + 1, :]
            mixes.append((jnp.exp(jnp.where(causal, seg, -jnp.inf)) * (cb * dt[j:j + 1, :])).astype(BF16))
        halves = []
        for hp in range(SSD_HPG // 2):
            yy = jnp.dot(jnp.concatenate(mixes[2 * hp:2 * hp + 2], axis=0), xs_b[:, hp * LANES:(hp + 1) * LANES],
                         preferred_element_type=F32)
            halves.append(jnp.where(low, yy[:L], yy[L:]))
        s["y"] = jnp.concatenate(halves, axis=1)
        s["st_new"] = jnp.dot(s.pop("bm_t"), (xs * spread[:, gw:]).astype(BF16),
                              preferred_element_type=F32)

    def inter(s):
        eacs_bc = s.pop("spread")[:, :gw]
        carry = carry_ref[...]
        y = s.pop("y") + jnp.dot(s.pop("cm_b"), carry.astype(BF16), preferred_element_type=F32) * eacs_bc
        carry_ref[...] = carry * eacs_bc[L - 1:L, :] + s.pop("st_new")
        y = y + s.pop("xs") * dsk
        rows = pl.ds(s["r0"], L)
        u_ref[0, rows, :] = (y * _silu(z_ref[0, rows, :].astype(F32))).astype(BF16)

    def body(i, _):
        states = [{"c": i * SSD_LOCKSTEP + k, "r0": pl.multiple_of((i * SSD_LOCKSTEP + k) * L, L)}
                  for k in range(SSD_LOCKSTEP)]
        for stage in (conv_taps, decay_cumsum, conv_act, decay_spread, scores, intra, inter):
            for s in states:
                stage(s)
        return 0

    lax.fori_loop(0, n_chunks // SSD_LOCKSTEP, body, 0)


def _ssd(xbc, z, dt_t, conv_w, conv_b, dt_bias, a_log, d_skip, bsz, seq):
    gw = SSD_GW
    nxb = SSD_INNER // SSD_STATE
    dsk = jnp.repeat(d_skip.astype(F32), SSD_HEAD_DIM).reshape(SSD_GROUPS, 1, gw)
    dtb = dt_bias.astype(F32).reshape(SSD_GROUPS, SSD_HPG, 1)
    alog = a_log.astype(F32).reshape(SSD_GROUPS, SSD_HPG, 1)
    cb2 = conv_b.reshape(1, SSD_CONV_CH)
    shift = jnp.asarray(_conv_shift_table(), BF16)
    spread = jnp.asarray(_head_spread_table(), BF16)
    x_map = lambda b, g: (b, 0, g)
    bm_map = lambda b, g: (b, 0, nxb + g)
    cm_map = lambda b, g: (b, 0, nxb + SSD_GROUPS + g)
    return pl.pallas_call(
        functools.partial(_ssd_kernel, n_chunks=seq // SSD_CHUNK),
        grid=(bsz, SSD_GROUPS),
        in_specs=[
            pl.BlockSpec((1, seq, gw), x_map),
            pl.BlockSpec((1, seq, SSD_STATE), bm_map),
            pl.BlockSpec((1, seq, SSD_STATE), cm_map),
            pl.BlockSpec((1, seq, gw), x_map),
            pl.BlockSpec((1, 1, SSD_HPG, seq), lambda b, g: (b, g, 0, 0)),
            pl.BlockSpec(shift.shape, lambda b, g: (0, 0, 0)),
            pl.BlockSpec(spread.shape, lambda b, g: (0, 0)),
            pl.BlockSpec((SSD_CONV, gw), lambda b, g: (0, g)),
            pl.BlockSpec((SSD_CONV, SSD_STATE), lambda b, g: (0, nxb + g)),
            pl.BlockSpec((SSD_CONV, SSD_STATE), lambda b, g: (0, nxb + SSD_GROUPS + g)),
            pl.BlockSpec((1, gw), lambda b, g: (0, g)),
            pl.BlockSpec((1, SSD_STATE), lambda b, g: (0, nxb + g)),
            pl.BlockSpec((1, SSD_STATE), lambda b, g: (0, nxb + SSD_GROUPS + g)),
            pl.BlockSpec((1, SSD_HPG, 1), lambda b, g: (g, 0, 0)),
            pl.BlockSpec((1, SSD_HPG, 1), lambda b, g: (g, 0, 0)),
            pl.BlockSpec((1, 1, gw), lambda b, g: (g, 0, 0)),
        ],
        out_specs=pl.BlockSpec((1, seq, gw), x_map),
        out_shape=jax.ShapeDtypeStruct((bsz, seq, SSD_INNER), BF16),
        scratch_shapes=[pltpu.VMEM((SSD_STATE, gw), F32)],
        compiler_params=_cparams("arbitrary", "arbitrary"),
    )(xbc, xbc, xbc, z, dt_t, shift, spread, conv_w, conv_w, conv_w, cb2, cb2, cb2, dtb, alog, dsk)


def _first_index_of_max(vals, lane_f):
    m = jnp.max(vals, axis=-1, keepdims=True)
    idx = jnp.min(jnp.where(vals == m, lane_f, float(LANES)), axis=-1, keepdims=True)
    return m, idx


def _post_kernel(att_ref, u_ref, gates_ref, x_ref, watt_ref, wssd_ref, wout_ref,
                 gssd_ref, gffn_ref, wr_ref, br_ref, x1_ref, meta_ref, cnt_ref, run_ref):
    i = pl.program_id(0)

    @pl.when(i == 0)
    def _():
        run_ref[...] = jnp.zeros(run_ref.shape, F32)

    tm = x_ref.shape[0]
    nsub = 2
    subs = [{"rows": pl.ds(k * (tm // nsub), tm // nsub)} for k in range(nsub)]

    def att_branch(s):
        s["y_att"] = jnp.dot(att_ref[s["rows"], :], watt_ref[...], preferred_element_type=F32)

    def ssd_branch(s):
        ssd = _rms(u_ref[s["rows"], :].astype(F32), gssd_ref[...])
        s["y_ssd"] = jnp.dot(ssd.astype(BF16), wssd_ref[...], preferred_element_type=F32)

    def mix_out(s):
        gates = gates_ref[s["rows"], :].astype(F32)
        merged = _sigmoid(gates[:, :D_MODEL]) * s.pop("y_att") + _sigmoid(gates[:, D_MODEL:]) * s.pop("y_ssd")
        x1 = x_ref[s["rows"], :] + jnp.dot(merged.astype(BF16), wout_ref[...], preferred_element_type=F32)
        x1_ref[s["rows"], :] = x1
        s["h2"] = _rms(x1, gffn_ref[...])

    def router(s):
        h2 = s.pop("h2")
        hi = h2.astype(BF16)
        lo = (h2 - hi.astype(F32)).astype(BF16)
        s["logits"] = jnp.dot(jnp.concatenate([hi, lo, hi], axis=1), wr_ref[...], preferred_element_type=F32)

    for stage in (att_branch, ssd_branch, mix_out, router):
        for s in subs:
            stage(s)
    logits = jnp.concatenate([s["logits"] for s in subs], axis=0) + br_ref[...]
    lane = lax.broadcasted_iota(jnp.int32, (tm, LANES), 1)
    lane_f = lane.astype(F32)
    ninf = -jnp.inf
    gl = jnp.where(lane < MOE_GROUPS, logits, ninf)
    gmax, gidx = _first_index_of_max(gl, lane_f)
    g_val = 1.0 / jnp.sum(jnp.exp(gl - gmax), axis=-1, keepdims=True)
    base = MOE_GROUPS + MOE_EPG * gidx
    el = jnp.where((lane_f >= base) & (lane_f < base + MOE_EPG), logits, ninf)
    e1, i1 = _first_index_of_max(el, lane_f)
    e2, i2 = _first_index_of_max(jnp.where(lane_f == i1, ninf, el), lane_f)
    t2 = jnp.exp(e2 - e1)
    w1 = g_val / (1.0 + t2)
    w2 = g_val * t2 / (1.0 + t2)
    a1, a2 = i1 - base, i2 - base
    lo, hi = jnp.minimum(a1, a2), jnp.maximum(a1, a2)
    c_lo = jnp.where(a1 < a2, w1, w2)
    c_hi = jnp.where(a1 < a2, w2, w1)
    pair = lo * (7.0 - lo) * 0.5 + (hi - lo - 1.0)
    cls = gidx * MOE_PAIRS + pair

    onehot = (lane_f == cls)
    oh_b = jnp.where(onehot, 1.0, 0.0).astype(BF16)
    rr = lax.broadcasted_iota(jnp.int32, (tm, tm), 0)
    cc = lax.broadcasted_iota(jnp.int32, (tm, tm), 1)
    strict = jnp.where(rr > cc, 1.0, 0.0).astype(BF16)
    prefix = jnp.dot(strict, oh_b, preferred_element_type=F32) + run_ref[...]
    rank = jnp.sum(jnp.where(onehot, prefix, 0.0), axis=-1, keepdims=True)
    run = run_ref[...] + jnp.sum(oh_b.astype(F32), axis=0, keepdims=True)
    run_ref[...] = run
    cnt_ref[...] = jnp.broadcast_to(run, cnt_ref.shape)

    meta = jnp.where(lane == 0, cls, jnp.where(lane == 1, rank, jnp.where(lane == 2, c_lo,
                     jnp.where(lane == 3, c_hi, 0.0))))
    meta_ref[...] = meta


def _post(att, u, gates, x2d, w_att, w_ssd, w_out, g_ssd, g_ffn, w_r, b_r):
    t = x2d.shape[0]
    tm = TM_POST
    row = lambda w: pl.BlockSpec((tm, w), lambda i: (i, 0))
    const = lambda a: pl.BlockSpec(a.shape, lambda i: (0,) * a.ndim)
    return pl.pallas_call(
        _post_kernel,
        grid=(t // tm,),
        in_specs=[row(ATT_OUT), row(SSD_INNER), row(2 * D_MODEL), row(D_MODEL),
                  const(w_att), const(w_ssd), const(w_out), const(g_ssd), const(g_ffn), const(w_r), const(b_r)],
        out_specs=(row(D_MODEL), row(LANES), pl.BlockSpec((8, LANES), lambda i: (0, 0))),
        out_shape=(jax.ShapeDtypeStruct((t, D_MODEL), F32),
                   jax.ShapeDtypeStruct((t, LANES), F32),
                   jax.ShapeDtypeStruct((8, LANES), F32)),
        scratch_shapes=[pltpu.VMEM((1, LANES), F32)],
        compiler_params=_cparams("arbitrary"),
    )(att, u, gates, x2d, w_att, w_ssd, w_out, g_ssd, g_ffn, w_r, b_r)


def _scatter_kernel(pos_ref, tend_ref, x1_ref, g_ref, xs_ref, h_ref, zero_ref, sem, zsem, *, n_tiles):
    i = pl.program_id(0)
    tm = h_ref.shape[0]

    @pl.when(i == 0)
    def _():
        zero_ref[...] = jnp.zeros(zero_ref.shape, F32)
        n_used = tend_ref[MOE_CLASSES]

        def zero_tile(tile):
            return pltpu.make_async_copy(zero_ref, xs_ref.at[pl.ds(tile * TM_EXP, TM_EXP)], zsem)

        conds = [(tend_ref[k + 1] > tend_ref[k], tend_ref[k + 1] - 1) for k in range(MOE_CLASSES)]
        conds += [(n_used <= tile, tile) for tile in range(n_tiles - MOE_CLASSES, n_tiles)]
        for act in ("start", "wait"):
            for cond, tile in conds:
                @pl.when(cond)
                def _(tile=tile, act=act):
                    getattr(zero_tile(tile), act)()

    h_ref[...] = _rms(x1_ref[...], g_ref[...])

    for r in range(tm):
        pltpu.make_async_copy(h_ref.at[pl.ds(r, 1)], xs_ref.at[pl.ds(pos_ref[i * tm + r], 1)], sem).start()
    pltpu.make_async_copy(h_ref, xs_ref.at[pl.ds(0, tm)], sem).wait()


def _scatter_rows(pos, tile_end0, x1, g_ffn, n_tiles):
    t = x1.shape[0]
    tm = TM_SCAT
    grid_spec = pltpu.PrefetchScalarGridSpec(
        num_scalar_prefetch=2,
        grid=(t // tm,),
        in_specs=[pl.BlockSpec((tm, D_MODEL), lambda i, pos, tend: (i, 0)),
                  pl.BlockSpec((1, D_MODEL), lambda i, pos, tend: (0, 0))],
        out_specs=pl.BlockSpec(memory_space=pl.ANY),
        scratch_shapes=[pltpu.VMEM((tm, D_MODEL), F32), pltpu.VMEM((TM_EXP, D_MODEL), F32),
                        pltpu.SemaphoreType.DMA(()), pltpu.SemaphoreType.DMA(())],
    )
    return pl.pallas_call(
        functools.partial(_scatter_kernel, n_tiles=n_tiles),
        grid_spec=grid_spec,
        out_shape=jax.ShapeDtypeStruct((n_tiles * TM_EXP, D_MODEL), F32),
        compiler_params=_cparams("arbitrary"),
    )(pos, tile_end0, x1, g_ffn)


def _pack_pair(a, b):
    ua = lax.bitcast_convert_type(a.astype(BF16).astype(F32), jnp.uint32)
    ub = lax.bitcast_convert_type(b.astype(BF16).astype(F32), jnp.uint32)
    return (ua & jnp.uint32(0xFFFF0000)) | (ub >> 16)


def _unpack_pair(p):
    a = lax.bitcast_convert_type(p & jnp.uint32(0xFFFF0000), F32)
    b = lax.bitcast_convert_type(p << 16, F32)
    return a, b


def _expert_kernel(nused_ref, elo_ref, ehi_ref, xs_ref, wg_lo, wu_lo, wd_lo, wg_hi, wu_hi, wd_hi, ys_ref):
    del elo_ref, ehi_ref
    i = pl.program_id(0)

    @pl.when(i < nused_ref[0])
    def _():
        xb = xs_ref[...].astype(BF16)
        pre = [(jnp.dot(xb, wg[0], preferred_element_type=F32), jnp.dot(xb, wu[0], preferred_element_type=F32))
               for wg, wu in ((wg_lo, wu_lo), (wg_hi, wu_hi))]
        hid = [(_silu(gt) * up).astype(BF16) for gt, up in pre]
        y_lo, y_hi = (jnp.dot(h, wd[0], preferred_element_type=F32) for h, wd in zip(hid, (wd_lo, wd_hi)))
        ys_ref[...] = _pack_pair(y_lo, y_hi)

    @pl.when(i >= nused_ref[0])
    def _():
        ys_ref[...] = jnp.zeros(ys_ref.shape, jnp.uint32)


def _experts(n_used, tile_elo, tile_ehi, xs, w_gate, w_up, w_down):
    p_rows = xs.shape[0]
    tm = TM_EXP
    n_tiles = p_rows // tm

    def last_used(i, nu):
        return jnp.minimum(i, jnp.maximum(nu[0] - 1, 0))

    def row_map(i, nu, elo, ehi):
        return (last_used(i, nu), 0)

    def wmap(which):
        def f(i, nu, elo, ehi):
            return ((elo, ehi)[which][last_used(i, nu)], 0, 0)
        return f

    wspec_in = lambda which: pl.BlockSpec((1, D_MODEL, MOE_HIDDEN), wmap(which))
    wspec_out = lambda which: pl.BlockSpec((1, MOE_HIDDEN, D_MODEL), wmap(which))
    grid_spec = pltpu.PrefetchScalarGridSpec(
        num_scalar_prefetch=3,
        grid=(n_tiles,),
        in_specs=[pl.BlockSpec((tm, D_MODEL), row_map),
                  wspec_in(0), wspec_in(0), wspec_out(0), wspec_in(1), wspec_in(1), wspec_out(1)],
        out_specs=pl.BlockSpec((tm, D_MODEL), lambda i, nu, elo, ehi: (i, 0)),
    )
    return pl.pallas_call(
        _expert_kernel,
        grid_spec=grid_spec,
        out_shape=jax.ShapeDtypeStruct((p_rows, D_MODEL), jnp.uint32),
        compiler_params=_cparams("arbitrary"),
    )(n_used, tile_elo, tile_ehi, xs, w_gate, w_up, w_down, w_gate, w_up, w_down)


def _final_kernel(pos_ref, x1_ref, meta_ref, p_ref, ys_ref, gple_ref, wpg_ref, wpp_ref, gfin_ref,
                  out_ref, buf_ref, sem, *, n_steps):
    i = pl.program_id(0)
    tm = x1_ref.shape[0]

    def issue_tile(step, slot):
        for r in range(tm):
            pltpu.make_async_copy(ys_ref.at[pl.ds(pos_ref[step * tm + r], 1)],
                                  buf_ref.at[slot, pl.ds(r, 1)], sem.at[slot]).start()

    @pl.when(i == 0)
    def _():
        issue_tile(0, 0)

    slot = i % 2
    for nxt in range(2):
        @pl.when((i + 1 < n_steps) & (1 - slot == nxt))
        def _(nxt=nxt):
            issue_tile(i + 1, nxt)

    pltpu.make_async_copy(ys_ref.at[pl.ds(0, tm)], buf_ref.at[slot], sem.at[slot]).wait()

    nsub = 2
    subs = [{"rows": pl.ds(k * (tm // nsub), tm // nsub)} for k in range(nsub)]

    def embed(s):
        s["pp"] = jnp.dot(p_ref[s["rows"], :].astype(BF16), wpp_ref[...], preferred_element_type=F32)

    def combine(s):
        y_lo, y_hi = _unpack_pair(buf_ref[slot, s["rows"], :])
        meta = meta_ref[s["rows"], :]
        s["x2"] = x1_ref[s["rows"], :] + meta[:, 2:3] * y_lo + meta[:, 3:4] * y_hi
        s["hn"] = _rms(s["x2"], gple_ref[...]).astype(BF16)

    def gate(s):
        s["gate"] = jnp.dot(s.pop("hn"), wpg_ref[...], preferred_element_type=F32)

    def finish(s):
        out_ref[s["rows"], :] = _rms(s.pop("x2") + _sigmoid(s.pop("gate")) * s.pop("pp"), gfin_ref[...])

    for stage in (embed, combine, gate, finish):
        for s in subs:
            stage(s)


def _final(pos, x1, meta, p2d, ys, g_ple, w_pg, w_pp, g_fin):
    t = x1.shape[0]
    tm = TM_FIN
    n_steps = t // tm
    row = lambda w: pl.BlockSpec((tm, w), lambda i, pos: (i, 0))
    const = lambda a: pl.BlockSpec(a.shape, lambda i, pos: (0,) * a.ndim)
    grid_spec = pltpu.PrefetchScalarGridSpec(
        num_scalar_prefetch=1,
        grid=(n_steps,),
        in_specs=[row(D_MODEL), row(LANES), row(PLE_DIM), pl.BlockSpec(memory_space=pl.ANY),
                  const(g_ple), const(w_pg), const(w_pp), const(g_fin)],
        out_specs=row(D_MODEL),
        scratch_shapes=[pltpu.VMEM((2, tm, D_MODEL), jnp.uint32), pltpu.SemaphoreType.DMA((2,))],
    )
    return pl.pallas_call(
        functools.partial(_final_kernel, n_steps=n_steps),
        grid_spec=grid_spec,
        out_shape=jax.ShapeDtypeStruct((t, D_MODEL), F32),
        compiler_params=_cparams("arbitrary"),
    )(pos, x1, meta, p2d, ys, g_ple, w_pg, w_pp, g_fin)


_PAIR_LO = np.array([0, 0, 0, 1, 1, 2], np.int32)
_PAIR_HI = np.array([1, 2, 3, 2, 3, 3], np.int32)


def _routing_tables(meta, counts_f, n_tiles):
    cls = meta[:, 0].astype(jnp.int32)
    rank = meta[:, 1].astype(jnp.int32)
    counts = counts_f[0, :MOE_CLASSES].astype(jnp.int32)
    tiles_per = (counts + TM_EXP - 1) // TM_EXP
    tile_end = jnp.cumsum(tiles_per)
    tile_start = tile_end - tiles_per
    class_ids = jnp.arange(MOE_CLASSES, dtype=jnp.int32)
    pos = jnp.sum(jnp.where(cls[:, None] == class_ids[None, :], (tile_start * TM_EXP)[None, :], 0), axis=1) + rank
    n_used = tile_end[-1:]
    tile_ids = jnp.arange(n_tiles, dtype=jnp.int32)
    tile_cls = jnp.minimum(jnp.sum((tile_end[None, :] <= tile_ids[:, None]).astype(jnp.int32), axis=1),
                           MOE_CLASSES - 1)
    grp = tile_cls // MOE_PAIRS
    pair = tile_cls % MOE_PAIRS
    tile_elo = grp * MOE_EPG + jnp.asarray(_PAIR_LO)[pair]
    tile_ehi = grp * MOE_EPG + jnp.asarray(_PAIR_HI)[pair]
    tile_end0 = jnp.concatenate([jnp.zeros((1,), jnp.int32), tile_end.astype(jnp.int32)])
    return (pos.astype(jnp.int32), n_used.astype(jnp.int32), tile_elo.astype(jnp.int32),
            tile_ehi.astype(jnp.int32), tile_end0)


def kernel(x, p, norm_mix_g, w_in, conv_w, conv_b, dt_bias, a_log, d_skip, ssd_norm_g, w_att_branch,
           w_ssd_branch, w_out, norm_ffn_g, w_router_group, b_router_group, w_router_expert,
           b_router_expert, w_exp_gate, w_exp_up, w_exp_down, norm_ple_g, w_ple_gate, w_ple_proj,
           final_norm_g):
    bsz, seq, _ = x.shape
    t = bsz * seq
    assert w_in.shape[0] == 1, "single-layer block"
    assert seq // ATT_PATTERNS[-1][1] == ATT_BLOCK and seq % TM_IN == 0
    x2d = x.reshape(t, D_MODEL)

    wi = w_in[0]
    c_dt = QKV_W + SSD_INNER + SSD_CONV_CH
    qkv_cols = [wi[:, sel * ATT_WIDTH + g * ATT_OUT: sel * ATT_WIDTH + (g + 1) * ATT_OUT]
                for g in range(ATT_GROUPS) for sel in range(3)]
    w_main = jnp.concatenate(qkv_cols + [wi[:, QKV_W:c_dt], wi[:, c_dt + SSD_HEADS:]], axis=1).astype(BF16)
    w_dt = jnp.pad(wi[:, c_dt:c_dt + SSD_HEADS], ((0, 0), (0, LANES - SSD_HEADS))).astype(BF16)
    row = lambda v: v.reshape(1, -1).astype(F32)

    qkv1, qkv2, qkv3, z, xbc, gates, dt_raw = _in_proj(x2d, row(norm_mix_g[0]), w_main, w_dt, bsz, seq)

    att = _attention(qkv1.reshape(bsz, seq, ATT_WIDTH), qkv2, qkv3, bsz, seq).reshape(t, ATT_OUT)

    dt_t = dt_raw[:, :SSD_HEADS].reshape(bsz, seq, SSD_GROUPS, SSD_HPG).transpose(0, 2, 3, 1)
    u = _ssd(xbc.reshape(bsz, seq, SSD_CONV_CH), z.reshape(bsz, seq, SSD_INNER), dt_t,
             conv_w[0], conv_b[0], dt_bias[0], a_log[0], d_skip[0], bsz, seq).reshape(t, SSD_INNER)

    w_r32 = jnp.pad(jnp.concatenate([w_router_group[0], w_router_expert[0]], axis=1),
                    ((0, 0), (0, LANES - MOE_GROUPS - MOE_EXPERTS))).astype(F32)
    w_r_hi = w_r32.astype(BF16)
    w_r_lo = (w_r32 - w_r_hi.astype(F32)).astype(BF16)
    w_r = jnp.concatenate([w_r_hi, w_r_hi, w_r_lo], axis=0)
    b_r = jnp.pad(jnp.concatenate([b_router_group[0], b_router_expert[0]]),
                  (0, LANES - MOE_GROUPS - MOE_EXPERTS)).reshape(1, LANES).astype(F32)
    x1, meta, counts = _post(att, u, gates, x2d,
                             w_att_branch[0].astype(BF16), w_ssd_branch[0].astype(BF16), w_out[0].astype(BF16),
                             row(ssd_norm_g[0]), row(norm_ffn_g[0]), w_r, b_r)

    n_tiles = t // TM_EXP + MOE_CLASSES
    pos, n_used, tile_elo, tile_ehi, tile_end0 = _routing_tables(meta, counts, n_tiles)
    xs = _scatter_rows(pos, tile_end0, x1, row(norm_ffn_g[0]), n_tiles)
    ys = _experts(n_used, tile_elo, tile_ehi, xs,
                  w_exp_gate[0].astype(BF16), w_exp_up[0].astype(BF16), w_exp_down[0].astype(BF16))
    out = _final(pos, x1, meta, p[0].reshape(t, PLE_DIM), ys, row(norm_ple_g[0]),
                 w_ple_gate[0].astype(BF16), w_ple_proj[0].astype(BF16), row(final_norm_g))
    return out.reshape(bsz, seq, D_MODEL)
```

```python
import functools

import numpy as np
import jax
import jax.numpy as jnp
from jax import lax
from jax.experimental import pallas as pl
from jax.experimental.pallas import tpu as pltpu

F32 = jnp.float32
BF16 = jnp.bfloat16

D_MODEL = 1024
PLE_DIM = 256
RMS_EPS = 1e-6

ATT_PATTERNS = ((128, 1), (512, 4), (2048, 16))
ATT_GROUPS = 3
ATT_HPG = 8
ATT_HEAD_DIM = 64
ATT_WIDTH = ATT_GROUPS * ATT_HPG * ATT_HEAD_DIM
ATT_OUT = ATT_HPG * ATT_HEAD_DIM
ATT_BLOCK = 128
ATT_LOCKSTEP = 2
ALIBI_MAX_BIAS = 8.0
QKV_W = 3 * ATT_WIDTH

SSD_INNER = 2048
SSD_HEADS = 32
SSD_GROUPS = 8
SSD_HPG = 4
SSD_HEAD_DIM = 64
SSD_STATE = 128
SSD_CONV = 4
SSD_CHUNK = 128
SSD_CONV_CH = SSD_INNER + 2 * SSD_GROUPS * SSD_STATE
SSD_GW = SSD_HPG * SSD_HEAD_DIM
SSD_LOCKSTEP = 4

MOE_GROUPS = 4
MOE_EPG = 4
MOE_EXPERTS = 16
MOE_HIDDEN = 512
MOE_PAIRS = 6
MOE_CLASSES = MOE_GROUPS * MOE_PAIRS

LANES = 128
NEG_BIG = -1e30
VMEM_LIMIT = 56 * 1024 * 1024

TM_IN = 1024
TN_IN = 512
TN_WIDE = 1024
TM_POST = 512
TM_SCAT = 512
TM_EXP = 256
TM_FIN = 512


def _cparams(*sem):
    return pltpu.CompilerParams(dimension_semantics=sem, vmem_limit_bytes=VMEM_LIMIT)


def _sigmoid(x):
    return 0.5 * jnp.tanh(0.5 * x) + 0.5


def _silu(x):
    h = 0.5 * x
    return h + h * jnp.tanh(h)


def _rms(x, g):
    ms = jnp.mean(x * x, axis=-1, keepdims=True)
    return x * lax.rsqrt(ms + RMS_EPS) * g


_IN_SEG = (3, 3, 3, SSD_INNER // TN_WIDE, SSD_CONV_CH // TN_WIDE, 2 * D_MODEL // TN_WIDE)
_IN_START = tuple(int(v) for v in np.cumsum((0,) + _IN_SEG))
_IN_NARROW = _IN_START[3]


def _inproj_kernel(x_ref, g_ref, wq_ref, ww_ref, wdt_ref, cw_ref, cb_ref, qkv1_ref, qkv2_ref, qkv3_ref, z_ref,
                   xbc_ref, gates_ref, dt_ref, h_ref, hcol_ref, halo_ref, *, per_seq):
    j = pl.program_id(1)
    tm = x_ref.shape[0]

    @pl.when(j == 0)
    def _():
        h = _rms(x_ref[...], g_ref[...])
        hb = h.astype(BF16)
        h_ref[0] = hb
        dt_ref[...] = jnp.dot(hb, wdt_ref[...], preferred_element_type=F32)
        ncb = hcol_ref.shape[0]
        for c in range(ncb):
            hcol_ref[c] = h[:, c * LANES:(c + 1) * LANES]
        for slot, (_, dil) in enumerate(ATT_PATTERNS[1:], start=1):
            rows = tm // dil
            for r in range(dil):
                for c in range(ncb):
                    h_ref[slot, r * rows:(r + 1) * rows, c * LANES:(c + 1) * LANES] = (
                        hcol_ref[c, pl.ds(r, rows, stride=dil), :].astype(BF16))

    def segment(k, fn):
        @pl.when((j >= _IN_START[k]) & (j < _IN_START[k + 1]))
        def _():
            fn()

    def narrow(slot, store):
        def fn():
            store(jnp.dot(h_ref[slot], wq_ref[...], preferred_element_type=F32).astype(BF16))
        return fn

    def wide(ref):
        def fn():
            ref[...] = jnp.dot(h_ref[0], ww_ref[...], preferred_element_type=F32).astype(BF16)
        return fn

    def store_plain(ref):
        def store(res):
            ref[...] = res
        return store

    def store_grouped(ref, dil):
        def store(res):
            ref[0] = res.reshape(dil, tm // dil, res.shape[1])
        return store

    def conv_silu():
        jb = j - _IN_START[4]
        nsub = 4
        rs = tm // nsub
        res = [jnp.dot(h_ref[0, k * rs:(k + 1) * rs, :], ww_ref[...], preferred_element_type=F32)
               for k in range(nsub)]
        seq_start = pl.program_id(0) % per_seq == 0
        prev = jnp.where(seq_start, 0.0, halo_ref[jb])
        row8 = lax.broadcasted_iota(jnp.int32, (8, 1), 0)
        cw = cw_ref[...]
        for k in range(nsub):
            acc = cb_ref[...] + cw[SSD_CONV - 1:SSD_CONV, :] * res[k]
            for sh in range(1, SSD_CONV):
                rolled = pltpu.roll(res[k], sh, 0)
                top = jnp.where(row8 < sh, pltpu.roll(prev, sh, 0), rolled[0:8])
                shifted = jnp.concatenate([top, rolled[8:]], axis=0)
                acc = acc + cw[SSD_CONV - 1 - sh:SSD_CONV - sh, :] * shifted
            xbc_ref[k * rs:(k + 1) * rs, :] = acc.astype(BF16)
            prev = res[k][rs - 8:rs]
        halo_ref[jb] = prev

    segment(0, narrow(0, store_plain(qkv1_ref)))
    segment(1, narrow(1, store_grouped(qkv2_ref, ATT_PATTERNS[1][1])))
    segment(2, narrow(2, store_grouped(qkv3_ref, ATT_PATTERNS[2][1])))
    segment(3, wide(z_ref))
    segment(4, conv_silu)
    segment(5, wide(gates_ref))


def _in_proj(x2d, g, w_qkv, w_wide, w_dt, conv_w, conv_b, bsz, seq):
    t = x2d.shape[0]
    nj = _IN_START[-1]
    per_seq = seq // TM_IN
    d2, d3 = ATT_PATTERNS[1][1], ATT_PATTERNS[2][1]

    def seg_map(k):
        return lambda i, j: (i, jnp.clip(j - _IN_START[k], 0, _IN_SEG[k] - 1))

    def grp_map(k):
        return lambda i, j: (i // per_seq, 0, i % per_seq, jnp.clip(j - _IN_START[k], 0, _IN_SEG[k] - 1))

    def wq_map(i, j):
        jj = jnp.minimum(j, _IN_NARROW - 1)
        return (0, (jj % 3) * ATT_GROUPS + jj // 3)

    out_shapes = (
        jax.ShapeDtypeStruct((t, ATT_WIDTH), BF16),
        jax.ShapeDtypeStruct((bsz, d2, seq // d2, ATT_WIDTH), BF16),
        jax.ShapeDtypeStruct((bsz, d3, seq // d3, ATT_WIDTH), BF16),
        jax.ShapeDtypeStruct((t, SSD_INNER), BF16),
        jax.ShapeDtypeStruct((t, SSD_CONV_CH), BF16),
        jax.ShapeDtypeStruct((t, 2 * D_MODEL), BF16),
        jax.ShapeDtypeStruct((t, LANES), F32),
    )
    out_specs = (
        pl.BlockSpec((TM_IN, TN_IN), seg_map(0)),
        pl.BlockSpec((1, d2, TM_IN // d2, TN_IN), grp_map(1)),
        pl.BlockSpec((1, d3, TM_IN // d3, TN_IN), grp_map(2)),
        pl.BlockSpec((TM_IN, TN_WIDE), seg_map(3)),
        pl.BlockSpec((TM_IN, TN_WIDE), seg_map(4)),
        pl.BlockSpec((TM_IN, TN_WIDE), seg_map(5)),
        pl.BlockSpec((TM_IN, LANES), lambda i, j: (i, 0)),
    )
    conv_map = lambda i, j: (0, jnp.clip(j - _IN_START[4], 0, _IN_SEG[4] - 1))
    return pl.pallas_call(
        functools.partial(_inproj_kernel, per_seq=per_seq),
        grid=(t // TM_IN, nj),
        in_specs=[
            pl.BlockSpec((TM_IN, D_MODEL), lambda i, j: (i, 0)),
            pl.BlockSpec((1, D_MODEL), lambda i, j: (0, 0)),
            pl.BlockSpec((D_MODEL, TN_IN), wq_map),
            pl.BlockSpec((D_MODEL, TN_WIDE), lambda i, j: (0, jnp.maximum(j - _IN_NARROW, 0))),
            pl.BlockSpec((D_MODEL, LANES), lambda i, j: (0, 0)),
            pl.BlockSpec((SSD_CONV, TN_WIDE), conv_map),
            pl.BlockSpec((1, TN_WIDE), conv_map),
        ],
        out_specs=out_specs,
        out_shape=out_shapes,
        scratch_shapes=[pltpu.VMEM((3, TM_IN, D_MODEL), BF16),
                        pltpu.VMEM((D_MODEL // LANES, TM_IN, LANES), F32),
                        pltpu.VMEM((_IN_SEG[4], 8, TN_WIDE), F32)],
        compiler_params=_cparams("arbitrary", "arbitrary"),
    )(x2d, g, w_qkv, w_wide, w_dt, conv_w.astype(F32), conv_b.reshape(1, SSD_CONV_CH).astype(F32))


def _att_bias_tables():
    h = np.arange(1, ATT_GROUPS * ATT_HPG + 1, dtype=np.float32)
    slopes = np.exp2(-ALIBI_MAX_BIAS * h / (ATT_GROUPS * ATT_HPG)).astype(np.float32).reshape(ATT_GROUPS, ATT_HPG)
    qi = np.arange(ATT_BLOCK)[:, None] + ATT_BLOCK
    kj = np.arange(2 * ATT_BLOCK)[None, :]
    delta = qi - kj
    tabs = []
    for g, (window, dil) in enumerate(ATT_PATTERNS):
        span = window // dil
        band = (delta >= 0) & (delta <= span)
        bias = (-slopes[g][:, None, None] * (delta * dil).astype(np.float32)[None]).astype(np.float32)
        with_prev = np.where(band[None], bias, np.float32(NEG_BIG))
        first = np.where((band & (kj >= ATT_BLOCK))[None], bias, np.float32(NEG_BIG))
        tabs.append(np.stack([first, with_prev]).astype(np.float32))
    return tabs[0], tabs[1], tabs[2][1][:, :, ATT_BLOCK:]


def _att_units(units):
    lane = lax.broadcasted_iota(jnp.int32, (1, LANES), 1)
    low = lane < ATT_HEAD_DIM
    scale = ATT_HEAD_DIM ** -0.5
    qmask = (jnp.where(low, scale, 0.0).astype(BF16), jnp.where(low, 0.0, scale).astype(BF16))
    heads = [(q, k2, v2, tab_fn, hh) for q, k2, v2, tab_fn in units for hh in range(2)]
    scores = [lax.dot_general(q * qmask[hh], k2, (((1,), (1,)), ((), ())), preferred_element_type=F32)
              for q, k2, _, _, hh in heads]
    probs = []
    for s, (_, _, _, tab_fn, hh) in zip(scores, heads):
        s = s + tab_fn(hh)
        m = jnp.max(s, axis=-1, keepdims=True)
        e = jnp.exp(s - m)
        probs.append((e.astype(BF16), m, jnp.sum(e, axis=-1, keepdims=True)))
    pvs = [jnp.dot(e, v2, preferred_element_type=F32) for (e, _, _), (_, _, v2, _, _) in zip(probs, heads)]
    outs = [pv / den for pv, (_, _, den) in zip(pvs, probs)]
    lses = [m + jnp.log(den) for _, m, den in probs]
    return [(jnp.where(low, outs[2 * i], outs[2 * i + 1]), jnp.where(low, lses[2 * i], lses[2 * i + 1]))
            for i in range(len(units))]


def _att_kernel(q1, k1, v1, q2, k2, v2, q3, k3, v3, t1, t2, t3, out_ref,
                o1, l1, o2p, l2p, o2n, l2n, o3p, l3p, o3n, l3n, *, seq):
    blk = ATT_BLOCK
    d2, d3 = ATT_PATTERNS[1][1], ATT_PATTERNS[2][1]
    nb1, nb2 = seq // blk, seq // d2 // blk

    def rows(n):
        return pl.ds(pl.multiple_of(n * blk, blk), blk)

    assert nb1 == d2 * nb2 == d3

    def unit_body(i, _):
        units, dests = [], []
        for k in range(ATT_LOCKSTEP):
            u = i * ATT_LOCKSTEP + k
            cur, prv = rows(u), rows(jnp.maximum(u - 1, 0))
            sel = jnp.minimum(u, 1)
            units.append((q1[0, cur, :], jnp.concatenate([k1[0, prv, :], k1[0, cur, :]], axis=0),
                          jnp.concatenate([v1[0, prv, :], v1[0, cur, :]], axis=0),
                          lambda hh, sel=sel: t1[sel, hh]))
            r, n = u // nb2, u % nb2
            cur2, prv2 = rows(n), rows(jnp.maximum(n - 1, 0))
            sel2 = jnp.minimum(n, 1)
            units.append((q2[0, r, cur2, :], jnp.concatenate([k2[0, r, prv2, :], k2[0, r, cur2, :]], axis=0),
                          jnp.concatenate([v2[0, r, prv2, :], v2[0, r, cur2, :]], axis=0),
                          lambda hh, sel2=sel2: t2[sel2, hh]))
            units.append((q3[0, u], k3[0, u], v3[0, u], lambda hh: t3[hh]))
            dests += [(o1, l1, cur), (o2p, l2p, cur), (o3p, l3p, cur)]
        for (o, l), (o_ref, l_ref, where) in zip(_att_units(units), dests):
            o_ref[where, :] = o
            l_ref[where, :] = l
        return 0

    lax.fori_loop(0, nb1 // ATT_LOCKSTEP, unit_body, 0)

    for dil, pairs in ((d2, ((o2p, o2n), (l2p, l2n))), (d3, ((o3p, o3n), (l3p, l3n)))):
        n_sub = seq // dil
        for r in range(dil):
            for src, dst in pairs:
                dst[pl.ds(r, n_sub, stride=dil), :] = src[r * n_sub:(r + 1) * n_sub, :]

    mrows = 2 * blk

    def merge(c, _):
        rr = pl.ds(pl.multiple_of(c * mrows, mrows), mrows)
        la, lb, lc = l1[rr, :], l2n[rr, :], l3n[rr, :]
        lm = jnp.maximum(jnp.maximum(la, lb), lc)
        ea, eb, ec = jnp.exp(la - lm), jnp.exp(lb - lm), jnp.exp(lc - lm)
        att = (ea * o1[rr, :] + eb * o2n[rr, :] + ec * o3n[rr, :]) / (ea + eb + ec)
        out_ref[0, rr, :] = att.astype(BF16)
        return 0

    lax.fori_loop(0, seq // mrows, merge, 0)


def _attention(qkv1, qkv2, qkv3, bsz, seq):
    t1, t2, t3 = (jnp.asarray(t) for t in _att_bias_tables())
    d2, d3 = ATT_PATTERNS[1][1], ATT_PATTERNS[2][1]
    npair = ATT_HPG // 2
    in_arrays, in_specs = [], []
    for arr, lead in ((qkv1, ()), (qkv2, (d2,)), (qkv3, (d3,))):
        n_rows = arr.shape[-2]
        for sel in range(3):
            zeros = (0,) * len(lead)
            in_arrays.append(arr)
            in_specs.append(pl.BlockSpec((1,) + lead + (n_rows, LANES),
                                         lambda hp, b, sel=sel, zeros=zeros: (b,) + zeros + (0, sel * npair + hp)))
    in_arrays += [t1, t2, t3]
    in_specs += [pl.BlockSpec((2, 2, ATT_BLOCK, 2 * ATT_BLOCK), lambda hp, b: (0, hp, 0, 0)),
                 pl.BlockSpec((2, 2, ATT_BLOCK, 2 * ATT_BLOCK), lambda hp, b: (0, hp, 0, 0)),
                 pl.BlockSpec((2, ATT_BLOCK, ATT_BLOCK), lambda hp, b: (hp, 0, 0))]
    return pl.pallas_call(
        functools.partial(_att_kernel, seq=seq),
        grid=(npair, bsz),
        in_specs=in_specs,
        out_specs=pl.BlockSpec((1, seq, LANES), lambda hp, b: (b, 0, hp)),
        out_shape=jax.ShapeDtypeStruct((bsz, seq, ATT_OUT), BF16),
        scratch_shapes=[pltpu.VMEM((seq, LANES), F32)] * 10,
        compiler_params=_cparams("arbitrary", "arbitrary"),
    )(*in_arrays)


def _softplus(x):
    return jnp.maximum(x, 0.0) + jnp.log1p(jnp.exp(-jnp.abs(x)))


def _head_spread_table():
    gw, n_src = SSD_GW, LANES
    tab = np.zeros((2 * n_src, 2 * gw), np.float32)
    for half in range(2):
        for blk, src0 in enumerate((4, 8)):
            for col in range(gw):
                tab[half * n_src + src0 + col // SSD_HEAD_DIM, blk * gw + col] = 1.0
    return tab


def _ssd_kernel(x_ref, b_ref, c_ref, z_ref, dt_ref, spread_ref, dtb_ref, alog_ref, dsk_ref, u_ref, carry_ref,
                *, n_chunks):
    L = SSD_CHUNK
    gw = SSD_GW
    carry_ref[...] = jnp.zeros(carry_ref.shape, F32)

    a_neg = -jnp.exp(alog_ref[0])
    dtb = dtb_ref[0]
    dsk = dsk_ref[0]
    ri = lax.broadcasted_iota(jnp.int32, (L, L), 0)
    ci = lax.broadcasted_iota(jnp.int32, (L, L), 1)
    upper_incl = (ri <= ci).astype(F32)
    causal = ri >= ci
    lane = lax.broadcasted_iota(jnp.int32, (1, LANES), 1)
    low = lane < SSD_HEAD_DIM
    heads = range(SSD_HPG)

    def load(s):
        rows = pl.ds(s["r0"], L)
        s["xs"] = _silu(x_ref[0, rows, :].astype(F32))
        s["xs_b"] = s["xs"].astype(BF16)
        s["bm"] = _silu(b_ref[0, rows, :].astype(F32))
        s["cm_b"] = _silu(c_ref[0, rows, :].astype(F32)).astype(BF16)

    def decay_cumsum(s):
        dt = _softplus(dt_ref[0, 0, :, pl.ds(s["r0"], L)] + dtb)
        rows8 = jnp.concatenate([dt * a_neg, dt], axis=0)
        cs8 = jnp.dot(rows8, upper_incl, preferred_element_type=F32, precision=lax.Precision.HIGHEST)
        s["dt"], s["acs_t"] = dt, cs8[0:SSD_HPG]

    def decay_spread(s):
        dt, acs_t = s["dt"], s["acs_t"]
        t16 = jnp.concatenate([acs_t, acs_t, acs_t, dt], axis=0)
        cols = jnp.concatenate([t16, jnp.zeros((L - 16, L), F32)], axis=0).T
        dt_at8 = pltpu.roll(cols, LANES - 4, 1)
        comb = jnp.where(lane < 8, jnp.exp(cols), jnp.exp(cols[L - 1:L, :] - cols) * dt_at8)
        hi = comb.astype(BF16)
        lo = (comb - hi.astype(F32)).astype(BF16)
        s["spread"] = jnp.dot(jnp.concatenate([hi, lo], axis=1), spread_ref[...], preferred_element_type=F32)
        s["acs_cols"] = cols

    def scores(s):
        s["bm_t"] = s.pop("bm").T.astype(BF16)
        s["cb"] = jnp.dot(s["cm_b"], s["bm_t"], preferred_element_type=F32)

    def intra(s):
        spread, acs_t, dt, cb, cols = s["spread"], s.pop("acs_t"), s.pop("dt"), s.pop("cb"), s.pop("acs_cols")
        xs, xs_b = s["xs"], s.pop("xs_b")
        mixes = []
        for j in heads:
            seg = cols[:, j:j + 1] - acs_t[j:j + 1, :]
            mixes.append((jnp.exp(jnp.where(causal, seg, -jnp.inf)) * (cb * dt[j:j + 1, :])).astype(BF16))
        halves = []
        for hp in range(SSD_HPG // 2):
            yy = jnp.dot(jnp.concatenate(mixes[2 * hp:2 * hp + 2], axis=0), xs_b[:, hp * LANES:(hp + 1) * LANES],
                         preferred_element_type=F32)
            halves.append(jnp.where(low, yy[:L], yy[L:]))
        s["y"] = jnp.concatenate(halves, axis=1)
        s["st_new"] = jnp.dot(s.pop("bm_t"), (xs * spread[:, gw:]).astype(BF16),
                              preferred_element_type=F32)

    def inter(s):
        eacs_bc = s.pop("spread")[:, :gw]
        carry = carry_ref[...]
        y = s.pop("y") + jnp.dot(s.pop("cm_b"), carry.astype(BF16), preferred_element_type=F32) * eacs_bc
        carry_ref[...] = carry * eacs_bc[L - 1:L, :] + s.pop("st_new")
        y = y + s.pop("xs") * dsk
        rows = pl.ds(s["r0"], L)
        u_ref[0, rows, :] = (y * _silu(z_ref[0, rows, :].astype(F32))).astype(BF16)

    def body(i, _):
        states = [{"c": i * SSD_LOCKSTEP + k, "r0": pl.multiple_of((i * SSD_LOCKSTEP + k) * L, L)}
                  for k in range(SSD_LOCKSTEP)]
        for stage in (load, decay_cumsum, decay_spread, scores, intra, inter):
            for s in states:
                stage(s)
        return 0

    lax.fori_loop(0, n_chunks // SSD_LOCKSTEP, body, 0)


def _ssd(xbc, z, dt_t, dt_bias, a_log, d_skip, bsz, seq):
    gw = SSD_GW
    nxb = SSD_INNER // SSD_STATE
    dsk = jnp.repeat(d_skip.astype(F32), SSD_HEAD_DIM).reshape(SSD_GROUPS, 1, gw)
    dtb = dt_bias.astype(F32).reshape(SSD_GROUPS, SSD_HPG, 1)
    alog = a_log.astype(F32).reshape(SSD_GROUPS, SSD_HPG, 1)
    spread = jnp.asarray(_head_spread_table(), BF16)
    x_map = lambda b, g: (b, 0, g)
    bm_map = lambda b, g: (b, 0, nxb + g)
    cm_map = lambda b, g: (b, 0, nxb + SSD_GROUPS + g)
    return pl.pallas_call(
        functools.partial(_ssd_kernel, n_chunks=seq // SSD_CHUNK),
        grid=(bsz, SSD_GROUPS),
        in_specs=[
            pl.BlockSpec((1, seq, gw), x_map),
            pl.BlockSpec((1, seq, SSD_STATE), bm_map),
            pl.BlockSpec((1, seq, SSD_STATE), cm_map),
            pl.BlockSpec((1, seq, gw), x_map),
            pl.BlockSpec((1, 1, SSD_HPG, seq), lambda b, g: (b, g, 0, 0)),
            pl.BlockSpec(spread.shape, lambda b, g: (0, 0)),
            pl.BlockSpec((1, SSD_HPG, 1), lambda b, g: (g, 0, 0)),
            pl.BlockSpec((1, SSD_HPG, 1), lambda b, g: (g, 0, 0)),
            pl.BlockSpec((1, 1, gw), lambda b, g: (g, 0, 0)),
        ],
        out_specs=pl.BlockSpec((1, seq, gw), x_map),
        out_shape=jax.ShapeDtypeStruct((bsz, seq, SSD_INNER), BF16),
        scratch_shapes=[pltpu.VMEM((SSD_STATE, gw), F32)],
        compiler_params=_cparams("arbitrary", "arbitrary"),
    )(xbc, xbc, xbc, z, dt_t, spread, dtb, alog, dsk)


def _first_index_of_max(vals, lane_f):
    m = jnp.max(vals, axis=-1, keepdims=True)
    idx = jnp.min(jnp.where(vals == m, lane_f, float(LANES)), axis=-1, keepdims=True)
    return m, idx


def _post_kernel(att_ref, u_ref, gates_ref, x_ref, watt_ref, wssd_ref, wout_ref,
                 gssd_ref, gffn_ref, wr_ref, br_ref, x1_ref, meta_ref, cnt_ref, run_ref):
    i = pl.program_id(0)

    @pl.when(i == 0)
    def _():
        run_ref[...] = jnp.zeros(run_ref.shape, F32)

    tm = x_ref.shape[0]
    nsub = 2
    subs = [{"rows": pl.ds(k * (tm // nsub), tm // nsub)} for k in range(nsub)]

    def att_branch(s):
        s["y_att"] = jnp.dot(att_ref[s["rows"], :], watt_ref[...], preferred_element_type=F32)

    def ssd_branch(s):
        ssd = _rms(u_ref[s["rows"], :].astype(F32), gssd_ref[...])
        s["y_ssd"] = jnp.dot(ssd.astype(BF16), wssd_ref[...], preferred_element_type=F32)

    def mix_out(s):
        gates = gates_ref[s["rows"], :].astype(F32)
        merged = _sigmoid(gates[:, :D_MODEL]) * s.pop("y_att") + _sigmoid(gates[:, D_MODEL:]) * s.pop("y_ssd")
        x1 = x_ref[s["rows"], :] + jnp.dot(merged.astype(BF16), wout_ref[...], preferred_element_type=F32)
        x1_ref[s["rows"], :] = x1
        s["h2"] = _rms(x1, gffn_ref[...])

    def router(s):
        h2 = s.pop("h2")
        hi = h2.astype(BF16)
        lo = (h2 - hi.astype(F32)).astype(BF16)
        s["logits"] = jnp.dot(jnp.concatenate([hi, lo, hi], axis=1), wr_ref[...], preferred_element_type=F32)

    for stage in (att_branch, ssd_branch, mix_out, router):
        for s in subs:
            stage(s)
    logits = jnp.concatenate([s["logits"] for s in subs], axis=0) + br_ref[...]
    lane = lax.broadcasted_iota(jnp.int32, (tm, LANES), 1)
    lane_f = lane.astype(F32)
    ninf = -jnp.inf
    gl = jnp.where(lane < MOE_GROUPS, logits, ninf)
    gmax, gidx = _first_index_of_max(gl, lane_f)
    g_val = 1.0 / jnp.sum(jnp.exp(gl - gmax), axis=-1, keepdims=True)
    base = MOE_GROUPS + MOE_EPG * gidx
    el = jnp.where((lane_f >= base) & (lane_f < base + MOE_EPG), logits, ninf)
    e1, i1 = _first_index_of_max(el, lane_f)
    e2, i2 = _first_index_of_max(jnp.where(lane_f == i1, ninf, el), lane_f)
    t2 = jnp.exp(e2 - e1)
    w1 = g_val / (1.0 + t2)
    w2 = g_val * t2 / (1.0 + t2)
    a1, a2 = i1 - base, i2 - base
    lo, hi = jnp.minimum(a1, a2), jnp.maximum(a1, a2)
    c_lo = jnp.where(a1 < a2, w1, w2)
    c_hi = jnp.where(a1 < a2, w2, w1)
    pair = lo * (7.0 - lo) * 0.5 + (hi - lo - 1.0)
    cls = gidx * MOE_PAIRS + pair

    onehot = (lane_f == cls)
    oh_b = jnp.where(onehot, 1.0, 0.0).astype(BF16)
    rr = lax.broadcasted_iota(jnp.int32, (tm, tm), 0)
    cc = lax.broadcasted_iota(jnp.int32, (tm, tm), 1)
    strict = jnp.where(rr > cc, 1.0, 0.0).astype(BF16)
    prefix = jnp.dot(strict, oh_b, preferred_element_type=F32) + run_ref[...]
    rank = jnp.sum(jnp.where(onehot, prefix, 0.0), axis=-1, keepdims=True)
    run = run_ref[...] + jnp.sum(oh_b.astype(F32), axis=0, keepdims=True)
    run_ref[...] = run
    cnt_ref[...] = jnp.broadcast_to(run, cnt_ref.shape)

    meta = jnp.where(lane == 0, cls, jnp.where(lane == 1, rank, jnp.where(lane == 2, c_lo,
                     jnp.where(lane == 3, c_hi, 0.0))))
    meta_ref[...] = meta


def _post(att, u, gates, x2d, w_att, w_ssd, w_out, g_ssd, g_ffn, w_r, b_r):
    t = x2d.shape[0]
    tm = TM_POST
    row = lambda w: pl.BlockSpec((tm, w), lambda i: (i, 0))
    const = lambda a: pl.BlockSpec(a.shape, lambda i: (0,) * a.ndim)
    return pl.pallas_call(
        _post_kernel,
        grid=(t // tm,),
        in_specs=[row(ATT_OUT), row(SSD_INNER), row(2 * D_MODEL), row(D_MODEL),
                  const(w_att), const(w_ssd), const(w_out), const(g_ssd), const(g_ffn), const(w_r), const(b_r)],
        out_specs=(row(D_MODEL), row(LANES), pl.BlockSpec((8, LANES), lambda i: (0, 0))),
        out_shape=(jax.ShapeDtypeStruct((t, D_MODEL), F32),
                   jax.ShapeDtypeStruct((t, LANES), F32),
                   jax.ShapeDtypeStruct((8, LANES), F32)),
        scratch_shapes=[pltpu.VMEM((1, LANES), F32)],
        compiler_params=_cparams("arbitrary"),
    )(att, u, gates, x2d, w_att, w_ssd, w_out, g_ssd, g_ffn, w_r, b_r)


def _scatter_kernel(pos_ref, tend_ref, x1_ref, g_ref, xs_ref, h_ref, zero_ref, sem, zsem, *, n_tiles):
    i = pl.program_id(0)
    tm = h_ref.shape[0]

    @pl.when(i == 0)
    def _():
        zero_ref[...] = jnp.zeros(zero_ref.shape, F32)
        n_used = tend_ref[MOE_CLASSES]

        def zero_tile(tile):
            return pltpu.make_async_copy(zero_ref, xs_ref.at[pl.ds(tile * TM_EXP, TM_EXP)], zsem)

        conds = [(tend_ref[k + 1] > tend_ref[k], tend_ref[k + 1] - 1) for k in range(MOE_CLASSES)]
        conds += [(n_used <= tile, tile) for tile in range(n_tiles - MOE_CLASSES, n_tiles)]
        for act in ("start", "wait"):
            for cond, tile in conds:
                @pl.when(cond)
                def _(tile=tile, act=act):
                    getattr(zero_tile(tile), act)()

    h_ref[...] = _rms(x1_ref[...], g_ref[...])

    for r in range(tm):
        pltpu.make_async_copy(h_ref.at[pl.ds(r, 1)], xs_ref.at[pl.ds(pos_ref[i * tm + r], 1)], sem).start()
    pltpu.make_async_copy(h_ref, xs_ref.at[pl.ds(0, tm)], sem).wait()


def _scatter_rows(pos, tile_end0, x1, g_ffn, n_tiles):
    t = x1.shape[0]
    tm = TM_SCAT
    grid_spec = pltpu.PrefetchScalarGridSpec(
        num_scalar_prefetch=2,
        grid=(t // tm,),
        in_specs=[pl.BlockSpec((tm, D_MODEL), lambda i, pos, tend: (i, 0)),
                  pl.BlockSpec((1, D_MODEL), lambda i, pos, tend: (0, 0))],
        out_specs=pl.BlockSpec(memory_space=pl.ANY),
        scratch_shapes=[pltpu.VMEM((tm, D_MODEL), F32), pltpu.VMEM((TM_EXP, D_MODEL), F32),
                        pltpu.SemaphoreType.DMA(()), pltpu.SemaphoreType.DMA(())],
    )
    return pl.pallas_call(
        functools.partial(_scatter_kernel, n_tiles=n_tiles),
        grid_spec=grid_spec,
        out_shape=jax.ShapeDtypeStruct((n_tiles * TM_EXP, D_MODEL), F32),
        compiler_params=_cparams("arbitrary"),
    )(pos, tile_end0, x1, g_ffn)


def _pack_pair(a, b):
    ua = lax.bitcast_convert_type(a.astype(BF16).astype(F32), jnp.uint32)
    ub = lax.bitcast_convert_type(b.astype(BF16).astype(F32), jnp.uint32)
    return (ua & jnp.uint32(0xFFFF0000)) | (ub >> 16)


def _unpack_pair(p):
    a = lax.bitcast_convert_type(p & jnp.uint32(0xFFFF0000), F32)
    b = lax.bitcast_convert_type(p << 16, F32)
    return a, b


def _expert_kernel(nused_ref, elo_ref, ehi_ref, xs_ref, wg_lo, wu_lo, wd_lo, wg_hi, wu_hi, wd_hi, ys_ref):
    del elo_ref, ehi_ref
    i = pl.program_id(0)

    @pl.when(i < nused_ref[0])
    def _():
        xb = xs_ref[...].astype(BF16)
        pre = [(jnp.dot(xb, wg[0], preferred_element_type=F32), jnp.dot(xb, wu[0], preferred_element_type=F32))
               for wg, wu in ((wg_lo, wu_lo), (wg_hi, wu_hi))]
        hid = [(_silu(gt) * up).astype(BF16) for gt, up in pre]
        y_lo, y_hi = (jnp.dot(h, wd[0], preferred_element_type=F32) for h, wd in zip(hid, (wd_lo, wd_hi)))
        ys_ref[...] = _pack_pair(y_lo, y_hi)

    @pl.when(i >= nused_ref[0])
    def _():
        ys_ref[...] = jnp.zeros(ys_ref.shape, jnp.uint32)


def _experts(n_used, tile_elo, tile_ehi, xs, w_gate, w_up, w_down):
    p_rows = xs.shape[0]
    tm = TM_EXP
    n_tiles = p_rows // tm

    def last_used(i, nu):
        return jnp.minimum(i, jnp.maximum(nu[0] - 1, 0))

    def row_map(i, nu, elo, ehi):
        return (last_used(i, nu), 0)

    def wmap(which):
        def f(i, nu, elo, ehi):
            return ((elo, ehi)[which][last_used(i, nu)], 0, 0)
        return f

    wspec_in = lambda which: pl.BlockSpec((1, D_MODEL, MOE_HIDDEN), wmap(which))
    wspec_out = lambda which: pl.BlockSpec((1, MOE_HIDDEN, D_MODEL), wmap(which))
    grid_spec = pltpu.PrefetchScalarGridSpec(
        num_scalar_prefetch=3,
        grid=(n_tiles,),
        in_specs=[pl.BlockSpec((tm, D_MODEL), row_map),
                  wspec_in(0), wspec_in(0), wspec_out(0), wspec_in(1), wspec_in(1), wspec_out(1)],
        out_specs=pl.BlockSpec((tm, D_MODEL), lambda i, nu, elo, ehi: (i, 0)),
    )
    return pl.pallas_call(
        _expert_kernel,
        grid_spec=grid_spec,
        out_shape=jax.ShapeDtypeStruct((p_rows, D_MODEL), jnp.uint32),
        compiler_params=_cparams("arbitrary"),
    )(n_used, tile_elo, tile_ehi, xs, w_gate, w_up, w_down, w_gate, w_up, w_down)


def _final_kernel(pos_ref, x1_ref, meta_ref, p_ref, ys_ref, gple_ref, wpg_ref, wpp_ref, gfin_ref,
                  out_ref, buf_ref, sem, *, n_steps):
    i = pl.program_id(0)
    tm = x1_ref.shape[0]

    def issue_tile(step, slot):
        for r in range(tm):
            pltpu.make_async_copy(ys_ref.at[pl.ds(pos_ref[step * tm + r], 1)],
                                  buf_ref.at[slot, pl.ds(r, 1)], sem.at[slot]).start()

    @pl.when(i == 0)
    def _():
        issue_tile(0, 0)

    slot = i % 2
    for nxt in range(2):
        @pl.when((i + 1 < n_steps) & (1 - slot == nxt))
        def _(nxt=nxt):
            issue_tile(i + 1, nxt)

    pltpu.make_async_copy(ys_ref.at[pl.ds(0, tm)], buf_ref.at[slot], sem.at[slot]).wait()

    nsub = 2
    subs = [{"rows": pl.ds(k * (tm // nsub), tm // nsub)} for k in range(nsub)]

    def embed(s):
        s["pp"] = jnp.dot(p_ref[s["rows"], :].astype(BF16), wpp_ref[...], preferred_element_type=F32)

    def combine(s):
        y_lo, y_hi = _unpack_pair(buf_ref[slot, s["rows"], :])
        meta = meta_ref[s["rows"], :]
        s["x2"] = x1_ref[s["rows"], :] + meta[:, 2:3] * y_lo + meta[:, 3:4] * y_hi
        s["hn"] = _rms(s["x2"], gple_ref[...]).astype(BF16)

    def gate(s):
        s["gate"] = jnp.dot(s.pop("hn"), wpg_ref[...], preferred_element_type=F32)

    def finish(s):
        out_ref[s["rows"], :] = _rms(s.pop("x2") + _sigmoid(s.pop("gate")) * s.pop("pp"), gfin_ref[...])

    for stage in (embed, combine, gate, finish):
        for s in subs:
            stage(s)


def _final(pos, x1, meta, p2d, ys, g_ple, w_pg, w_pp, g_fin):
    t = x1.shape[0]
    tm = TM_FIN
    n_steps = t // tm
    row = lambda w: pl.BlockSpec((tm, w), lambda i, pos: (i, 0))
    const = lambda a: pl.BlockSpec(a.shape, lambda i, pos: (0,) * a.ndim)
    grid_spec = pltpu.PrefetchScalarGridSpec(
        num_scalar_prefetch=1,
        grid=(n_steps,),
        in_specs=[row(D_MODEL), row(LANES), row(PLE_DIM), pl.BlockSpec(memory_space=pl.ANY),
                  const(g_ple), const(w_pg), const(w_pp), const(g_fin)],
        out_specs=row(D_MODEL),
        scratch_shapes=[pltpu.VMEM((2, tm, D_MODEL), jnp.uint32), pltpu.SemaphoreType.DMA((2,))],
    )
    return pl.pallas_call(
        functools.partial(_final_kernel, n_steps=n_steps),
        grid_spec=grid_spec,
        out_shape=jax.ShapeDtypeStruct((t, D_MODEL), F32),
        compiler_params=_cparams("arbitrary"),
    )(pos, x1, meta, p2d, ys, g_ple, w_pg, w_pp, g_fin)


_PAIR_LO = np.array([0, 0, 0, 1, 1, 2], np.int32)
_PAIR_HI = np.array([1, 2, 3, 2, 3, 3], np.int32)


def _routing_tables(meta, counts_f, n_tiles):
    cls = meta[:, 0].astype(jnp.int32)
    rank = meta[:, 1].astype(jnp.int32)
    counts = counts_f[0, :MOE_CLASSES].astype(jnp.int32)
    tiles_per = (counts + TM_EXP - 1) // TM_EXP
    tile_end = jnp.cumsum(tiles_per)
    tile_start = tile_end - tiles_per
    class_ids = jnp.arange(MOE_CLASSES, dtype=jnp.int32)
    pos = jnp.sum(jnp.where(cls[:, None] == class_ids[None, :], (tile_start * TM_EXP)[None, :], 0), axis=1) + rank
    n_used = tile_end[-1:]
    tile_ids = jnp.arange(n_tiles, dtype=jnp.int32)
    tile_cls = jnp.minimum(jnp.sum((tile_end[None, :] <= tile_ids[:, None]).astype(jnp.int32), axis=1),
                           MOE_CLASSES - 1)
    grp = tile_cls // MOE_PAIRS
    pair = tile_cls % MOE_PAIRS
    tile_elo = grp * MOE_EPG + jnp.asarray(_PAIR_LO)[pair]
    tile_ehi = grp * MOE_EPG + jnp.asarray(_PAIR_HI)[pair]
    tile_end0 = jnp.concatenate([jnp.zeros((1,), jnp.int32), tile_end.astype(jnp.int32)])
    return (pos.astype(jnp.int32), n_used.astype(jnp.int32), tile_elo.astype(jnp.int32),
            tile_ehi.astype(jnp.int32), tile_end0)


def kernel(x, p, norm_mix_g, w_in, conv_w, conv_b, dt_bias, a_log, d_skip, ssd_norm_g, w_att_branch,
           w_ssd_branch, w_out, norm_ffn_g, w_router_group, b_router_group, w_router_expert,
           b_router_expert, w_exp_gate, w_exp_up, w_exp_down, norm_ple_g, w_ple_gate, w_ple_proj,
           final_norm_g):
    bsz, seq, _ = x.shape
    t = bsz * seq
    assert w_in.shape[0] == 1, "single-layer block"
    assert seq // ATT_PATTERNS[-1][1] == ATT_BLOCK and seq % TM_IN == 0
    x2d = x.reshape(t, D_MODEL)

    wi = w_in[0]
    c_dt = QKV_W + SSD_INNER + SSD_CONV_CH
    w_qkv = wi[:, :QKV_W].astype(BF16)
    w_wide = jnp.concatenate([wi[:, QKV_W:c_dt], wi[:, c_dt + SSD_HEADS:]], axis=1).astype(BF16)
    w_dt = jnp.pad(wi[:, c_dt:c_dt + SSD_HEADS], ((0, 0), (0, LANES - SSD_HEADS))).astype(BF16)
    row = lambda v: v.reshape(1, -1).astype(F32)

    qkv1, qkv2, qkv3, z, xbc, gates, dt_raw = _in_proj(x2d, row(norm_mix_g[0]), w_qkv, w_wide, w_dt,
                                                       conv_w[0], conv_b[0], bsz, seq)

    att = _attention(qkv1.reshape(bsz, seq, ATT_WIDTH), qkv2, qkv3, bsz, seq).reshape(t, ATT_OUT)

    dt_t = dt_raw[:, :SSD_HEADS].reshape(bsz, seq, SSD_GROUPS, SSD_HPG).transpose(0, 2, 3, 1)
    u = _ssd(xbc.reshape(bsz, seq, SSD_CONV_CH), z.reshape(bsz, seq, SSD_INNER), dt_t,
             dt_bias[0], a_log[0], d_skip[0], bsz, seq).reshape(t, SSD_INNER)

    w_r32 = jnp.pad(jnp.concatenate([w_router_group[0], w_router_expert[0]], axis=1),
                    ((0, 0), (0, LANES - MOE_GROUPS - MOE_EXPERTS))).astype(F32)
    w_r_hi = w_r32.astype(BF16)
    w_r_lo = (w_r32 - w_r_hi.astype(F32)).astype(BF16)
    w_r = jnp.concatenate([w_r_hi, w_r_hi, w_r_lo], axis=0)
    b_r = jnp.pad(jnp.concatenate([b_router_group[0], b_router_expert[0]]),
                  (0, LANES - MOE_GROUPS - MOE_EXPERTS)).reshape(1, LANES).astype(F32)
    x1, meta, counts = _post(att, u, gates, x2d,
                             w_att_branch[0].astype(BF16), w_ssd_branch[0].astype(BF16), w_out[0].astype(BF16),
                             row(ssd_norm_g[0]), row(norm_ffn_g[0]), w_r, b_r)

    n_tiles = t // TM_EXP + MOE_CLASSES
    pos, n_used, tile_elo, tile_ehi, tile_end0 = _routing_tables(meta, counts, n_tiles)
    xs = _scatter_rows(pos, tile_end0, x1, row(norm_ffn_g[0]), n_tiles)
    ys = _experts(n_used, tile_elo, tile_ehi, xs,
                  w_exp_gate[0].astype(BF16), w_exp_up[0].astype(BF16), w_exp_down[0].astype(BF16))
    out = _final(pos, x1, meta, p[0].reshape(t, PLE_DIM), ys, row(norm_ple_g[0]),
                 w_ple_gate[0].astype(BF16), w_ple_proj[0].astype(BF16), row(final_norm_g))
    return out.reshape(bsz, seq, D_MODEL)
```

```python
import functools

import numpy as np
import jax
import jax.numpy as jnp
from jax import lax
from jax.experimental import pallas as pl
from jax.experimental.pallas import tpu as pltpu

F32 = jnp.float32
BF16 = jnp.bfloat16

D_MODEL = 1024
PLE_DIM = 256
RMS_EPS = 1e-6

ATT_PATTERNS = ((128, 1), (512, 4), (2048, 16))
ATT_GROUPS = 3
ATT_HPG = 8
ATT_HEAD_DIM = 64
ATT_WIDTH = ATT_GROUPS * ATT_HPG * ATT_HEAD_DIM
ATT_OUT = ATT_HPG * ATT_HEAD_DIM
ATT_BLOCK = 128
ATT_LOCKSTEP = 2
ALIBI_MAX_BIAS = 8.0
QKV_W = 3 * ATT_WIDTH

SSD_INNER = 2048
SSD_HEADS = 32
SSD_GROUPS = 8
SSD_HPG = 4
SSD_HEAD_DIM = 64
SSD_STATE = 128
SSD_CONV = 4
SSD_CHUNK = 128
SSD_CONV_CH = SSD_INNER + 2 * SSD_GROUPS * SSD_STATE
SSD_GW = SSD_HPG * SSD_HEAD_DIM
SSD_LOCKSTEP = 4

MOE_GROUPS = 4
MOE_EPG = 4
MOE_EXPERTS = 16
MOE_HIDDEN = 512
MOE_PAIRS = 6
MOE_CLASSES = MOE_GROUPS * MOE_PAIRS

LANES = 128
NEG_BIG = -1e30
VMEM_LIMIT = 56 * 1024 * 1024

TM_IN = 1024
TN_IN = 512
TN_WIDE = 1024
TM_POST = 512
TM_EXP = 256
TM_FIN = 512


def _cparams(*sem):
    return pltpu.CompilerParams(dimension_semantics=sem, vmem_limit_bytes=VMEM_LIMIT)


def _sigmoid(x):
    return 0.5 * jnp.tanh(0.5 * x) + 0.5


def _silu(x):
    h = 0.5 * x
    return h + h * jnp.tanh(h)


def _rms(x, g):
    ms = jnp.mean(x * x, axis=-1, keepdims=True)
    return x * lax.rsqrt(ms + RMS_EPS) * g


_IN_SEG = (3, 3, 3, SSD_INNER // TN_WIDE, SSD_CONV_CH // TN_WIDE, 2 * D_MODEL // TN_WIDE)
_IN_START = tuple(int(v) for v in np.cumsum((0,) + _IN_SEG))
_IN_NARROW = _IN_START[3]


def _inproj_kernel(x_ref, g_ref, wq_ref, ww_ref, wdt_ref, cw_ref, cb_ref, qkv1_ref, qkv2_ref, qkv3_ref, z_ref,
                   xbc_ref, gates_ref, dt_ref, h_ref, hcol_ref, halo_ref, *, per_seq):
    j = pl.program_id(1)
    tm = x_ref.shape[0]

    @pl.when(j == 0)
    def _():
        h = _rms(x_ref[...], g_ref[...])
        hb = h.astype(BF16)
        h_ref[0] = hb
        dt_ref[...] = jnp.dot(hb, wdt_ref[...], preferred_element_type=F32)
        ncb = hcol_ref.shape[0]
        for c in range(ncb):
            hcol_ref[c] = h[:, c * LANES:(c + 1) * LANES]
        for slot, (_, dil) in enumerate(ATT_PATTERNS[1:], start=1):
            rows = tm // dil
            for r in range(dil):
                for c in range(ncb):
                    h_ref[slot, r * rows:(r + 1) * rows, c * LANES:(c + 1) * LANES] = (
                        hcol_ref[c, pl.ds(r, rows, stride=dil), :].astype(BF16))

    def segment(k, fn):
        @pl.when((j >= _IN_START[k]) & (j < _IN_START[k + 1]))
        def _():
            fn()

    def narrow(slot, store):
        def fn():
            store(jnp.dot(h_ref[slot], wq_ref[...], preferred_element_type=F32).astype(BF16))
        return fn

    def wide(ref):
        def fn():
            ref[...] = jnp.dot(h_ref[0], ww_ref[...], preferred_element_type=F32).astype(BF16)
        return fn

    def store_plain(ref):
        def store(res):
            ref[...] = res
        return store

    def store_grouped(ref, dil):
        def store(res):
            ref[0] = res.reshape(dil, tm // dil, res.shape[1])
        return store

    def conv_silu():
        jb = j - _IN_START[4]
        nsub = 4
        rs = tm // nsub
        res = [jnp.dot(h_ref[0, k * rs:(k + 1) * rs, :], ww_ref[...], preferred_element_type=F32)
               for k in range(nsub)]
        seq_start = pl.program_id(0) % per_seq == 0
        prev = jnp.where(seq_start, 0.0, halo_ref[jb])
        row8 = lax.broadcasted_iota(jnp.int32, (8, 1), 0)
        cw = cw_ref[...]
        for k in range(nsub):
            acc = cb_ref[...] + cw[SSD_CONV - 1:SSD_CONV, :] * res[k]
            for sh in range(1, SSD_CONV):
                rolled = pltpu.roll(res[k], sh, 0)
                top = jnp.where(row8 < sh, pltpu.roll(prev, sh, 0), rolled[0:8])
                shifted = jnp.concatenate([top, rolled[8:]], axis=0)
                acc = acc + cw[SSD_CONV - 1 - sh:SSD_CONV - sh, :] * shifted
            xbc_ref[k * rs:(k + 1) * rs, :] = acc.astype(BF16)
            prev = res[k][rs - 8:rs]
        halo_ref[jb] = prev

    segment(0, narrow(0, store_plain(qkv1_ref)))
    segment(1, narrow(1, store_grouped(qkv2_ref, ATT_PATTERNS[1][1])))
    segment(2, narrow(2, store_grouped(qkv3_ref, ATT_PATTERNS[2][1])))
    segment(3, wide(z_ref))
    segment(4, conv_silu)
    segment(5, wide(gates_ref))


def _in_proj(x2d, g, w_qkv, w_wide, w_dt, conv_w, conv_b, bsz, seq):
    t = x2d.shape[0]
    nj = _IN_START[-1]
    per_seq = seq // TM_IN
    d2, d3 = ATT_PATTERNS[1][1], ATT_PATTERNS[2][1]

    def seg_map(k):
        return lambda i, j: (i, jnp.clip(j - _IN_START[k], 0, _IN_SEG[k] - 1))

    def grp_map(k):
        return lambda i, j: (i // per_seq, 0, i % per_seq, jnp.clip(j - _IN_START[k], 0, _IN_SEG[k] - 1))

    def wq_map(i, j):
        jj = jnp.minimum(j, _IN_NARROW - 1)
        return (0, (jj % 3) * ATT_GROUPS + jj // 3)

    out_shapes = (
        jax.ShapeDtypeStruct((t, ATT_WIDTH), BF16),
        jax.ShapeDtypeStruct((bsz, d2, seq // d2, ATT_WIDTH), BF16),
        jax.ShapeDtypeStruct((bsz, d3, seq // d3, ATT_WIDTH), BF16),
        jax.ShapeDtypeStruct((t, SSD_INNER), BF16),
        jax.ShapeDtypeStruct((t, SSD_CONV_CH), BF16),
        jax.ShapeDtypeStruct((t, 2 * D_MODEL), BF16),
        jax.ShapeDtypeStruct((t, LANES), F32),
    )
    out_specs = (
        pl.BlockSpec((TM_IN, TN_IN), seg_map(0)),
        pl.BlockSpec((1, d2, TM_IN // d2, TN_IN), grp_map(1)),
        pl.BlockSpec((1, d3, TM_IN // d3, TN_IN), grp_map(2)),
        pl.BlockSpec((TM_IN, TN_WIDE), seg_map(3)),
        pl.BlockSpec((TM_IN, TN_WIDE), seg_map(4)),
        pl.BlockSpec((TM_IN, TN_WIDE), seg_map(5)),
        pl.BlockSpec((TM_IN, LANES), lambda i, j: (i, 0)),
    )
    conv_map = lambda i, j: (0, jnp.clip(j - _IN_START[4], 0, _IN_SEG[4] - 1))
    return pl.pallas_call(
        functools.partial(_inproj_kernel, per_seq=per_seq),
        grid=(t // TM_IN, nj),
        in_specs=[
            pl.BlockSpec((TM_IN, D_MODEL), lambda i, j: (i, 0)),
            pl.BlockSpec((1, D_MODEL), lambda i, j: (0, 0)),
            pl.BlockSpec((D_MODEL, TN_IN), wq_map),
            pl.BlockSpec((D_MODEL, TN_WIDE), lambda i, j: (0, jnp.maximum(j - _IN_NARROW, 0))),
            pl.BlockSpec((D_MODEL, LANES), lambda i, j: (0, 0)),
            pl.BlockSpec((SSD_CONV, TN_WIDE), conv_map),
            pl.BlockSpec((1, TN_WIDE), conv_map),
        ],
        out_specs=out_specs,
        out_shape=out_shapes,
        scratch_shapes=[pltpu.VMEM((3, TM_IN, D_MODEL), BF16),
                        pltpu.VMEM((D_MODEL // LANES, TM_IN, LANES), F32),
                        pltpu.VMEM((_IN_SEG[4], 8, TN_WIDE), F32)],
        compiler_params=_cparams("arbitrary", "arbitrary"),
    )(x2d, g, w_qkv, w_wide, w_dt, conv_w.astype(F32), conv_b.reshape(1, SSD_CONV_CH).astype(F32))


def _att_bias_tables():
    h = np.arange(1, ATT_GROUPS * ATT_HPG + 1, dtype=np.float32)
    slopes = np.exp2(-ALIBI_MAX_BIAS * h / (ATT_GROUPS * ATT_HPG)).astype(np.float32).reshape(ATT_GROUPS, ATT_HPG)
    qi = np.arange(ATT_BLOCK)[:, None] + ATT_BLOCK
    kj = np.arange(2 * ATT_BLOCK)[None, :]
    delta = qi - kj
    tabs = []
    for g, (window, dil) in enumerate(ATT_PATTERNS):
        span = window // dil
        band = (delta >= 0) & (delta <= span)
        bias = (-slopes[g][:, None, None] * (delta * dil).astype(np.float32)[None]).astype(np.float32)
        with_prev = np.where(band[None], bias, np.float32(NEG_BIG))
        first = np.where((band & (kj >= ATT_BLOCK))[None], bias, np.float32(NEG_BIG))
        tabs.append(np.stack([first, with_prev]).astype(np.float32))
    return tabs[0], tabs[1], tabs[2][1][:, :, ATT_BLOCK:]


def _att_units(units):
    lane = lax.broadcasted_iota(jnp.int32, (1, LANES), 1)
    low = lane < ATT_HEAD_DIM
    scale = ATT_HEAD_DIM ** -0.5
    qmask = (jnp.where(low, scale, 0.0).astype(BF16), jnp.where(low, 0.0, scale).astype(BF16))
    heads = [(q, k2, v2, tab_fn, hh) for q, k2, v2, tab_fn in units for hh in range(2)]
    scores = [lax.dot_general(q * qmask[hh], k2, (((1,), (1,)), ((), ())), preferred_element_type=F32)
              for q, k2, _, _, hh in heads]
    probs = []
    for s, (_, _, _, tab_fn, hh) in zip(scores, heads):
        s = s + tab_fn(hh)
        m = jnp.max(s, axis=-1, keepdims=True)
        e = jnp.exp(s - m)
        probs.append((e.astype(BF16), m, jnp.sum(e, axis=-1, keepdims=True)))
    pvs = [jnp.dot(e, v2, preferred_element_type=F32) for (e, _, _), (_, _, v2, _, _) in zip(probs, heads)]
    outs = [pv / den for pv, (_, _, den) in zip(pvs, probs)]
    lses = [m + jnp.log(den) for _, m, den in probs]
    return [(jnp.where(low, outs[2 * i], outs[2 * i + 1]), jnp.where(low, lses[2 * i], lses[2 * i + 1]))
            for i in range(len(units))]


def _att_kernel(q1, k1, v1, q2, k2, v2, q3, k3, v3, t1, t2, t3, out_ref,
                o1, l1, o2p, l2p, o2n, l2n, o3p, l3p, o3n, l3n, *, seq):
    blk = ATT_BLOCK
    d2, d3 = ATT_PATTERNS[1][1], ATT_PATTERNS[2][1]
    nb1, nb2 = seq // blk, seq // d2 // blk

    def rows(n):
        return pl.ds(pl.multiple_of(n * blk, blk), blk)

    assert nb1 == d2 * nb2 == d3

    def unit_body(i, _):
        units, dests = [], []
        for k in range(ATT_LOCKSTEP):
            u = i * ATT_LOCKSTEP + k
            cur, prv = rows(u), rows(jnp.maximum(u - 1, 0))
            sel = jnp.minimum(u, 1)
            units.append((q1[0, cur, :], jnp.concatenate([k1[0, prv, :], k1[0, cur, :]], axis=0),
                          jnp.concatenate([v1[0, prv, :], v1[0, cur, :]], axis=0),
                          lambda hh, sel=sel: t1[sel, hh]))
            r, n = u // nb2, u % nb2
            cur2, prv2 = rows(n), rows(jnp.maximum(n - 1, 0))
            sel2 = jnp.minimum(n, 1)
            units.append((q2[0, r, cur2, :], jnp.concatenate([k2[0, r, prv2, :], k2[0, r, cur2, :]], axis=0),
                          jnp.concatenate([v2[0, r, prv2, :], v2[0, r, cur2, :]], axis=0),
                          lambda hh, sel2=sel2: t2[sel2, hh]))
            units.append((q3[0, u], k3[0, u], v3[0, u], lambda hh: t3[hh]))
            dests += [(o1, l1, cur), (o2p, l2p, cur), (o3p, l3p, cur)]
        for (o, l), (o_ref, l_ref, where) in zip(_att_units(units), dests):
            o_ref[where, :] = o
            l_ref[where, :] = l
        return 0

    lax.fori_loop(0, nb1 // ATT_LOCKSTEP, unit_body, 0)

    for dil, pairs in ((d2, ((o2p, o2n), (l2p, l2n))), (d3, ((o3p, o3n), (l3p, l3n)))):
        n_sub = seq // dil
        for r in range(dil):
            for src, dst in pairs:
                dst[pl.ds(r, n_sub, stride=dil), :] = src[r * n_sub:(r + 1) * n_sub, :]

    mrows = 2 * blk

    def merge(c, _):
        rr = pl.ds(pl.multiple_of(c * mrows, mrows), mrows)
        la, lb, lc = l1[rr, :], l2n[rr, :], l3n[rr, :]
        lm = jnp.maximum(jnp.maximum(la, lb), lc)
        ea, eb, ec = jnp.exp(la - lm), jnp.exp(lb - lm), jnp.exp(lc - lm)
        att = (ea * o1[rr, :] + eb * o2n[rr, :] + ec * o3n[rr, :]) / (ea + eb + ec)
        out_ref[0, rr, :] = att.astype(BF16)
        return 0

    lax.fori_loop(0, seq // mrows, merge, 0)


def _attention(qkv1, qkv2, qkv3, bsz, seq):
    t1, t2, t3 = (jnp.asarray(t) for t in _att_bias_tables())
    d2, d3 = ATT_PATTERNS[1][1], ATT_PATTERNS[2][1]
    npair = ATT_HPG // 2
    in_arrays, in_specs = [], []
    for arr, lead in ((qkv1, ()), (qkv2, (d2,)), (qkv3, (d3,))):
        n_rows = arr.shape[-2]
        for sel in range(3):
            zeros = (0,) * len(lead)
            in_arrays.append(arr)
            in_specs.append(pl.BlockSpec((1,) + lead + (n_rows, LANES),
                                         lambda hp, b, sel=sel, zeros=zeros: (b,) + zeros + (0, sel * npair + hp)))
    in_arrays += [t1, t2, t3]
    in_specs += [pl.BlockSpec((2, 2, ATT_BLOCK, 2 * ATT_BLOCK), lambda hp, b: (0, hp, 0, 0)),
                 pl.BlockSpec((2, 2, ATT_BLOCK, 2 * ATT_BLOCK), lambda hp, b: (0, hp, 0, 0)),
                 pl.BlockSpec((2, ATT_BLOCK, ATT_BLOCK), lambda hp, b: (hp, 0, 0))]
    return pl.pallas_call(
        functools.partial(_att_kernel, seq=seq),
        grid=(npair, bsz),
        in_specs=in_specs,
        out_specs=pl.BlockSpec((1, seq, LANES), lambda hp, b: (b, 0, hp)),
        out_shape=jax.ShapeDtypeStruct((bsz, seq, ATT_OUT), BF16),
        scratch_shapes=[pltpu.VMEM((seq, LANES), F32)] * 10,
        compiler_params=_cparams("arbitrary", "arbitrary"),
    )(*in_arrays)


def _softplus(x):
    return jnp.maximum(x, 0.0) + jnp.log1p(jnp.exp(-jnp.abs(x)))


def _head_spread_table():
    gw, n_src = SSD_GW, LANES
    tab = np.zeros((2 * n_src, 2 * gw), np.float32)
    for half in range(2):
        for blk, src0 in enumerate((4, 8)):
            for col in range(gw):
                tab[half * n_src + src0 + col // SSD_HEAD_DIM, blk * gw + col] = 1.0
    return tab


def _ssd_kernel(x_ref, b_ref, c_ref, z_ref, dt_ref, spread_ref, dtb_ref, alog_ref, dsk_ref, u_ref, carry_ref,
                *, n_chunks):
    L = SSD_CHUNK
    gw = SSD_GW
    carry_ref[...] = jnp.zeros(carry_ref.shape, F32)

    a_neg = -jnp.exp(alog_ref[0])
    dtb = dtb_ref[0]
    dsk = dsk_ref[0]
    ri = lax.broadcasted_iota(jnp.int32, (L, L), 0)
    ci = lax.broadcasted_iota(jnp.int32, (L, L), 1)
    upper_incl = (ri <= ci).astype(F32)
    causal = ri >= ci
    lane = lax.broadcasted_iota(jnp.int32, (1, LANES), 1)
    low = lane < SSD_HEAD_DIM
    heads = range(SSD_HPG)

    def load(s):
        rows = pl.ds(s["r0"], L)
        s["xs"] = _silu(x_ref[0, rows, :].astype(F32))
        s["xs_b"] = s["xs"].astype(BF16)
        s["bm"] = _silu(b_ref[0, rows, :].astype(F32))
        s["cm_b"] = _silu(c_ref[0, rows, :].astype(F32)).astype(BF16)

    def decay_cumsum(s):
        dt = _softplus(dt_ref[0, 0, :, pl.ds(s["r0"], L)] + dtb)
        rows8 = jnp.concatenate([dt * a_neg, dt], axis=0)
        cs8 = jnp.dot(rows8, upper_incl, preferred_element_type=F32, precision=lax.Precision.HIGHEST)
        s["dt"], s["acs_t"] = dt, cs8[0:SSD_HPG]

    def decay_spread(s):
        dt, acs_t = s["dt"], s["acs_t"]
        t16 = jnp.concatenate([acs_t, acs_t, acs_t, dt], axis=0)
        cols = jnp.concatenate([t16, jnp.zeros((L - 16, L), F32)], axis=0).T
        dt_at8 = pltpu.roll(cols, LANES - 4, 1)
        comb = jnp.where(lane < 8, jnp.exp(cols), jnp.exp(cols[L - 1:L, :] - cols) * dt_at8)
        hi = comb.astype(BF16)
        lo = (comb - hi.astype(F32)).astype(BF16)
        s["spread"] = jnp.dot(jnp.concatenate([hi, lo], axis=1), spread_ref[...], preferred_element_type=F32)
        s["acs_cols"] = cols

    def scores(s):
        s["bm_t"] = s.pop("bm").T.astype(BF16)
        s["cb"] = jnp.dot(s["cm_b"], s["bm_t"], preferred_element_type=F32)

    def intra(s):
        spread, acs_t, dt, cb, cols = s["spread"], s.pop("acs_t"), s.pop("dt"), s.pop("cb"), s.pop("acs_cols")
        xs, xs_b = s["xs"], s.pop("xs_b")
        mixes = []
        for j in heads:
            seg = cols[:, j:j + 1] - acs_t[j:j + 1, :]
            mixes.append((jnp.exp(jnp.where(causal, seg, -jnp.inf)) * (cb * dt[j:j + 1, :])).astype(BF16))
        halves = []
        for hp in range(SSD_HPG // 2):
            yy = jnp.dot(jnp.concatenate(mixes[2 * hp:2 * hp + 2], axis=0), xs_b[:, hp * LANES:(hp + 1) * LANES],
                         preferred_element_type=F32)
            halves.append(jnp.where(low, yy[:L], yy[L:]))
        s["y"] = jnp.concatenate(halves, axis=1)
        s["st_new"] = jnp.dot(s.pop("bm_t"), (xs * spread[:, gw:]).astype(BF16),
                              preferred_element_type=F32)

    def inter(s):
        eacs_bc = s.pop("spread")[:, :gw]
        carry = carry_ref[...]
        y = s.pop("y") + jnp.dot(s.pop("cm_b"), carry.astype(BF16), preferred_element_type=F32) * eacs_bc
        carry_ref[...] = carry * eacs_bc[L - 1:L, :] + s.pop("st_new")
        y = y + s.pop("xs") * dsk
        rows = pl.ds(s["r0"], L)
        u_ref[0, rows, :] = (y * _silu(z_ref[0, rows, :].astype(F32))).astype(BF16)

    def body(i, _):
        states = [{"c": i * SSD_LOCKSTEP + k, "r0": pl.multiple_of((i * SSD_LOCKSTEP + k) * L, L)}
                  for k in range(SSD_LOCKSTEP)]
        for stage in (load, decay_cumsum, decay_spread, scores, intra, inter):
            for s in states:
                stage(s)
        return 0

    lax.fori_loop(0, n_chunks // SSD_LOCKSTEP, body, 0)


def _ssd(xbc, z, dt_t, dt_bias, a_log, d_skip, bsz, seq):
    gw = SSD_GW
    nxb = SSD_INNER // SSD_STATE
    dsk = jnp.repeat(d_skip.astype(F32), SSD_HEAD_DIM).reshape(SSD_GROUPS, 1, gw)
    dtb = dt_bias.astype(F32).reshape(SSD_GROUPS, SSD_HPG, 1)
    alog = a_log.astype(F32).reshape(SSD_GROUPS, SSD_HPG, 1)
    spread = jnp.asarray(_head_spread_table(), BF16)
    x_map = lambda b, g: (b, 0, g)
    bm_map = lambda b, g: (b, 0, nxb + g)
    cm_map = lambda b, g: (b, 0, nxb + SSD_GROUPS + g)
    return pl.pallas_call(
        functools.partial(_ssd_kernel, n_chunks=seq // SSD_CHUNK),
        grid=(bsz, SSD_GROUPS),
        in_specs=[
            pl.BlockSpec((1, seq, gw), x_map),
            pl.BlockSpec((1, seq, SSD_STATE), bm_map),
            pl.BlockSpec((1, seq, SSD_STATE), cm_map),
            pl.BlockSpec((1, seq, gw), x_map),
            pl.BlockSpec((1, 1, SSD_HPG, seq), lambda b, g: (b, g, 0, 0)),
            pl.BlockSpec(spread.shape, lambda b, g: (0, 0)),
            pl.BlockSpec((1, SSD_HPG, 1), lambda b, g: (g, 0, 0)),
            pl.BlockSpec((1, SSD_HPG, 1), lambda b, g: (g, 0, 0)),
            pl.BlockSpec((1, 1, gw), lambda b, g: (g, 0, 0)),
        ],
        out_specs=pl.BlockSpec((1, seq, gw), x_map),
        out_shape=jax.ShapeDtypeStruct((bsz, seq, SSD_INNER), BF16),
        scratch_shapes=[pltpu.VMEM((SSD_STATE, gw), F32)],
        compiler_params=_cparams("arbitrary", "arbitrary"),
    )(xbc, xbc, xbc, z, dt_t, spread, dtb, alog, dsk)


def _first_index_of_max(vals, lane_f):
    m = jnp.max(vals, axis=-1, keepdims=True)
    idx = jnp.min(jnp.where(vals == m, lane_f, float(LANES)), axis=-1, keepdims=True)
    return m, idx


def _post_kernel(att_ref, u_ref, gates_ref, x_ref, watt_ref, wssd_ref, wout_ref,
                 gssd_ref, gffn_ref, wr_ref, br_ref, x1_ref, meta_ref, cnt_ref, run_ref):
    i = pl.program_id(0)

    @pl.when(i == 0)
    def _():
        run_ref[...] = jnp.zeros(run_ref.shape, F32)

    tm = x_ref.shape[0]
    nsub = 2
    subs = [{"rows": pl.ds(k * (tm // nsub), tm // nsub)} for k in range(nsub)]

    def att_branch(s):
        s["y_att"] = jnp.dot(att_ref[s["rows"], :], watt_ref[...], preferred_element_type=F32)

    def ssd_branch(s):
        ssd = _rms(u_ref[s["rows"], :].astype(F32), gssd_ref[...])
        s["y_ssd"] = jnp.dot(ssd.astype(BF16), wssd_ref[...], preferred_element_type=F32)

    def mix_out(s):
        gates = gates_ref[s["rows"], :].astype(F32)
        merged = _sigmoid(gates[:, :D_MODEL]) * s.pop("y_att") + _sigmoid(gates[:, D_MODEL:]) * s.pop("y_ssd")
        x1 = x_ref[s["rows"], :] + jnp.dot(merged.astype(BF16), wout_ref[...], preferred_element_type=F32)
        x1_ref[s["rows"], :] = x1
        s["h2"] = _rms(x1, gffn_ref[...])

    def router(s):
        h2 = s.pop("h2")
        hi = h2.astype(BF16)
        lo = (h2 - hi.astype(F32)).astype(BF16)
        s["logits"] = jnp.dot(jnp.concatenate([hi, lo, hi], axis=1), wr_ref[...], preferred_element_type=F32)

    for stage in (att_branch, ssd_branch, mix_out, router):
        for s in subs:
            stage(s)
    logits = jnp.concatenate([s["logits"] for s in subs], axis=0) + br_ref[...]
    lane = lax.broadcasted_iota(jnp.int32, (tm, LANES), 1)
    lane_f = lane.astype(F32)
    ninf = -jnp.inf
    gl = jnp.where(lane < MOE_GROUPS, logits, ninf)
    gmax, gidx = _first_index_of_max(gl, lane_f)
    g_val = 1.0 / jnp.sum(jnp.exp(gl - gmax), axis=-1, keepdims=True)
    base = MOE_GROUPS + MOE_EPG * gidx
    el = jnp.where((lane_f >= base) & (lane_f < base + MOE_EPG), logits, ninf)
    e1, i1 = _first_index_of_max(el, lane_f)
    e2, i2 = _first_index_of_max(jnp.where(lane_f == i1, ninf, el), lane_f)
    t2 = jnp.exp(e2 - e1)
    w1 = g_val / (1.0 + t2)
    w2 = g_val * t2 / (1.0 + t2)
    a1, a2 = i1 - base, i2 - base
    lo, hi = jnp.minimum(a1, a2), jnp.maximum(a1, a2)
    c_lo = jnp.where(a1 < a2, w1, w2)
    c_hi = jnp.where(a1 < a2, w2, w1)
    pair = lo * (7.0 - lo) * 0.5 + (hi - lo - 1.0)
    cls = gidx * MOE_PAIRS + pair

    onehot = (lane_f == cls)
    oh_b = jnp.where(onehot, 1.0, 0.0).astype(BF16)
    rr = lax.broadcasted_iota(jnp.int32, (tm, tm), 0)
    cc = lax.broadcasted_iota(jnp.int32, (tm, tm), 1)
    strict = jnp.where(rr > cc, 1.0, 0.0).astype(BF16)
    prefix = jnp.dot(strict, oh_b, preferred_element_type=F32) + run_ref[...]
    rank = jnp.sum(jnp.where(onehot, prefix, 0.0), axis=-1, keepdims=True)
    run = run_ref[...] + jnp.sum(oh_b.astype(F32), axis=0, keepdims=True)
    run_ref[...] = run
    cnt_ref[...] = jnp.broadcast_to(run, cnt_ref.shape)

    meta = jnp.where(lane == 0, cls, jnp.where(lane == 1, rank, jnp.where(lane == 2, c_lo,
                     jnp.where(lane == 3, c_hi, 0.0))))
    meta_ref[...] = meta


def _post(att, u, gates, x2d, w_att, w_ssd, w_out, g_ssd, g_ffn, w_r, b_r):
    t = x2d.shape[0]
    tm = TM_POST
    row = lambda w: pl.BlockSpec((tm, w), lambda i: (i, 0))
    const = lambda a: pl.BlockSpec(a.shape, lambda i: (0,) * a.ndim)
    return pl.pallas_call(
        _post_kernel,
        grid=(t // tm,),
        in_specs=[row(ATT_OUT), row(SSD_INNER), row(2 * D_MODEL), row(D_MODEL),
                  const(w_att), const(w_ssd), const(w_out), const(g_ssd), const(g_ffn), const(w_r), const(b_r)],
        out_specs=(row(D_MODEL), row(LANES), pl.BlockSpec((8, LANES), lambda i: (0, 0))),
        out_shape=(jax.ShapeDtypeStruct((t, D_MODEL), F32),
                   jax.ShapeDtypeStruct((t, LANES), F32),
                   jax.ShapeDtypeStruct((8, LANES), F32)),
        scratch_shapes=[pltpu.VMEM((1, LANES), F32)],
        compiler_params=_cparams("arbitrary"),
    )(att, u, gates, x2d, w_att, w_ssd, w_out, g_ssd, g_ffn, w_r, b_r)


def _pack_pair(a, b):
    ua = lax.bitcast_convert_type(a.astype(BF16).astype(F32), jnp.uint32)
    ub = lax.bitcast_convert_type(b.astype(BF16).astype(F32), jnp.uint32)
    return (ua & jnp.uint32(0xFFFF0000)) | (ub >> 16)


def _unpack_pair(p):
    a = lax.bitcast_convert_type(p & jnp.uint32(0xFFFF0000), F32)
    b = lax.bitcast_convert_type(p << 16, F32)
    return a, b


def _expert_kernel(nused_ref, elo_ref, ehi_ref, src_ref, x1_ref, g_ref, wg_lo, wu_lo, wd_lo, wg_hi, wu_hi, wd_hi,
                   ys_ref, buf_ref, sem):
    del elo_ref, ehi_ref
    i = pl.program_id(0)
    tm = ys_ref.shape[0]
    n_used = nused_ref[0]

    def issue_rows(tile, slot, r0, r1):
        for r in range(r0, r1):
            pltpu.make_async_copy(x1_ref.at[pl.ds(src_ref[tile * tm + r], 1)],
                                  buf_ref.at[slot, pl.ds(r, 1)], sem.at[slot]).start()

    def wait_slot(slot):
        pltpu.make_async_copy(x1_ref.at[pl.ds(0, tm)], buf_ref.at[slot], sem.at[slot]).wait()

    @pl.when(i == 0)
    def _():
        issue_rows(0, 0, 0, tm)

    @pl.when(i < n_used)
    def _():
        slot = i % 2
        wait_slot(slot)
        nxt = jnp.minimum(i + 1, n_used - 1)
        third = tm // 3
        xb = _rms(buf_ref[slot], g_ref[...]).astype(BF16)
        issue_rows(nxt, 1 - slot, 0, third)
        pre = [(jnp.dot(xb, wg[0], preferred_element_type=F32), jnp.dot(xb, wu[0], preferred_element_type=F32))
               for wg, wu in ((wg_lo, wu_lo), (wg_hi, wu_hi))]
        issue_rows(nxt, 1 - slot, third, 2 * third)
        hid = [(_silu(gt) * up).astype(BF16) for gt, up in pre]
        y_lo, y_hi = (jnp.dot(h, wd[0], preferred_element_type=F32) for h, wd in zip(hid, (wd_lo, wd_hi)))
        issue_rows(nxt, 1 - slot, 2 * third, tm)
        ys_ref[...] = _pack_pair(y_lo, y_hi)

        @pl.when(i == n_used - 1)
        def _():
            wait_slot(1 - slot)

    @pl.when(i >= n_used)
    def _():
        ys_ref[...] = jnp.zeros(ys_ref.shape, jnp.uint32)


def _experts(n_used, tile_elo, tile_ehi, src, x1, g_ffn, w_gate, w_up, w_down):
    tm = TM_EXP
    n_tiles = src.shape[0] // tm

    def wmap(which):
        def f(i, nu, elo, ehi, src):
            return ((elo, ehi)[which][jnp.minimum(i, jnp.maximum(nu[0] - 1, 0))], 0, 0)
        return f

    wspec_in = lambda which: pl.BlockSpec((1, D_MODEL, MOE_HIDDEN), wmap(which))
    wspec_out = lambda which: pl.BlockSpec((1, MOE_HIDDEN, D_MODEL), wmap(which))
    grid_spec = pltpu.PrefetchScalarGridSpec(
        num_scalar_prefetch=4,
        grid=(n_tiles,),
        in_specs=[pl.BlockSpec(memory_space=pl.ANY),
                  pl.BlockSpec((1, D_MODEL), lambda i, nu, elo, ehi, src: (0, 0)),
                  wspec_in(0), wspec_in(0), wspec_out(0), wspec_in(1), wspec_in(1), wspec_out(1)],
        out_specs=pl.BlockSpec((tm, D_MODEL), lambda i, nu, elo, ehi, src: (i, 0)),
        scratch_shapes=[pltpu.VMEM((2, tm, D_MODEL), F32), pltpu.SemaphoreType.DMA((2,))],
    )
    return pl.pallas_call(
        _expert_kernel,
        grid_spec=grid_spec,
        out_shape=jax.ShapeDtypeStruct((n_tiles * tm, D_MODEL), jnp.uint32),
        compiler_params=_cparams("arbitrary"),
    )(n_used, tile_elo, tile_ehi, src, x1, g_ffn, w_gate, w_up, w_down, w_gate, w_up, w_down)


def _final_kernel(pos_ref, x1_ref, meta_ref, p_ref, ys_ref, gple_ref, wpg_ref, wpp_ref, gfin_ref,
                  out_ref, buf_ref, sem, *, n_steps):
    i = pl.program_id(0)
    tm = buf_ref.shape[1]

    def issue_rows(tile, slot, r0, r1):
        for r in range(r0, r1):
            pltpu.make_async_copy(ys_ref.at[pl.ds(pos_ref[tile * tm + r], 1)],
                                  buf_ref.at[slot, pl.ds(r, 1)], sem.at[slot]).start()

    def wait_slot(slot):
        pltpu.make_async_copy(ys_ref.at[pl.ds(0, tm)], buf_ref.at[slot], sem.at[slot]).wait()

    @pl.when(i == 0)
    def _():
        issue_rows(0, 0, 0, tm)

    def compute_tile(slot, fetch_tile, fetch_slot):
        nsub = 2
        rs = tm // nsub
        subs = [{"buf": pl.ds(k * rs, rs), "rows": pl.ds(slot * tm + k * rs, rs)} for k in range(nsub)]

        def embed(s):
            s["pp"] = jnp.dot(p_ref[s["rows"], :].astype(BF16), wpp_ref[...], preferred_element_type=F32)

        def combine(s):
            y_lo, y_hi = _unpack_pair(buf_ref[slot, s["buf"], :])
            meta = meta_ref[s["rows"], :]
            s["x2"] = x1_ref[s["rows"], :] + meta[:, 2:3] * y_lo + meta[:, 3:4] * y_hi
            s["hn"] = _rms(s["x2"], gple_ref[...]).astype(BF16)

        def gate(s):
            s["gate"] = jnp.dot(s.pop("hn"), wpg_ref[...], preferred_element_type=F32)

        def finish(s):
            out_ref[s["rows"], :] = _rms(s.pop("x2") + _sigmoid(s.pop("gate")) * s.pop("pp"), gfin_ref[...])

        stages = (embed, combine, gate, finish)
        per_stage = tm // len(stages)
        for k, stage in enumerate(stages):
            issue_rows(fetch_tile, fetch_slot, k * per_stage, (k + 1) * per_stage)
            for s in subs:
                stage(s)

    wait_slot(0)
    compute_tile(0, 2 * i + 1, 1)
    wait_slot(1)
    compute_tile(1, jnp.minimum(2 * i + 2, 2 * n_steps - 1), 0)

    @pl.when(i == n_steps - 1)
    def _():
        wait_slot(0)


def _final(pos, x1, meta, p2d, ys, g_ple, w_pg, w_pp, g_fin):
    t = x1.shape[0]
    tm = TM_FIN
    n_steps = t // (2 * tm)
    row = lambda w: pl.BlockSpec((2 * tm, w), lambda i, pos: (i, 0))
    const = lambda a: pl.BlockSpec(a.shape, lambda i, pos: (0,) * a.ndim)
    grid_spec = pltpu.PrefetchScalarGridSpec(
        num_scalar_prefetch=1,
        grid=(n_steps,),
        in_specs=[row(D_MODEL), row(LANES), row(PLE_DIM), pl.BlockSpec(memory_space=pl.ANY),
                  const(g_ple), const(w_pg), const(w_pp), const(g_fin)],
        out_specs=row(D_MODEL),
        scratch_shapes=[pltpu.VMEM((2, tm, D_MODEL), jnp.uint32), pltpu.SemaphoreType.DMA((2,))],
    )
    return pl.pallas_call(
        functools.partial(_final_kernel, n_steps=n_steps),
        grid_spec=grid_spec,
        out_shape=jax.ShapeDtypeStruct((t, D_MODEL), F32),
        compiler_params=_cparams("arbitrary"),
    )(pos, x1, meta, p2d, ys, g_ple, w_pg, w_pp, g_fin)


_PAIR_LO = np.array([0, 0, 0, 1, 1, 2], np.int32)
_PAIR_HI = np.array([1, 2, 3, 2, 3, 3], np.int32)


def _routing_tables(meta, counts_f, n_tiles):
    cls = meta[:, 0].astype(jnp.int32)
    rank = meta[:, 1].astype(jnp.int32)
    counts = counts_f[0, :MOE_CLASSES].astype(jnp.int32)
    tiles_per = (counts + TM_EXP - 1) // TM_EXP
    tile_end = jnp.cumsum(tiles_per)
    tile_start = tile_end - tiles_per
    class_ids = jnp.arange(MOE_CLASSES, dtype=jnp.int32)
    pos = jnp.sum(jnp.where(cls[:, None] == class_ids[None, :], (tile_start * TM_EXP)[None, :], 0), axis=1) + rank
    n_used = tile_end[-1:]
    tile_ids = jnp.arange(n_tiles, dtype=jnp.int32)
    tile_cls = jnp.minimum(jnp.sum((tile_end[None, :] <= tile_ids[:, None]).astype(jnp.int32), axis=1),
                           MOE_CLASSES - 1)
    grp = tile_cls // MOE_PAIRS
    pair = tile_cls % MOE_PAIRS
    tile_elo = grp * MOE_EPG + jnp.asarray(_PAIR_LO)[pair]
    tile_ehi = grp * MOE_EPG + jnp.asarray(_PAIR_HI)[pair]
    pos = pos.astype(jnp.int32)
    src = jnp.zeros((n_tiles * TM_EXP,), jnp.int32).at[pos].set(
        jnp.arange(pos.shape[0], dtype=jnp.int32), unique_indices=True)
    return pos, n_used.astype(jnp.int32), tile_elo.astype(jnp.int32), tile_ehi.astype(jnp.int32), src


def kernel(x, p, norm_mix_g, w_in, conv_w, conv_b, dt_bias, a_log, d_skip, ssd_norm_g, w_att_branch,
           w_ssd_branch, w_out, norm_ffn_g, w_router_group, b_router_group, w_router_expert,
           b_router_expert, w_exp_gate, w_exp_up, w_exp_down, norm_ple_g, w_ple_gate, w_ple_proj,
           final_norm_g):
    bsz, seq, _ = x.shape
    t = bsz * seq
    assert w_in.shape[0] == 1, "single-layer block"
    assert seq // ATT_PATTERNS[-1][1] == ATT_BLOCK and seq % TM_IN == 0
    x2d = x.reshape(t, D_MODEL)

    wi = w_in[0]
    c_dt = QKV_W + SSD_INNER + SSD_CONV_CH
    w_qkv = wi[:, :QKV_W].astype(BF16)
    w_wide = jnp.concatenate([wi[:, QKV_W:c_dt], wi[:, c_dt + SSD_HEADS:]], axis=1).astype(BF16)
    w_dt = jnp.pad(wi[:, c_dt:c_dt + SSD_HEADS], ((0, 0), (0, LANES - SSD_HEADS))).astype(BF16)
    row = lambda v: v.reshape(1, -1).astype(F32)

    qkv1, qkv2, qkv3, z, xbc, gates, dt_raw = _in_proj(x2d, row(norm_mix_g[0]), w_qkv, w_wide, w_dt,
                                                       conv_w[0], conv_b[0], bsz, seq)

    att = _attention(qkv1.reshape(bsz, seq, ATT_WIDTH), qkv2, qkv3, bsz, seq).reshape(t, ATT_OUT)

    dt_t = dt_raw[:, :SSD_HEADS].reshape(bsz, seq, SSD_GROUPS, SSD_HPG).transpose(0, 2, 3, 1)
    u = _ssd(xbc.reshape(bsz, seq, SSD_CONV_CH), z.reshape(bsz, seq, SSD_INNER), dt_t,
             dt_bias[0], a_log[0], d_skip[0], bsz, seq).reshape(t, SSD_INNER)

    w_r32 = jnp.pad(jnp.concatenate([w_router_group[0], w_router_expert[0]], axis=1),
                    ((0, 0), (0, LANES - MOE_GROUPS - MOE_EXPERTS))).astype(F32)
    w_r_hi = w_r32.astype(BF16)
    w_r_lo = (w_r32 - w_r_hi.astype(F32)).astype(BF16)
    w_r = jnp.concatenate([w_r_hi, w_r_hi, w_r_lo], axis=0)
    b_r = jnp.pad(jnp.concatenate([b_router_group[0], b_router_expert[0]]),
                  (0, LANES - MOE_GROUPS - MOE_EXPERTS)).reshape(1, LANES).astype(F32)
    x1, meta, counts = _post(att, u, gates, x2d,
                             w_att_branch[0].astype(BF16), w_ssd_branch[0].astype(BF16), w_out[0].astype(BF16),
                             row(ssd_norm_g[0]), row(norm_ffn_g[0]), w_r, b_r)

    n_tiles = t // TM_EXP + MOE_CLASSES
    pos, n_used, tile_elo, tile_ehi, src = _routing_tables(meta, counts, n_tiles)
    ys = _experts(n_used, tile_elo, tile_ehi, src, x1, row(norm_ffn_g[0]),
                  w_exp_gate[0].astype(BF16), w_exp_up[0].astype(BF16), w_exp_down[0].astype(BF16))
    out = _final(pos, x1, meta, p[0].reshape(t, PLE_DIM), ys, row(norm_ple_g[0]),
                 w_ple_gate[0].astype(BF16), w_ple_proj[0].astype(BF16), row(final_norm_g))
    return out.reshape(bsz, seq, D_MODEL)
```

```python
import functools

import numpy as np
import jax
import jax.numpy as jnp
from jax import lax
from jax.experimental import pallas as pl
from jax.experimental.pallas import tpu as pltpu
from jax.experimental.pallas import tpu_sc as plsc

F32 = jnp.float32
BF16 = jnp.bfloat16

D_MODEL = 1024
PLE_DIM = 256
RMS_EPS = 1e-6

ATT_PATTERNS = ((128, 1), (512, 4), (2048, 16))
ATT_GROUPS = 3
ATT_HPG = 8
ATT_HEAD_DIM = 64
ATT_WIDTH = ATT_GROUPS * ATT_HPG * ATT_HEAD_DIM
ATT_OUT = ATT_HPG * ATT_HEAD_DIM
ATT_BLOCK = 128
ATT_LOCKSTEP = 2
ALIBI_MAX_BIAS = 8.0
QKV_W = 3 * ATT_WIDTH

SSD_INNER = 2048
SSD_HEADS = 32
SSD_GROUPS = 8
SSD_HPG = 4
SSD_HEAD_DIM = 64
SSD_STATE = 128
SSD_CONV = 4
SSD_CHUNK = 128
SSD_CONV_CH = SSD_INNER + 2 * SSD_GROUPS * SSD_STATE
SSD_GW = SSD_HPG * SSD_HEAD_DIM
SSD_LOCKSTEP = 4

MOE_GROUPS = 4
MOE_EPG = 4
MOE_EXPERTS = 16
MOE_HIDDEN = 512
MOE_PAIRS = 6
MOE_CLASSES = MOE_GROUPS * MOE_PAIRS

LANES = 128
NEG_BIG = -1e30
VMEM_LIMIT = 56 * 1024 * 1024

TM_IN = 1024
TN_IN = 512
TN_WIDE = 1024
TM_POST = 512
TM_EXP = 256
TM_FIN = 512
SC_WINDOW = 128
SC_SUB = 16


def _cparams(*sem):
    return pltpu.CompilerParams(dimension_semantics=sem, vmem_limit_bytes=VMEM_LIMIT)


def _sigmoid(x):
    return 0.5 * jnp.tanh(0.5 * x) + 0.5


def _silu(x):
    h = 0.5 * x
    return h + h * jnp.tanh(h)


def _rms(x, g):
    ms = jnp.mean(x * x, axis=-1, keepdims=True)
    return x * lax.rsqrt(ms + RMS_EPS) * g


_IN_SEG = (3, 3, 3, SSD_INNER // TN_WIDE, SSD_CONV_CH // TN_WIDE, 2 * D_MODEL // TN_WIDE)
_IN_START = tuple(int(v) for v in np.cumsum((0,) + _IN_SEG))
_IN_NARROW = _IN_START[3]


def _inproj_kernel(x_ref, g_ref, wq_ref, ww_ref, wdt_ref, cw_ref, cb_ref, qkv1_ref, qkv2_ref, qkv3_ref, z_ref,
                   xbc_ref, gates_ref, dt_ref, h_ref, hcol_ref, halo_ref, *, per_seq):
    j = pl.program_id(1)
    tm = x_ref.shape[0]

    @pl.when(j == 0)
    def _():
        h = _rms(x_ref[...], g_ref[...])
        hb = h.astype(BF16)
        h_ref[0] = hb
        dt_ref[...] = jnp.dot(hb, wdt_ref[...], preferred_element_type=F32)
        ncb = hcol_ref.shape[0]
        for c in range(ncb):
            hcol_ref[c] = h[:, c * LANES:(c + 1) * LANES]
        for slot, (_, dil) in enumerate(ATT_PATTERNS[1:], start=1):
            rows = tm // dil
            for r in range(dil):
                for c in range(ncb):
                    h_ref[slot, r * rows:(r + 1) * rows, c * LANES:(c + 1) * LANES] = (
                        hcol_ref[c, pl.ds(r, rows, stride=dil), :].astype(BF16))

    def segment(k, fn):
        @pl.when((j >= _IN_START[k]) & (j < _IN_START[k + 1]))
        def _():
            fn()

    def narrow(slot, store):
        def fn():
            store(jnp.dot(h_ref[slot], wq_ref[...], preferred_element_type=F32).astype(BF16))
        return fn

    def wide(ref):
        def fn():
            ref[...] = jnp.dot(h_ref[0], ww_ref[...], preferred_element_type=F32).astype(BF16)
        return fn

    def store_plain(ref):
        def store(res):
            ref[...] = res
        return store

    def store_grouped(ref, dil):
        def store(res):
            ref[0] = res.reshape(dil, tm // dil, res.shape[1])
        return store

    def conv_silu():
        jb = j - _IN_START[4]
        nsub = 4
        rs = tm // nsub
        res = [jnp.dot(h_ref[0, k * rs:(k + 1) * rs, :], ww_ref[...], preferred_element_type=F32)
               for k in range(nsub)]
        seq_start = pl.program_id(0) % per_seq == 0
        prev = jnp.where(seq_start, 0.0, halo_ref[jb])
        row8 = lax.broadcasted_iota(jnp.int32, (8, 1), 0)
        cw = cw_ref[...]
        for k in range(nsub):
            acc = cb_ref[...] + cw[SSD_CONV - 1:SSD_CONV, :] * res[k]
            for sh in range(1, SSD_CONV):
                rolled = pltpu.roll(res[k], sh, 0)
                top = jnp.where(row8 < sh, pltpu.roll(prev, sh, 0), rolled[0:8])
                shifted = jnp.concatenate([top, rolled[8:]], axis=0)
                acc = acc + cw[SSD_CONV - 1 - sh:SSD_CONV - sh, :] * shifted
            xbc_ref[k * rs:(k + 1) * rs, :] = acc.astype(BF16)
            prev = res[k][rs - 8:rs]
        halo_ref[jb] = prev

    segment(0, narrow(0, store_plain(qkv1_ref)))
    segment(1, narrow(1, store_grouped(qkv2_ref, ATT_PATTERNS[1][1])))
    segment(2, narrow(2, store_grouped(qkv3_ref, ATT_PATTERNS[2][1])))
    segment(3, wide(z_ref))
    segment(4, conv_silu)
    segment(5, wide(gates_ref))


def _in_proj(x2d, g, w_qkv, w_wide, w_dt, conv_w, conv_b, bsz, seq):
    t = x2d.shape[0]
    nj = _IN_START[-1]
    per_seq = seq // TM_IN
    d2, d3 = ATT_PATTERNS[1][1], ATT_PATTERNS[2][1]

    def seg_map(k):
        return lambda i, j: (i, jnp.clip(j - _IN_START[k], 0, _IN_SEG[k] - 1))

    def grp_map(k):
        return lambda i, j: (i // per_seq, 0, i % per_seq, jnp.clip(j - _IN_START[k], 0, _IN_SEG[k] - 1))

    def wq_map(i, j):
        jj = jnp.minimum(j, _IN_NARROW - 1)
        return (0, (jj % 3) * ATT_GROUPS + jj // 3)

    out_shapes = (
        jax.ShapeDtypeStruct((t, ATT_WIDTH), BF16),
        jax.ShapeDtypeStruct((bsz, d2, seq // d2, ATT_WIDTH), BF16),
        jax.ShapeDtypeStruct((bsz, d3, seq // d3, ATT_WIDTH), BF16),
        jax.ShapeDtypeStruct((t, SSD_INNER), BF16),
        jax.ShapeDtypeStruct((t, SSD_CONV_CH), BF16),
        jax.ShapeDtypeStruct((t, 2 * D_MODEL), BF16),
        jax.ShapeDtypeStruct((t, LANES), F32),
    )
    out_specs = (
        pl.BlockSpec((TM_IN, TN_IN), seg_map(0)),
        pl.BlockSpec((1, d2, TM_IN // d2, TN_IN), grp_map(1)),
        pl.BlockSpec((1, d3, TM_IN // d3, TN_IN), grp_map(2)),
        pl.BlockSpec((TM_IN, TN_WIDE), seg_map(3)),
        pl.BlockSpec((TM_IN, TN_WIDE), seg_map(4)),
        pl.BlockSpec((TM_IN, TN_WIDE), seg_map(5)),
        pl.BlockSpec((TM_IN, LANES), lambda i, j: (i, 0)),
    )
    conv_map = lambda i, j: (0, jnp.clip(j - _IN_START[4], 0, _IN_SEG[4] - 1))
    return pl.pallas_call(
        functools.partial(_inproj_kernel, per_seq=per_seq),
        grid=(t // TM_IN, nj),
        in_specs=[
            pl.BlockSpec((TM_IN, D_MODEL), lambda i, j: (i, 0)),
            pl.BlockSpec((1, D_MODEL), lambda i, j: (0, 0)),
            pl.BlockSpec((D_MODEL, TN_IN), wq_map),
            pl.BlockSpec((D_MODEL, TN_WIDE), lambda i, j: (0, jnp.maximum(j - _IN_NARROW, 0))),
            pl.BlockSpec((D_MODEL, LANES), lambda i, j: (0, 0)),
            pl.BlockSpec((SSD_CONV, TN_WIDE), conv_map),
            pl.BlockSpec((1, TN_WIDE), conv_map),
        ],
        out_specs=out_specs,
        out_shape=out_shapes,
        scratch_shapes=[pltpu.VMEM((3, TM_IN, D_MODEL), BF16),
                        pltpu.VMEM((D_MODEL // LANES, TM_IN, LANES), F32),
                        pltpu.VMEM((_IN_SEG[4], 8, TN_WIDE), F32)],
        compiler_params=_cparams("arbitrary", "arbitrary"),
    )(x2d, g, w_qkv, w_wide, w_dt, conv_w.astype(F32), conv_b.reshape(1, SSD_CONV_CH).astype(F32))


def _att_bias_tables():
    h = np.arange(1, ATT_GROUPS * ATT_HPG + 1, dtype=np.float32)
    slopes = np.exp2(-ALIBI_MAX_BIAS * h / (ATT_GROUPS * ATT_HPG)).astype(np.float32).reshape(ATT_GROUPS, ATT_HPG)
    qi = np.arange(ATT_BLOCK)[:, None] + ATT_BLOCK
    kj = np.arange(2 * ATT_BLOCK)[None, :]
    delta = qi - kj
    tabs = []
    for g, (window, dil) in enumerate(ATT_PATTERNS):
        span = window // dil
        band = (delta >= 0) & (delta <= span)
        bias = (-slopes[g][:, None, None] * (delta * dil).astype(np.float32)[None]).astype(np.float32)
        with_prev = np.where(band[None], bias, np.float32(NEG_BIG))
        first = np.where((band & (kj >= ATT_BLOCK))[None], bias, np.float32(NEG_BIG))
        tabs.append(np.stack([first, with_prev]).astype(np.float32))
    return tabs[0], tabs[1], tabs[2][1][:, :, ATT_BLOCK:]


def _att_units(units):
    lane = lax.broadcasted_iota(jnp.int32, (1, LANES), 1)
    low = lane < ATT_HEAD_DIM
    scale = ATT_HEAD_DIM ** -0.5
    qmask = (jnp.where(low, scale, 0.0).astype(BF16), jnp.where(low, 0.0, scale).astype(BF16))
    heads = [(q, k2, v2, tab_fn, hh) for q, k2, v2, tab_fn in units for hh in range(2)]
    scores = [lax.dot_general(q * qmask[hh], k2, (((1,), (1,)), ((), ())), preferred_element_type=F32)
              for q, k2, _, _, hh in heads]
    probs = []
    for s, (_, _, _, tab_fn, hh) in zip(scores, heads):
        s = s + tab_fn(hh)
        m = jnp.max(s, axis=-1, keepdims=True)
        e = jnp.exp(s - m)
        probs.append((e.astype(BF16), m, jnp.sum(e, axis=-1, keepdims=True)))
    pvs = [jnp.dot(e, v2, preferred_element_type=F32) for (e, _, _), (_, _, v2, _, _) in zip(probs, heads)]
    outs = [pv / den for pv, (_, _, den) in zip(pvs, probs)]
    lses = [m + jnp.log(den) for _, m, den in probs]
    return [(jnp.where(low, outs[2 * i], outs[2 * i + 1]), jnp.where(low, lses[2 * i], lses[2 * i + 1]))
            for i in range(len(units))]


def _att_kernel(q1, k1, v1, q2, k2, v2, q3, k3, v3, t1, t2, t3, out_ref,
                o1, l1, o2p, l2p, o2n, l2n, o3p, l3p, o3n, l3n, *, seq):
    blk = ATT_BLOCK
    d2, d3 = ATT_PATTERNS[1][1], ATT_PATTERNS[2][1]
    nb1, nb2 = seq // blk, seq // d2 // blk

    def rows(n):
        return pl.ds(pl.multiple_of(n * blk, blk), blk)

    assert nb1 == d2 * nb2 == d3

    def unit_body(i, _):
        units, dests = [], []
        for k in range(ATT_LOCKSTEP):
            u = i * ATT_LOCKSTEP + k
            cur, prv = rows(u), rows(jnp.maximum(u - 1, 0))
            sel = jnp.minimum(u, 1)
            units.append((q1[0, cur, :], jnp.concatenate([k1[0, prv, :], k1[0, cur, :]], axis=0),
                          jnp.concatenate([v1[0, prv, :], v1[0, cur, :]], axis=0),
                          lambda hh, sel=sel: t1[sel, hh]))
            r, n = u // nb2, u % nb2
            cur2, prv2 = rows(n), rows(jnp.maximum(n - 1, 0))
            sel2 = jnp.minimum(n, 1)
            units.append((q2[0, r, cur2, :], jnp.concatenate([k2[0, r, prv2, :], k2[0, r, cur2, :]], axis=0),
                          jnp.concatenate([v2[0, r, prv2, :], v2[0, r, cur2, :]], axis=0),
                          lambda hh, sel2=sel2: t2[sel2, hh]))
            units.append((q3[0, u], k3[0, u], v3[0, u], lambda hh: t3[hh]))
            dests += [(o1, l1, cur), (o2p, l2p, cur), (o3p, l3p, cur)]
        for (o, l), (o_ref, l_ref, where) in zip(_att_units(units), dests):
            o_ref[where, :] = o
            l_ref[where, :] = l
        return 0

    lax.fori_loop(0, nb1 // ATT_LOCKSTEP, unit_body, 0)

    for dil, pairs in ((d2, ((o2p, o2n), (l2p, l2n))), (d3, ((o3p, o3n), (l3p, l3n)))):
        n_sub = seq // dil
        for r in range(dil):
            for src, dst in pairs:
                dst[pl.ds(r, n_sub, stride=dil), :] = src[r * n_sub:(r + 1) * n_sub, :]

    mrows = 2 * blk

    def merge(c, _):
        rr = pl.ds(pl.multiple_of(c * mrows, mrows), mrows)
        la, lb, lc = l1[rr, :], l2n[rr, :], l3n[rr, :]
        lm = jnp.maximum(jnp.maximum(la, lb), lc)
        ea, eb, ec = jnp.exp(la - lm), jnp.exp(lb - lm), jnp.exp(lc - lm)
        att = (ea * o1[rr, :] + eb * o2n[rr, :] + ec * o3n[rr, :]) / (ea + eb + ec)
        out_ref[0, rr, :] = att.astype(BF16)
        return 0

    lax.fori_loop(0, seq // mrows, merge, 0)


def _attention(qkv1, qkv2, qkv3, bsz, seq):
    t1, t2, t3 = (jnp.asarray(t) for t in _att_bias_tables())
    d2, d3 = ATT_PATTERNS[1][1], ATT_PATTERNS[2][1]
    npair = ATT_HPG // 2
    in_arrays, in_specs = [], []
    for arr, lead in ((qkv1, ()), (qkv2, (d2,)), (qkv3, (d3,))):
        n_rows = arr.shape[-2]
        for sel in range(3):
            zeros = (0,) * len(lead)
            in_arrays.append(arr)
            in_specs.append(pl.BlockSpec((1,) + lead + (n_rows, LANES),
                                         lambda hp, b, sel=sel, zeros=zeros: (b,) + zeros + (0, sel * npair + hp)))
    in_arrays += [t1, t2, t3]
    in_specs += [pl.BlockSpec((2, 2, ATT_BLOCK, 2 * ATT_BLOCK), lambda hp, b: (0, hp, 0, 0)),
                 pl.BlockSpec((2, 2, ATT_BLOCK, 2 * ATT_BLOCK), lambda hp, b: (0, hp, 0, 0)),
                 pl.BlockSpec((2, ATT_BLOCK, ATT_BLOCK), lambda hp, b: (hp, 0, 0))]
    return pl.pallas_call(
        functools.partial(_att_kernel, seq=seq),
        grid=(npair, bsz),
        in_specs=in_specs,
        out_specs=pl.BlockSpec((1, seq, LANES), lambda hp, b: (b, 0, hp)),
        out_shape=jax.ShapeDtypeStruct((bsz, seq, ATT_OUT), BF16),
        scratch_shapes=[pltpu.VMEM((seq, LANES), F32)] * 10,
        compiler_params=_cparams("arbitrary", "arbitrary"),
    )(*in_arrays)


def _softplus(x):
    return jnp.maximum(x, 0.0) + jnp.log1p(jnp.exp(-jnp.abs(x)))


def _head_spread_table():
    gw, n_src = SSD_GW, LANES
    tab = np.zeros((2 * n_src, 2 * gw), np.float32)
    for half in range(2):
        for blk, src0 in enumerate((4, 8)):
            for col in range(gw):
                tab[half * n_src + src0 + col // SSD_HEAD_DIM, blk * gw + col] = 1.0
    return tab


def _ssd_kernel(x_ref, b_ref, c_ref, z_ref, dt_ref, spread_ref, dtb_ref, alog_ref, dsk_ref, u_ref, carry_ref,
                *, n_chunks):
    L = SSD_CHUNK
    gw = SSD_GW
    carry_ref[...] = jnp.zeros(carry_ref.shape, F32)

    a_neg = -jnp.exp(alog_ref[0])
    dtb = dtb_ref[0]
    dsk = dsk_ref[0]
    ri = lax.broadcasted_iota(jnp.int32, (L, L), 0)
    ci = lax.broadcasted_iota(jnp.int32, (L, L), 1)
    upper_incl = (ri <= ci).astype(F32)
    causal = ri >= ci
    lane = lax.broadcasted_iota(jnp.int32, (1, LANES), 1)
    low = lane < SSD_HEAD_DIM
    heads = range(SSD_HPG)

    def load(s):
        rows = pl.ds(s["r0"], L)
        s["xs"] = _silu(x_ref[0, rows, :].astype(F32))
        s["xs_b"] = s["xs"].astype(BF16)
        s["bm"] = _silu(b_ref[0, rows, :].astype(F32))
        s["cm_b"] = _silu(c_ref[0, rows, :].astype(F32)).astype(BF16)

    def decay_cumsum(s):
        dt = _softplus(dt_ref[0, 0, :, pl.ds(s["r0"], L)] + dtb)
        rows8 = jnp.concatenate([dt * a_neg, dt], axis=0)
        cs8 = jnp.dot(rows8, upper_incl, preferred_element_type=F32, precision=lax.Precision.HIGHEST)
        s["dt"], s["acs_t"] = dt, cs8[0:SSD_HPG]

    def decay_spread(s):
        dt, acs_t = s["dt"], s["acs_t"]
        t16 = jnp.concatenate([acs_t, acs_t, acs_t, dt], axis=0)
        cols = jnp.concatenate([t16, jnp.zeros((L - 16, L), F32)], axis=0).T
        dt_at8 = pltpu.roll(cols, LANES - 4, 1)
        comb = jnp.where(lane < 8, jnp.exp(cols), jnp.exp(cols[L - 1:L, :] - cols) * dt_at8)
        hi = comb.astype(BF16)
        lo = (comb - hi.astype(F32)).astype(BF16)
        s["spread"] = jnp.dot(jnp.concatenate([hi, lo], axis=1), spread_ref[...], preferred_element_type=F32)
        s["acs_cols"] = cols

    def scores(s):
        s["bm_t"] = s.pop("bm").T.astype(BF16)
        s["cb"] = jnp.dot(s["cm_b"], s["bm_t"], preferred_element_type=F32)

    def intra(s):
        spread, acs_t, dt, cb, cols = s["spread"], s.pop("acs_t"), s.pop("dt"), s.pop("cb"), s.pop("acs_cols")
        xs, xs_b = s["xs"], s.pop("xs_b")
        mixes = []
        for j in heads:
            seg = cols[:, j:j + 1] - acs_t[j:j + 1, :]
            mixes.append((jnp.exp(jnp.where(causal, seg, -jnp.inf)) * (cb * dt[j:j + 1, :])).astype(BF16))
        halves = []
        for hp in range(SSD_HPG // 2):
            yy = jnp.dot(jnp.concatenate(mixes[2 * hp:2 * hp + 2], axis=0), xs_b[:, hp * LANES:(hp + 1) * LANES],
                         preferred_element_type=F32)
            halves.append(jnp.where(low, yy[:L], yy[L:]))
        s["y"] = jnp.concatenate(halves, axis=1)
        s["st_new"] = jnp.dot(s.pop("bm_t"), (xs * spread[:, gw:]).astype(BF16),
                              preferred_element_type=F32)

    def inter(s):
        eacs_bc = s.pop("spread")[:, :gw]
        carry = carry_ref[...]
        y = s.pop("y") + jnp.dot(s.pop("cm_b"), carry.astype(BF16), preferred_element_type=F32) * eacs_bc
        carry_ref[...] = carry * eacs_bc[L - 1:L, :] + s.pop("st_new")
        y = y + s.pop("xs") * dsk
        rows = pl.ds(s["r0"], L)
        u_ref[0, rows, :] = (y * _silu(z_ref[0, rows, :].astype(F32))).astype(BF16)

    def body(i, _):
        states = [{"c": i * SSD_LOCKSTEP + k, "r0": pl.multiple_of((i * SSD_LOCKSTEP + k) * L, L)}
                  for k in range(SSD_LOCKSTEP)]
        for stage in (load, decay_cumsum, decay_spread, scores, intra, inter):
            for s in states:
                stage(s)
        return 0

    lax.fori_loop(0, n_chunks // SSD_LOCKSTEP, body, 0)


def _ssd(xbc, z, dt_t, dt_bias, a_log, d_skip, bsz, seq):
    gw = SSD_GW
    nxb = SSD_INNER // SSD_STATE
    dsk = jnp.repeat(d_skip.astype(F32), SSD_HEAD_DIM).reshape(SSD_GROUPS, 1, gw)
    dtb = dt_bias.astype(F32).reshape(SSD_GROUPS, SSD_HPG, 1)
    alog = a_log.astype(F32).reshape(SSD_GROUPS, SSD_HPG, 1)
    spread = jnp.asarray(_head_spread_table(), BF16)
    x_map = lambda b, g: (b, 0, g)
    bm_map = lambda b, g: (b, 0, nxb + g)
    cm_map = lambda b, g: (b, 0, nxb + SSD_GROUPS + g)
    return pl.pallas_call(
        functools.partial(_ssd_kernel, n_chunks=seq // SSD_CHUNK),
        grid=(bsz, SSD_GROUPS),
        in_specs=[
            pl.BlockSpec((1, seq, gw), x_map),
            pl.BlockSpec((1, seq, SSD_STATE), bm_map),
            pl.BlockSpec((1, seq, SSD_STATE), cm_map),
            pl.BlockSpec((1, seq, gw), x_map),
            pl.BlockSpec((1, 1, SSD_HPG, seq), lambda b, g: (b, g, 0, 0)),
            pl.BlockSpec(spread.shape, lambda b, g: (0, 0)),
            pl.BlockSpec((1, SSD_HPG, 1), lambda b, g: (g, 0, 0)),
            pl.BlockSpec((1, SSD_HPG, 1), lambda b, g: (g, 0, 0)),
            pl.BlockSpec((1, 1, gw), lambda b, g: (g, 0, 0)),
        ],
        out_specs=pl.BlockSpec((1, seq, gw), x_map),
        out_shape=jax.ShapeDtypeStruct((bsz, seq, SSD_INNER), BF16),
        scratch_shapes=[pltpu.VMEM((SSD_STATE, gw), F32)],
        compiler_params=_cparams("arbitrary", "arbitrary"),
    )(xbc, xbc, xbc, z, dt_t, spread, dtb, alog, dsk)


def _first_index_of_max(vals, lane_f):
    m = jnp.max(vals, axis=-1, keepdims=True)
    idx = jnp.min(jnp.where(vals == m, lane_f, float(LANES)), axis=-1, keepdims=True)
    return m, idx


def _post_kernel(att_ref, u_ref, gates_ref, x_ref, watt_ref, wssd_ref, wout_ref,
                 gssd_ref, gffn_ref, wr_ref, br_ref, x1_ref, meta_ref, cnt_ref, run_ref):
    i = pl.program_id(0)

    @pl.when(i == 0)
    def _():
        run_ref[...] = jnp.zeros(run_ref.shape, F32)

    tm = x_ref.shape[0]
    nsub = 2
    subs = [{"rows": pl.ds(k * (tm // nsub), tm // nsub)} for k in range(nsub)]

    def att_branch(s):
        s["y_att"] = jnp.dot(att_ref[s["rows"], :], watt_ref[...], preferred_element_type=F32)

    def ssd_branch(s):
        ssd = _rms(u_ref[s["rows"], :].astype(F32), gssd_ref[...])
        s["y_ssd"] = jnp.dot(ssd.astype(BF16), wssd_ref[...], preferred_element_type=F32)

    def mix_out(s):
        gates = gates_ref[s["rows"], :].astype(F32)
        merged = _sigmoid(gates[:, :D_MODEL]) * s.pop("y_att") + _sigmoid(gates[:, D_MODEL:]) * s.pop("y_ssd")
        x1 = x_ref[s["rows"], :] + jnp.dot(merged.astype(BF16), wout_ref[...], preferred_element_type=F32)
        x1_ref[s["rows"], :] = x1
        s["h2"] = _rms(x1, gffn_ref[...])

    def router(s):
        h2 = s.pop("h2")
        hi = h2.astype(BF16)
        lo = (h2 - hi.astype(F32)).astype(BF16)
        s["logits"] = jnp.dot(jnp.concatenate([hi, lo, hi], axis=1), wr_ref[...], preferred_element_type=F32)

    for stage in (att_branch, ssd_branch, mix_out, router):
        for s in subs:
            stage(s)
    logits = jnp.concatenate([s["logits"] for s in subs], axis=0) + br_ref[...]
    lane = lax.broadcasted_iota(jnp.int32, (tm, LANES), 1)
    lane_f = lane.astype(F32)
    ninf = -jnp.inf
    gl = jnp.where(lane < MOE_GROUPS, logits, ninf)
    gmax, gidx = _first_index_of_max(gl, lane_f)
    g_val = 1.0 / jnp.sum(jnp.exp(gl - gmax), axis=-1, keepdims=True)
    base = MOE_GROUPS + MOE_EPG * gidx
    el = jnp.where((lane_f >= base) & (lane_f < base + MOE_EPG), logits, ninf)
    e1, i1 = _first_index_of_max(el, lane_f)
    e2, i2 = _first_index_of_max(jnp.where(lane_f == i1, ninf, el), lane_f)
    t2 = jnp.exp(e2 - e1)
    w1 = g_val / (1.0 + t2)
    w2 = g_val * t2 / (1.0 + t2)
    a1, a2 = i1 - base, i2 - base
    lo, hi = jnp.minimum(a1, a2), jnp.maximum(a1, a2)
    c_lo = jnp.where(a1 < a2, w1, w2)
    c_hi = jnp.where(a1 < a2, w2, w1)
    pair = lo * (7.0 - lo) * 0.5 + (hi - lo - 1.0)
    cls = gidx * MOE_PAIRS + pair

    onehot = (lane_f == cls)
    oh_b = jnp.where(onehot, 1.0, 0.0).astype(BF16)
    rr = lax.broadcasted_iota(jnp.int32, (tm, tm), 0)
    cc = lax.broadcasted_iota(jnp.int32, (tm, tm), 1)
    strict = jnp.where(rr > cc, 1.0, 0.0).astype(BF16)
    prefix = jnp.dot(strict, oh_b, preferred_element_type=F32) + run_ref[...]
    rank = jnp.sum(jnp.where(onehot, prefix, 0.0), axis=-1, keepdims=True)
    run = run_ref[...] + jnp.sum(oh_b.astype(F32), axis=0, keepdims=True)
    run_ref[...] = run
    cnt_ref[...] = jnp.broadcast_to(run, cnt_ref.shape)

    meta = jnp.where(lane == 0, cls, jnp.where(lane == 1, rank, jnp.where(lane == 2, c_lo,
                     jnp.where(lane == 3, c_hi, 0.0))))
    meta_ref[...] = meta


def _post(att, u, gates, x2d, w_att, w_ssd, w_out, g_ssd, g_ffn, w_r, b_r):
    t = x2d.shape[0]
    tm = TM_POST
    row = lambda w: pl.BlockSpec((tm, w), lambda i: (i, 0))
    const = lambda a: pl.BlockSpec(a.shape, lambda i: (0,) * a.ndim)
    return pl.pallas_call(
        _post_kernel,
        grid=(t // tm,),
        in_specs=[row(ATT_OUT), row(SSD_INNER), row(2 * D_MODEL), row(D_MODEL),
                  const(w_att), const(w_ssd), const(w_out), const(g_ssd), const(g_ffn), const(w_r), const(b_r)],
        out_specs=(row(D_MODEL), row(LANES), pl.BlockSpec((8, LANES), lambda i: (0, 0))),
        out_shape=(jax.ShapeDtypeStruct((t, D_MODEL), F32),
                   jax.ShapeDtypeStruct((t, LANES), F32),
                   jax.ShapeDtypeStruct((8, LANES), F32)),
        scratch_shapes=[pltpu.VMEM((1, LANES), F32)],
        compiler_params=_cparams("arbitrary"),
    )(att, u, gates, x2d, w_att, w_ssd, w_out, g_ssd, g_ffn, w_r, b_r)


def _pack_pair(a, b):
    ua = lax.bitcast_convert_type(a.astype(BF16).astype(F32), jnp.uint32)
    ub = lax.bitcast_convert_type(b.astype(BF16).astype(F32), jnp.uint32)
    return (ua & jnp.uint32(0xFFFF0000)) | (ub >> 16)


def _unpack_pair(p):
    a = lax.bitcast_convert_type(p & jnp.uint32(0xFFFF0000), F32)
    b = lax.bitcast_convert_type(p << 16, F32)
    return a, b


def _sc_mesh():
    return plsc.VectorSubcoreMesh(core_axis_name="c", subcore_axis_name="s")


def _sc_permute_rows(data, idx, n_out, scatter):
    n, width = idx.shape[0], data.shape[1]
    mesh = _sc_mesh()
    workers = mesh.num_cores * mesh.num_subcores
    per_worker = n // SC_WINDOW // workers
    assert per_worker * workers * SC_WINDOW == n

    @pl.kernel(out_type=jax.ShapeDtypeStruct((n_out, width), data.dtype), mesh=mesh,
               scratch_types=[pltpu.VMEM((1, SC_WINDOW), jnp.int32), pltpu.VMEM((SC_SUB, width), data.dtype)])
    def permute(data_hbm, idx_hbm, out_hbm, idx_v, rows_v):
        worker = lax.axis_index("c") * mesh.num_subcores + lax.axis_index("s")

        @pl.loop(0, per_worker)
        def _(j):
            base = (worker * per_worker + j) * SC_WINDOW
            pltpu.sync_copy(idx_hbm.at[:, pl.ds(base, SC_WINDOW)], idx_v)
            for k in range(SC_WINDOW // SC_SUB):
                sub_idx = idx_v.at[0, pl.ds(k * SC_SUB, SC_SUB)]
                plain = pl.ds(base + k * SC_SUB, SC_SUB)
                if scatter:
                    pltpu.sync_copy(data_hbm.at[plain], rows_v)
                    pltpu.sync_copy(rows_v, out_hbm.at[sub_idx])
                else:
                    pltpu.sync_copy(data_hbm.at[sub_idx], rows_v)
                    pltpu.sync_copy(rows_v, out_hbm.at[plain])

    return permute(data, idx.reshape(1, n))


def _sc_scatter_rows(data, idx, n_out):
    return _sc_permute_rows(data, idx, n_out, scatter=True)


def _sc_gather_rows(data, idx):
    return _sc_permute_rows(data, idx, idx.shape[0], scatter=False)


def _expert_kernel(nused_ref, elo_ref, ehi_ref, valid_ref, xs_ref, g_ref, wg_lo, wu_lo, wd_lo, wg_hi, wu_hi, wd_hi,
                   ys_ref):
    del elo_ref, ehi_ref
    i = pl.program_id(0)

    @pl.when(i < nused_ref[0])
    def _():
        row = lax.broadcasted_iota(jnp.int32, (xs_ref.shape[0], 1), 0)
        xb = _rms(jnp.where(row < valid_ref[i], xs_ref[...], 0.0), g_ref[...]).astype(BF16)
        pre = [(jnp.dot(xb, wg[0], preferred_element_type=F32), jnp.dot(xb, wu[0], preferred_element_type=F32))
               for wg, wu in ((wg_lo, wu_lo), (wg_hi, wu_hi))]
        hid = [(_silu(gt) * up).astype(BF16) for gt, up in pre]
        y_lo, y_hi = (jnp.dot(h, wd[0], preferred_element_type=F32) for h, wd in zip(hid, (wd_lo, wd_hi)))
        ys_ref[...] = _pack_pair(y_lo, y_hi)

    @pl.when(i >= nused_ref[0])
    def _():
        ys_ref[...] = jnp.zeros(ys_ref.shape, jnp.uint32)


def _experts(n_used, tile_elo, tile_ehi, tile_valid, xs, g_ffn, w_gate, w_up, w_down):
    p_rows = xs.shape[0]
    tm = TM_EXP
    n_tiles = p_rows // tm

    def last_used(i, nu):
        return jnp.minimum(i, jnp.maximum(nu[0] - 1, 0))

    def row_map(i, nu, elo, ehi, valid):
        return (last_used(i, nu), 0)

    def wmap(which):
        def f(i, nu, elo, ehi, valid):
            return ((elo, ehi)[which][last_used(i, nu)], 0, 0)
        return f

    wspec_in = lambda which: pl.BlockSpec((1, D_MODEL, MOE_HIDDEN), wmap(which))
    wspec_out = lambda which: pl.BlockSpec((1, MOE_HIDDEN, D_MODEL), wmap(which))
    grid_spec = pltpu.PrefetchScalarGridSpec(
        num_scalar_prefetch=4,
        grid=(n_tiles,),
        in_specs=[pl.BlockSpec((tm, D_MODEL), row_map),
                  pl.BlockSpec((1, D_MODEL), lambda i, nu, elo, ehi, valid: (0, 0)),
                  wspec_in(0), wspec_in(0), wspec_out(0), wspec_in(1), wspec_in(1), wspec_out(1)],
        out_specs=pl.BlockSpec((tm, D_MODEL), lambda i, nu, elo, ehi, valid: (i, 0)),
    )
    return pl.pallas_call(
        _expert_kernel,
        grid_spec=grid_spec,
        out_shape=jax.ShapeDtypeStruct((p_rows, D_MODEL), jnp.uint32),
        compiler_params=_cparams("arbitrary"),
    )(n_used, tile_elo, tile_ehi, tile_valid, xs, g_ffn, w_gate, w_up, w_down, w_gate, w_up, w_down)


def _final_kernel(x1_ref, meta_ref, p_ref, yt_ref, gple_ref, wpg_ref, wpp_ref, gfin_ref, out_ref):
    tm = x1_ref.shape[0]
    nsub = 2
    subs = [{"rows": pl.ds(k * (tm // nsub), tm // nsub)} for k in range(nsub)]

    def embed(s):
        s["pp"] = jnp.dot(p_ref[s["rows"], :].astype(BF16), wpp_ref[...], preferred_element_type=F32)

    def combine(s):
        y_lo, y_hi = _unpack_pair(yt_ref[s["rows"], :])
        meta = meta_ref[s["rows"], :]
        s["x2"] = x1_ref[s["rows"], :] + meta[:, 2:3] * y_lo + meta[:, 3:4] * y_hi
        s["hn"] = _rms(s["x2"], gple_ref[...]).astype(BF16)

    def gate(s):
        s["gate"] = jnp.dot(s.pop("hn"), wpg_ref[...], preferred_element_type=F32)

    def finish(s):
        out_ref[s["rows"], :] = _rms(s.pop("x2") + _sigmoid(s.pop("gate")) * s.pop("pp"), gfin_ref[...])

    for stage in (embed, combine, gate, finish):
        for s in subs:
            stage(s)


def _final(x1, meta, p2d, ys_tok, g_ple, w_pg, w_pp, g_fin):
    t = x1.shape[0]
    tm = TM_FIN
    row = lambda w: pl.BlockSpec((tm, w), lambda i: (i, 0))
    const = lambda a: pl.BlockSpec(a.shape, lambda i: (0,) * a.ndim)
    return pl.pallas_call(
        _final_kernel,
        grid=(t // tm,),
        in_specs=[row(D_MODEL), row(LANES), row(PLE_DIM), row(D_MODEL),
                  const(g_ple), const(w_pg), const(w_pp), const(g_fin)],
        out_specs=row(D_MODEL),
        out_shape=jax.ShapeDtypeStruct((t, D_MODEL), F32),
        compiler_params=_cparams("arbitrary"),
    )(x1, meta, p2d, ys_tok, g_ple, w_pg, w_pp, g_fin)


_PAIR_LO = np.array([0, 0, 0, 1, 1, 2], np.int32)
_PAIR_HI = np.array([1, 2, 3, 2, 3, 3], np.int32)


def _routing_tables(meta, counts_f, n_tiles):
    cls = meta[:, 0].astype(jnp.int32)
    rank = meta[:, 1].astype(jnp.int32)
    counts = counts_f[0, :MOE_CLASSES].astype(jnp.int32)
    tiles_per = (counts + TM_EXP - 1) // TM_EXP
    tile_end = jnp.cumsum(tiles_per)
    tile_start = tile_end - tiles_per
    class_ids = jnp.arange(MOE_CLASSES, dtype=jnp.int32)
    pos = jnp.sum(jnp.where(cls[:, None] == class_ids[None, :], (tile_start * TM_EXP)[None, :], 0), axis=1) + rank
    n_used = tile_end[-1:]
    tile_ids = jnp.arange(n_tiles, dtype=jnp.int32)
    tile_cls = jnp.minimum(jnp.sum((tile_end[None, :] <= tile_ids[:, None]).astype(jnp.int32), axis=1),
                           MOE_CLASSES - 1)
    grp = tile_cls // MOE_PAIRS
    pair = tile_cls % MOE_PAIRS
    tile_elo = grp * MOE_EPG + jnp.asarray(_PAIR_LO)[pair]
    tile_ehi = grp * MOE_EPG + jnp.asarray(_PAIR_HI)[pair]
    class_left = jnp.sum(jnp.where(tile_cls[:, None] == class_ids[None, :],
                                   (counts - (tile_ids[:, None] - tile_start[None, :]) * TM_EXP), 0), axis=1)
    tile_valid = jnp.clip(class_left, 0, TM_EXP)
    return (pos.astype(jnp.int32), n_used.astype(jnp.int32), tile_elo.astype(jnp.int32),
            tile_ehi.astype(jnp.int32), tile_valid.astype(jnp.int32))


def kernel(x, p, norm_mix_g, w_in, conv_w, conv_b, dt_bias, a_log, d_skip, ssd_norm_g, w_att_branch,
           w_ssd_branch, w_out, norm_ffn_g, w_router_group, b_router_group, w_router_expert,
           b_router_expert, w_exp_gate, w_exp_up, w_exp_down, norm_ple_g, w_ple_gate, w_ple_proj,
           final_norm_g):
    bsz, seq, _ = x.shape
    t = bsz * seq
    assert w_in.shape[0] == 1, "single-layer block"
    assert seq // ATT_PATTERNS[-1][1] == ATT_BLOCK and seq % TM_IN == 0
    x2d = x.reshape(t, D_MODEL)

    wi = w_in[0]
    c_dt = QKV_W + SSD_INNER + SSD_CONV_CH
    w_qkv = wi[:, :QKV_W].astype(BF16)
    w_wide = jnp.concatenate([wi[:, QKV_W:c_dt], wi[:, c_dt + SSD_HEADS:]], axis=1).astype(BF16)
    w_dt = jnp.pad(wi[:, c_dt:c_dt + SSD_HEADS], ((0, 0), (0, LANES - SSD_HEADS))).astype(BF16)
    row = lambda v: v.reshape(1, -1).astype(F32)

    qkv1, qkv2, qkv3, z, xbc, gates, dt_raw = _in_proj(x2d, row(norm_mix_g[0]), w_qkv, w_wide, w_dt,
                                                       conv_w[0], conv_b[0], bsz, seq)

    att = _attention(qkv1.reshape(bsz, seq, ATT_WIDTH), qkv2, qkv3, bsz, seq).reshape(t, ATT_OUT)

    dt_t = dt_raw[:, :SSD_HEADS].reshape(bsz, seq, SSD_GROUPS, SSD_HPG).transpose(0, 2, 3, 1)
    u = _ssd(xbc.reshape(bsz, seq, SSD_CONV_CH), z.reshape(bsz, seq, SSD_INNER), dt_t,
             dt_bias[0], a_log[0], d_skip[0], bsz, seq).reshape(t, SSD_INNER)

    w_r32 = jnp.pad(jnp.concatenate([w_router_group[0], w_router_expert[0]], axis=1),
                    ((0, 0), (0, LANES - MOE_GROUPS - MOE_EXPERTS))).astype(F32)
    w_r_hi = w_r32.astype(BF16)
    w_r_lo = (w_r32 - w_r_hi.astype(F32)).astype(BF16)
    w_r = jnp.concatenate([w_r_hi, w_r_hi, w_r_lo], axis=0)
    b_r = jnp.pad(jnp.concatenate([b_router_group[0], b_router_expert[0]]),
                  (0, LANES - MOE_GROUPS - MOE_EXPERTS)).reshape(1, LANES).astype(F32)
    x1, meta, counts = _post(att, u, gates, x2d,
                             w_att_branch[0].astype(BF16), w_ssd_branch[0].astype(BF16), w_out[0].astype(BF16),
                             row(ssd_norm_g[0]), row(norm_ffn_g[0]), w_r, b_r)

    n_tiles = t // TM_EXP + MOE_CLASSES
    pos, n_used, tile_elo, tile_ehi, tile_valid = _routing_tables(meta, counts, n_tiles)
    xs = _sc_scatter_rows(x1, pos, n_tiles * TM_EXP)
    ys = _experts(n_used, tile_elo, tile_ehi, tile_valid, xs, row(norm_ffn_g[0]),
                  w_exp_gate[0].astype(BF16), w_exp_up[0].astype(BF16), w_exp_down[0].astype(BF16))
    ys_tok = _sc_gather_rows(ys, pos)
    out = _final(x1, meta, p[0].reshape(t, PLE_DIM), ys_tok, row(norm_ple_g[0]),
                 w_ple_gate[0].astype(BF16), w_ple_proj[0].astype(BF16), row(final_norm_g))
    return out.reshape(bsz, seq, D_MODEL)
```

```python
import functools

import numpy as np
import jax
import jax.numpy as jnp
from jax import lax
from jax.experimental import pallas as pl
from jax.experimental.pallas import tpu as pltpu
from jax.experimental.pallas import tpu_sc as plsc

F32 = jnp.float32
BF16 = jnp.bfloat16

D_MODEL = 1024
PLE_DIM = 256
RMS_EPS = 1e-6

ATT_PATTERNS = ((128, 1), (512, 4), (2048, 16))
ATT_GROUPS = 3
ATT_HPG = 8
ATT_HEAD_DIM = 64
ATT_WIDTH = ATT_GROUPS * ATT_HPG * ATT_HEAD_DIM
ATT_OUT = ATT_HPG * ATT_HEAD_DIM
ATT_BLOCK = 128
ATT_LOCKSTEP = 2
ALIBI_MAX_BIAS = 8.0
QKV_W = 3 * ATT_WIDTH

SSD_INNER = 2048
SSD_HEADS = 32
SSD_GROUPS = 8
SSD_HPG = 4
SSD_HEAD_DIM = 64
SSD_STATE = 128
SSD_CONV = 4
SSD_CHUNK = 128
SSD_CONV_CH = SSD_INNER + 2 * SSD_GROUPS * SSD_STATE
SSD_GW = SSD_HPG * SSD_HEAD_DIM
SSD_LOCKSTEP = 4

MOE_GROUPS = 4
MOE_EPG = 4
MOE_EXPERTS = 16
MOE_HIDDEN = 512
MOE_PAIRS = 6
MOE_CLASSES = MOE_GROUPS * MOE_PAIRS

LANES = 128
NEG_BIG = -1e30
VMEM_LIMIT = 56 * 1024 * 1024

TM_IN = 1024
TN_IN = 512
TN_WIDE = 1024
TM_POST = 512
TM_EXP = 256
TM_FIN = 512
SC_WINDOW = 128
SC_SUB = 16


def _cparams(*sem):
    return pltpu.CompilerParams(dimension_semantics=sem, vmem_limit_bytes=VMEM_LIMIT)


def _sigmoid(x):
    return 0.5 * jnp.tanh(0.5 * x) + 0.5


def _silu(x):
    h = 0.5 * x
    return h + h * jnp.tanh(h)


def _rms(x, g):
    ms = jnp.mean(x * x, axis=-1, keepdims=True)
    return x * lax.rsqrt(ms + RMS_EPS) * g


_IN_SEG = (3, 3, 3, SSD_INNER // TN_WIDE, SSD_CONV_CH // TN_WIDE, 2 * D_MODEL // TN_WIDE)
_IN_START = tuple(int(v) for v in np.cumsum((0,) + _IN_SEG))
_IN_NARROW = _IN_START[3]


def _inproj_kernel(x_ref, g_ref, wq_ref, ww_ref, wdt_ref, cw_ref, cb_ref, qkv1_ref, qkv2_ref, qkv3_ref, z_ref,
                   xbc_ref, gates_ref, dt_ref, h_ref, hcol_ref, halo_ref, *, per_seq):
    j = pl.program_id(1)
    tm = x_ref.shape[0]

    @pl.when(j == 0)
    def _():
        h = _rms(x_ref[...], g_ref[...])
        hb = h.astype(BF16)
        h_ref[0] = hb
        dt_ref[...] = jnp.dot(hb, wdt_ref[...], preferred_element_type=F32)
        ncb = hcol_ref.shape[0]
        for c in range(ncb):
            hcol_ref[c] = h[:, c * LANES:(c + 1) * LANES]
        for slot, (_, dil) in enumerate(ATT_PATTERNS[1:], start=1):
            rows = tm // dil
            for r in range(dil):
                for c in range(ncb):
                    h_ref[slot, r * rows:(r + 1) * rows, c * LANES:(c + 1) * LANES] = (
                        hcol_ref[c, pl.ds(r, rows, stride=dil), :].astype(BF16))

    def segment(k, fn):
        @pl.when((j >= _IN_START[k]) & (j < _IN_START[k + 1]))
        def _():
            fn()

    def narrow(slot, store):
        def fn():
            store(jnp.dot(h_ref[slot], wq_ref[...], preferred_element_type=F32).astype(BF16))
        return fn

    def wide(ref):
        def fn():
            ref[...] = jnp.dot(h_ref[0], ww_ref[...], preferred_element_type=F32).astype(BF16)
        return fn

    def store_plain(ref):
        def store(res):
            ref[...] = res
        return store

    def store_grouped(ref, dil):
        def store(res):
            ref[0] = res.reshape(dil, tm // dil, res.shape[1])
        return store

    def conv_silu():
        jb = j - _IN_START[4]
        nsub = 4
        rs = tm // nsub
        res = [jnp.dot(h_ref[0, k * rs:(k + 1) * rs, :], ww_ref[...], preferred_element_type=F32)
               for k in range(nsub)]
        seq_start = pl.program_id(0) % per_seq == 0
        prev = jnp.where(seq_start, 0.0, halo_ref[jb])
        row8 = lax.broadcasted_iota(jnp.int32, (8, 1), 0)
        cw = cw_ref[...]
        for k in range(nsub):
            acc = cb_ref[...] + cw[SSD_CONV - 1:SSD_CONV, :] * res[k]
            for sh in range(1, SSD_CONV):
                rolled = pltpu.roll(res[k], sh, 0)
                top = jnp.where(row8 < sh, pltpu.roll(prev, sh, 0), rolled[0:8])
                shifted = jnp.concatenate([top, rolled[8:]], axis=0)
                acc = acc + cw[SSD_CONV - 1 - sh:SSD_CONV - sh, :] * shifted
            xbc_ref[k * rs:(k + 1) * rs, :] = acc.astype(BF16)
            prev = res[k][rs - 8:rs]
        halo_ref[jb] = prev

    segment(0, narrow(0, store_plain(qkv1_ref)))
    segment(1, narrow(1, store_grouped(qkv2_ref, ATT_PATTERNS[1][1])))
    segment(2, narrow(2, store_grouped(qkv3_ref, ATT_PATTERNS[2][1])))
    segment(3, wide(z_ref))
    segment(4, conv_silu)
    segment(5, wide(gates_ref))


def _in_proj(x2d, g, w_qkv, w_wide, w_dt, conv_w, conv_b, bsz, seq):
    t = x2d.shape[0]
    nj = _IN_START[-1]
    per_seq = seq // TM_IN
    d2, d3 = ATT_PATTERNS[1][1], ATT_PATTERNS[2][1]

    def seg_map(k):
        return lambda i, j: (i, jnp.clip(j - _IN_START[k], 0, _IN_SEG[k] - 1))

    def grp_map(k):
        return lambda i, j: (i // per_seq, 0, i % per_seq, jnp.clip(j - _IN_START[k], 0, _IN_SEG[k] - 1))

    def wq_map(i, j):
        jj = jnp.minimum(j, _IN_NARROW - 1)
        return (0, (jj % 3) * ATT_GROUPS + jj // 3)

    out_shapes = (
        jax.ShapeDtypeStruct((t, ATT_WIDTH), BF16),
        jax.ShapeDtypeStruct((bsz, d2, seq // d2, ATT_WIDTH), BF16),
        jax.ShapeDtypeStruct((bsz, d3, seq // d3, ATT_WIDTH), BF16),
        jax.ShapeDtypeStruct((t, SSD_INNER), BF16),
        jax.ShapeDtypeStruct((t, SSD_CONV_CH), BF16),
        jax.ShapeDtypeStruct((t, 2 * D_MODEL), BF16),
        jax.ShapeDtypeStruct((t, LANES), F32),
    )
    out_specs = (
        pl.BlockSpec((TM_IN, TN_IN), seg_map(0)),
        pl.BlockSpec((1, d2, TM_IN // d2, TN_IN), grp_map(1)),
        pl.BlockSpec((1, d3, TM_IN // d3, TN_IN), grp_map(2)),
        pl.BlockSpec((TM_IN, TN_WIDE), seg_map(3)),
        pl.BlockSpec((TM_IN, TN_WIDE), seg_map(4)),
        pl.BlockSpec((TM_IN, TN_WIDE), seg_map(5)),
        pl.BlockSpec((TM_IN, LANES), lambda i, j: (i, 0)),
    )
    conv_map = lambda i, j: (0, jnp.clip(j - _IN_START[4], 0, _IN_SEG[4] - 1))
    return pl.pallas_call(
        functools.partial(_inproj_kernel, per_seq=per_seq),
        grid=(t // TM_IN, nj),
        in_specs=[
            pl.BlockSpec((TM_IN, D_MODEL), lambda i, j: (i, 0)),
            pl.BlockSpec((1, D_MODEL), lambda i, j: (0, 0)),
            pl.BlockSpec((D_MODEL, TN_IN), wq_map),
            pl.BlockSpec((D_MODEL, TN_WIDE), lambda i, j: (0, jnp.maximum(j - _IN_NARROW, 0))),
            pl.BlockSpec((D_MODEL, LANES), lambda i, j: (0, 0)),
            pl.BlockSpec((SSD_CONV, TN_WIDE), conv_map),
            pl.BlockSpec((1, TN_WIDE), conv_map),
        ],
        out_specs=out_specs,
        out_shape=out_shapes,
        scratch_shapes=[pltpu.VMEM((3, TM_IN, D_MODEL), BF16),
                        pltpu.VMEM((D_MODEL // LANES, TM_IN, LANES), F32),
                        pltpu.VMEM((_IN_SEG[4], 8, TN_WIDE), F32)],
        compiler_params=_cparams("arbitrary", "arbitrary"),
    )(x2d, g, w_qkv, w_wide, w_dt, conv_w.astype(F32), conv_b.reshape(1, SSD_CONV_CH).astype(F32))


def _att_bias_tables():
    h = np.arange(1, ATT_GROUPS * ATT_HPG + 1, dtype=np.float32)
    slopes = np.exp2(-ALIBI_MAX_BIAS * h / (ATT_GROUPS * ATT_HPG)).astype(np.float32).reshape(ATT_GROUPS, ATT_HPG)
    qi = np.arange(ATT_BLOCK)[:, None] + ATT_BLOCK
    kj = np.arange(2 * ATT_BLOCK)[None, :]
    delta = qi - kj
    tabs = []
    for g, (window, dil) in enumerate(ATT_PATTERNS):
        span = window // dil
        band = (delta >= 0) & (delta <= span)
        bias = (-slopes[g][:, None, None] * (delta * dil).astype(np.float32)[None]).astype(np.float32)
        with_prev = np.where(band[None], bias, np.float32(NEG_BIG))
        first = np.where((band & (kj >= ATT_BLOCK))[None], bias, np.float32(NEG_BIG))
        tabs.append(np.stack([first, with_prev]).astype(np.float32))
    return tabs[0], tabs[1], tabs[2][1][:, :, ATT_BLOCK:]


def _att_units(units):
    lane = lax.broadcasted_iota(jnp.int32, (1, LANES), 1)
    low = lane < ATT_HEAD_DIM
    scale = ATT_HEAD_DIM ** -0.5
    qmask = (jnp.where(low, scale, 0.0).astype(BF16), jnp.where(low, 0.0, scale).astype(BF16))
    heads = [(q, k2, v2, tab_fn, hh) for q, k2, v2, tab_fn in units for hh in range(2)]
    scores = [lax.dot_general(q * qmask[hh], k2, (((1,), (1,)), ((), ())), preferred_element_type=F32)
              for q, k2, _, _, hh in heads]
    probs = []
    for s, (_, _, _, tab_fn, hh) in zip(scores, heads):
        s = s + tab_fn(hh)
        m = jnp.max(s, axis=-1, keepdims=True)
        e = jnp.exp(s - m)
        probs.append((e.astype(BF16), m, jnp.sum(e, axis=-1, keepdims=True)))
    pvs = [jnp.dot(e, v2, preferred_element_type=F32) for (e, _, _), (_, _, v2, _, _) in zip(probs, heads)]
    outs = [pv / den for pv, (_, _, den) in zip(pvs, probs)]
    lses = [m + jnp.log(den) for _, m, den in probs]
    return [(jnp.where(low, outs[2 * i], outs[2 * i + 1]), jnp.where(low, lses[2 * i], lses[2 * i + 1]))
            for i in range(len(units))]


def _att_kernel(q1, k1, v1, q2, k2, v2, q3, k3, v3, t1, t2, t3, out_ref,
                o1, l1, o2p, l2p, o2n, l2n, o3p, l3p, o3n, l3n, *, seq):
    blk = ATT_BLOCK
    d2, d3 = ATT_PATTERNS[1][1], ATT_PATTERNS[2][1]
    nb1, nb2 = seq // blk, seq // d2 // blk

    def rows(n):
        return pl.ds(pl.multiple_of(n * blk, blk), blk)

    assert nb1 == d2 * nb2 == d3

    def unit_body(i, _):
        units, dests = [], []
        for k in range(ATT_LOCKSTEP):
            u = i * ATT_LOCKSTEP + k
            cur, prv = rows(u), rows(jnp.maximum(u - 1, 0))
            sel = jnp.minimum(u, 1)
            units.append((q1[0, cur, :], jnp.concatenate([k1[0, prv, :], k1[0, cur, :]], axis=0),
                          jnp.concatenate([v1[0, prv, :], v1[0, cur, :]], axis=0),
                          lambda hh, sel=sel: t1[sel, hh]))
            r, n = u // nb2, u % nb2
            cur2, prv2 = rows(n), rows(jnp.maximum(n - 1, 0))
            sel2 = jnp.minimum(n, 1)
            units.append((q2[0, r, cur2, :], jnp.concatenate([k2[0, r, prv2, :], k2[0, r, cur2, :]], axis=0),
                          jnp.concatenate([v2[0, r, prv2, :], v2[0, r, cur2, :]], axis=0),
                          lambda hh, sel2=sel2: t2[sel2, hh]))
            units.append((q3[0, u], k3[0, u], v3[0, u], lambda hh: t3[hh]))
            dests += [(o1, l1, cur), (o2p, l2p, cur), (o3p, l3p, cur)]
        for (o, l), (o_ref, l_ref, where) in zip(_att_units(units), dests):
            o_ref[where, :] = o
            l_ref[where, :] = l
        return 0

    lax.fori_loop(0, nb1 // ATT_LOCKSTEP, unit_body, 0)

    for dil, pairs in ((d2, ((o2p, o2n), (l2p, l2n))), (d3, ((o3p, o3n), (l3p, l3n)))):
        n_sub = seq // dil
        for r in range(dil):
            for src, dst in pairs:
                dst[pl.ds(r, n_sub, stride=dil), :] = src[r * n_sub:(r + 1) * n_sub, :]

    mrows = 2 * blk

    def merge(c, _):
        rr = pl.ds(pl.multiple_of(c * mrows, mrows), mrows)
        la, lb, lc = l1[rr, :], l2n[rr, :], l3n[rr, :]
        lm = jnp.maximum(jnp.maximum(la, lb), lc)
        ea, eb, ec = jnp.exp(la - lm), jnp.exp(lb - lm), jnp.exp(lc - lm)
        att = (ea * o1[rr, :] + eb * o2n[rr, :] + ec * o3n[rr, :]) / (ea + eb + ec)
        out_ref[0, rr, :] = att.astype(BF16)
        return 0

    lax.fori_loop(0, seq // mrows, merge, 0)


def _attention(qkv1, qkv2, qkv3, bsz, seq):
    t1, t2, t3 = (jnp.asarray(t) for t in _att_bias_tables())
    d2, d3 = ATT_PATTERNS[1][1], ATT_PATTERNS[2][1]
    npair = ATT_HPG // 2
    in_arrays, in_specs = [], []
    for arr, lead in ((qkv1, ()), (qkv2, (d2,)), (qkv3, (d3,))):
        n_rows = arr.shape[-2]
        for sel in range(3):
            zeros = (0,) * len(lead)
            in_arrays.append(arr)
            in_specs.append(pl.BlockSpec((1,) + lead + (n_rows, LANES),
                                         lambda hp, b, sel=sel, zeros=zeros: (b,) + zeros + (0, sel * npair + hp)))
    in_arrays += [t1, t2, t3]
    in_specs += [pl.BlockSpec((2, 2, ATT_BLOCK, 2 * ATT_BLOCK), lambda hp, b: (0, hp, 0, 0)),
                 pl.BlockSpec((2, 2, ATT_BLOCK, 2 * ATT_BLOCK), lambda hp, b: (0, hp, 0, 0)),
                 pl.BlockSpec((2, ATT_BLOCK, ATT_BLOCK), lambda hp, b: (hp, 0, 0))]
    return pl.pallas_call(
        functools.partial(_att_kernel, seq=seq),
        grid=(npair, bsz),
        in_specs=in_specs,
        out_specs=pl.BlockSpec((1, seq, LANES), lambda hp, b: (b, 0, hp)),
        out_shape=jax.ShapeDtypeStruct((bsz, seq, ATT_OUT), BF16),
        scratch_shapes=[pltpu.VMEM((seq, LANES), F32)] * 10,
        compiler_params=_cparams("arbitrary", "arbitrary"),
    )(*in_arrays)


def _softplus(x):
    return jnp.maximum(x, 0.0) + jnp.log1p(jnp.exp(-jnp.abs(x)))


def _head_spread_table():
    gw, n_src = SSD_GW, LANES
    tab = np.zeros((2 * n_src, 2 * gw), np.float32)
    for half in range(2):
        for blk, src0 in enumerate((4, 8)):
            for col in range(gw):
                tab[half * n_src + src0 + col // SSD_HEAD_DIM, blk * gw + col] = 1.0
    return tab


def _ssd_kernel(x_ref, b_ref, c_ref, z_ref, dt_ref, spread_ref, dtb_ref, alog_ref, dsk_ref, u_ref, carry_ref,
                *, n_chunks):
    L = SSD_CHUNK
    gw = SSD_GW
    carry_ref[...] = jnp.zeros(carry_ref.shape, F32)

    a_neg = -jnp.exp(alog_ref[0])
    dtb = dtb_ref[0]
    dsk = dsk_ref[0]
    ri = lax.broadcasted_iota(jnp.int32, (L, L), 0)
    ci = lax.broadcasted_iota(jnp.int32, (L, L), 1)
    upper_incl = (ri <= ci).astype(F32)
    causal = ri >= ci
    lane = lax.broadcasted_iota(jnp.int32, (1, LANES), 1)
    low = lane < SSD_HEAD_DIM
    heads = range(SSD_HPG)

    def load(s):
        rows = pl.ds(s["r0"], L)
        s["xs"] = _silu(x_ref[0, rows, :].astype(F32))
        s["xs_b"] = s["xs"].astype(BF16)
        s["bm"] = _silu(b_ref[0, rows, :].astype(F32))
        s["cm_b"] = _silu(c_ref[0, rows, :].astype(F32)).astype(BF16)

    def decay_cumsum(s):
        dt = _softplus(dt_ref[0, 0, :, pl.ds(s["r0"], L)] + dtb)
        rows8 = jnp.concatenate([dt * a_neg, dt], axis=0)
        cs8 = jnp.dot(rows8, upper_incl, preferred_element_type=F32, precision=lax.Precision.HIGHEST)
        s["dt"], s["acs_t"] = dt, cs8[0:SSD_HPG]

    def decay_spread(s):
        dt, acs_t = s["dt"], s["acs_t"]
        t16 = jnp.concatenate([acs_t, acs_t, acs_t, dt], axis=0)
        cols = jnp.concatenate([t16, jnp.zeros((L - 16, L), F32)], axis=0).T
        dt_at8 = pltpu.roll(cols, LANES - 4, 1)
        comb = jnp.where(lane < 8, jnp.exp(cols), jnp.exp(cols[L - 1:L, :] - cols) * dt_at8)
        hi = comb.astype(BF16)
        lo = (comb - hi.astype(F32)).astype(BF16)
        s["spread"] = jnp.dot(jnp.concatenate([hi, lo], axis=1), spread_ref[...], preferred_element_type=F32)
        s["acs_cols"] = cols

    def scores(s):
        s["bm_t"] = s.pop("bm").T.astype(BF16)
        s["cb"] = jnp.dot(s["cm_b"], s["bm_t"], preferred_element_type=F32)

    def intra(s):
        spread, acs_t, dt, cb, cols = s["spread"], s.pop("acs_t"), s.pop("dt"), s.pop("cb"), s.pop("acs_cols")
        xs, xs_b = s["xs"], s.pop("xs_b")
        mixes = []
        for j in heads:
            seg = cols[:, j:j + 1] - acs_t[j:j + 1, :]
            mixes.append((jnp.exp(jnp.where(causal, seg, -jnp.inf)) * (cb * dt[j:j + 1, :])).astype(BF16))
        halves = []
        for hp in range(SSD_HPG // 2):
            yy = jnp.dot(jnp.concatenate(mixes[2 * hp:2 * hp + 2], axis=0), xs_b[:, hp * LANES:(hp + 1) * LANES],
                         preferred_element_type=F32)
            halves.append(jnp.where(low, yy[:L], yy[L:]))
        s["y"] = jnp.concatenate(halves, axis=1)
        s["st_new"] = jnp.dot(s.pop("bm_t"), (xs * spread[:, gw:]).astype(BF16),
                              preferred_element_type=F32)

    def inter(s):
        eacs_bc = s.pop("spread")[:, :gw]
        carry = carry_ref[...]
        y = s.pop("y") + jnp.dot(s.pop("cm_b"), carry.astype(BF16), preferred_element_type=F32) * eacs_bc
        carry_ref[...] = carry * eacs_bc[L - 1:L, :] + s.pop("st_new")
        y = y + s.pop("xs") * dsk
        rows = pl.ds(s["r0"], L)
        u_ref[0, rows, :] = (y * _silu(z_ref[0, rows, :].astype(F32))).astype(BF16)

    def body(i, _):
        states = [{"c": i * SSD_LOCKSTEP + k, "r0": pl.multiple_of((i * SSD_LOCKSTEP + k) * L, L)}
                  for k in range(SSD_LOCKSTEP)]
        for stage in (load, decay_cumsum, decay_spread, scores, intra, inter):
            for s in states:
                stage(s)
        return 0

    lax.fori_loop(0, n_chunks // SSD_LOCKSTEP, body, 0)


def _ssd(xbc, z, dt_t, dt_bias, a_log, d_skip, bsz, seq):
    gw = SSD_GW
    nxb = SSD_INNER // SSD_STATE
    dsk = jnp.repeat(d_skip.astype(F32), SSD_HEAD_DIM).reshape(SSD_GROUPS, 1, gw)
    dtb = dt_bias.astype(F32).reshape(SSD_GROUPS, SSD_HPG, 1)
    alog = a_log.astype(F32).reshape(SSD_GROUPS, SSD_HPG, 1)
    spread = jnp.asarray(_head_spread_table(), BF16)
    x_map = lambda b, g: (b, 0, g)
    bm_map = lambda b, g: (b, 0, nxb + g)
    cm_map = lambda b, g: (b, 0, nxb + SSD_GROUPS + g)
    return pl.pallas_call(
        functools.partial(_ssd_kernel, n_chunks=seq // SSD_CHUNK),
        grid=(bsz, SSD_GROUPS),
        in_specs=[
            pl.BlockSpec((1, seq, gw), x_map),
            pl.BlockSpec((1, seq, SSD_STATE), bm_map),
            pl.BlockSpec((1, seq, SSD_STATE), cm_map),
            pl.BlockSpec((1, seq, gw), x_map),
            pl.BlockSpec((1, 1, SSD_HPG, seq), lambda b, g: (b, g, 0, 0)),
            pl.BlockSpec(spread.shape, lambda b, g: (0, 0)),
            pl.BlockSpec((1, SSD_HPG, 1), lambda b, g: (g, 0, 0)),
            pl.BlockSpec((1, SSD_HPG, 1), lambda b, g: (g, 0, 0)),
            pl.BlockSpec((1, 1, gw), lambda b, g: (g, 0, 0)),
        ],
        out_specs=pl.BlockSpec((1, seq, gw), x_map),
        out_shape=jax.ShapeDtypeStruct((bsz, seq, SSD_INNER), BF16),
        scratch_shapes=[pltpu.VMEM((SSD_STATE, gw), F32)],
        compiler_params=_cparams("arbitrary", "arbitrary"),
    )(xbc, xbc, xbc, z, dt_t, spread, dtb, alog, dsk)


def _first_index_of_max(vals, lane_f):
    m = jnp.max(vals, axis=-1, keepdims=True)
    idx = jnp.min(jnp.where(vals == m, lane_f, float(LANES)), axis=-1, keepdims=True)
    return m, idx


def _post_kernel(att_ref, u_ref, gates_ref, x_ref, watt_ref, wssd_ref, wout_ref,
                 gssd_ref, gffn_ref, wr_ref, br_ref, x1_ref, meta_ref, cnt_ref, run_ref):
    i = pl.program_id(0)

    @pl.when(i == 0)
    def _():
        run_ref[...] = jnp.zeros(run_ref.shape, F32)

    tm = x_ref.shape[0]
    nsub = 2
    subs = [{"rows": pl.ds(k * (tm // nsub), tm // nsub)} for k in range(nsub)]

    def att_branch(s):
        s["y_att"] = jnp.dot(att_ref[s["rows"], :], watt_ref[...], preferred_element_type=F32)

    def ssd_branch(s):
        ssd = _rms(u_ref[s["rows"], :].astype(F32), gssd_ref[...])
        s["y_ssd"] = jnp.dot(ssd.astype(BF16), wssd_ref[...], preferred_element_type=F32)

    def mix_out(s):
        gates = gates_ref[s["rows"], :].astype(F32)
        merged = _sigmoid(gates[:, :D_MODEL]) * s.pop("y_att") + _sigmoid(gates[:, D_MODEL:]) * s.pop("y_ssd")
        x1 = x_ref[s["rows"], :] + jnp.dot(merged.astype(BF16), wout_ref[...], preferred_element_type=F32)
        x1_ref[s["rows"], :] = x1
        s["h2"] = _rms(x1, gffn_ref[...])

    def router(s):
        h2 = s.pop("h2")
        hi = h2.astype(BF16)
        lo = (h2 - hi.astype(F32)).astype(BF16)
        s["logits"] = jnp.dot(jnp.concatenate([hi, lo, hi], axis=1), wr_ref[...], preferred_element_type=F32)

    for stage in (att_branch, ssd_branch, mix_out, router):
        for s in subs:
            stage(s)
    logits = jnp.concatenate([s["logits"] for s in subs], axis=0) + br_ref[...]
    lane = lax.broadcasted_iota(jnp.int32, (tm, LANES), 1)
    lane_f = lane.astype(F32)
    ninf = -jnp.inf
    gl = jnp.where(lane < MOE_GROUPS, logits, ninf)
    gmax, gidx = _first_index_of_max(gl, lane_f)
    g_val = 1.0 / jnp.sum(jnp.exp(gl - gmax), axis=-1, keepdims=True)
    base = MOE_GROUPS + MOE_EPG * gidx
    el = jnp.where((lane_f >= base) & (lane_f < base + MOE_EPG), logits, ninf)
    e1, i1 = _first_index_of_max(el, lane_f)
    e2, i2 = _first_index_of_max(jnp.where(lane_f == i1, ninf, el), lane_f)
    t2 = jnp.exp(e2 - e1)
    w1 = g_val / (1.0 + t2)
    w2 = g_val * t2 / (1.0 + t2)
    a1, a2 = i1 - base, i2 - base
    lo, hi = jnp.minimum(a1, a2), jnp.maximum(a1, a2)
    c_lo = jnp.where(a1 < a2, w1, w2)
    c_hi = jnp.where(a1 < a2, w2, w1)
    pair = lo * (7.0 - lo) * 0.5 + (hi - lo - 1.0)
    cls = gidx * MOE_PAIRS + pair

    onehot = (lane_f == cls)
    oh_b = jnp.where(onehot, 1.0, 0.0).astype(BF16)
    rr = lax.broadcasted_iota(jnp.int32, (tm, tm), 0)
    cc = lax.broadcasted_iota(jnp.int32, (tm, tm), 1)
    strict = jnp.where(rr > cc, 1.0, 0.0).astype(BF16)
    prefix = jnp.dot(strict, oh_b, preferred_element_type=F32) + run_ref[...]
    rank = jnp.sum(jnp.where(onehot, prefix, 0.0), axis=-1, keepdims=True)
    run = run_ref[...] + jnp.sum(oh_b.astype(F32), axis=0, keepdims=True)
    run_ref[...] = run
    cnt_ref[...] = jnp.broadcast_to(run, cnt_ref.shape)

    meta = jnp.where(lane == 0, cls, jnp.where(lane == 1, rank, jnp.where(lane == 2, c_lo,
                     jnp.where(lane == 3, c_hi, 0.0))))
    meta_ref[...] = meta


def _post(att, u, gates, x2d, w_att, w_ssd, w_out, g_ssd, g_ffn, w_r, b_r):
    t = x2d.shape[0]
    tm = TM_POST
    row = lambda w: pl.BlockSpec((tm, w), lambda i: (i, 0))
    const = lambda a: pl.BlockSpec(a.shape, lambda i: (0,) * a.ndim)
    return pl.pallas_call(
        _post_kernel,
        grid=(t // tm,),
        in_specs=[row(ATT_OUT), row(SSD_INNER), row(2 * D_MODEL), row(D_MODEL),
                  const(w_att), const(w_ssd), const(w_out), const(g_ssd), const(g_ffn), const(w_r), const(b_r)],
        out_specs=(row(D_MODEL), row(LANES), pl.BlockSpec((8, LANES), lambda i: (0, 0))),
        out_shape=(jax.ShapeDtypeStruct((t, D_MODEL), F32),
                   jax.ShapeDtypeStruct((t, LANES), F32),
                   jax.ShapeDtypeStruct((8, LANES), F32)),
        scratch_shapes=[pltpu.VMEM((1, LANES), F32)],
        compiler_params=_cparams("arbitrary"),
    )(att, u, gates, x2d, w_att, w_ssd, w_out, g_ssd, g_ffn, w_r, b_r)


def _pack_pair(a, b):
    ua = lax.bitcast_convert_type(a.astype(BF16).astype(F32), jnp.uint32)
    ub = lax.bitcast_convert_type(b.astype(BF16).astype(F32), jnp.uint32)
    return (ua & jnp.uint32(0xFFFF0000)) | (ub >> 16)


def _unpack_pair(p):
    a = lax.bitcast_convert_type(p & jnp.uint32(0xFFFF0000), F32)
    b = lax.bitcast_convert_type(p << 16, F32)
    return a, b


def _sc_mesh():
    return plsc.VectorSubcoreMesh(core_axis_name="c", subcore_axis_name="s")


def _sc_permute_rows(data, idx, n_out, scatter):
    n, width = idx.shape[0], data.shape[1]
    mesh = _sc_mesh()
    workers = mesh.num_cores * mesh.num_subcores
    per_worker = n // SC_WINDOW // workers
    assert per_worker * workers * SC_WINDOW == n

    n_sub = SC_WINDOW // SC_SUB

    @pl.kernel(out_type=jax.ShapeDtypeStruct((n_out, width), data.dtype), mesh=mesh,
               scratch_types=[pltpu.VMEM((1, SC_WINDOW), jnp.int32), pltpu.VMEM((2, SC_SUB, width), data.dtype),
                              pltpu.SemaphoreType.DMA((2,)), pltpu.SemaphoreType.DMA((2,))])
    def permute(data_hbm, idx_hbm, out_hbm, idx_v, rows_v, sem_in, sem_out):
        worker = lax.axis_index("c") * mesh.num_subcores + lax.axis_index("s")

        @pl.loop(0, per_worker)
        def _(j):
            base = (worker * per_worker + j) * SC_WINDOW
            pltpu.sync_copy(idx_hbm.at[:, pl.ds(base, SC_WINDOW)], idx_v)

            def copies(k):
                slot = k % 2
                sub_idx = idx_v.at[0, pl.ds(k * SC_SUB, SC_SUB)]
                plain = pl.ds(base + k * SC_SUB, SC_SUB)
                src, dst = (data_hbm.at[plain], out_hbm.at[sub_idx]) if scatter else (data_hbm.at[sub_idx],
                                                                                      out_hbm.at[plain])
                return (pltpu.make_async_copy(src, rows_v.at[slot], sem_in.at[slot]),
                        pltpu.make_async_copy(rows_v.at[slot], dst, sem_out.at[slot]))

            copies(0)[0].start()
            for k in range(n_sub):
                read, write = copies(k)
                read.wait()
                if k + 1 < n_sub:
                    if k >= 1:
                        copies(k - 1)[1].wait()
                    copies(k + 1)[0].start()
                write.start()
            copies(n_sub - 2)[1].wait()
            copies(n_sub - 1)[1].wait()

    return permute(data, idx.reshape(1, n))


def _sc_scatter_rows(data, idx, n_out):
    return _sc_permute_rows(data, idx, n_out, scatter=True)


def _sc_gather_rows(data, idx):
    return _sc_permute_rows(data, idx, idx.shape[0], scatter=False)


def _expert_kernel(nused_ref, elo_ref, ehi_ref, valid_ref, xs_ref, g_ref, wg_lo, wu_lo, wd_lo, wg_hi, wu_hi, wd_hi,
                   ys_ref):
    del elo_ref, ehi_ref
    i = pl.program_id(0)

    @pl.when(i < nused_ref[0])
    def _():
        row = lax.broadcasted_iota(jnp.int32, (xs_ref.shape[0], 1), 0)
        xb = _rms(jnp.where(row < valid_ref[i], xs_ref[...], 0.0), g_ref[...]).astype(BF16)
        pre = [(jnp.dot(xb, wg[0], preferred_element_type=F32), jnp.dot(xb, wu[0], preferred_element_type=F32))
               for wg, wu in ((wg_lo, wu_lo), (wg_hi, wu_hi))]
        hid = [(_silu(gt) * up).astype(BF16) for gt, up in pre]
        y_lo, y_hi = (jnp.dot(h, wd[0], preferred_element_type=F32) for h, wd in zip(hid, (wd_lo, wd_hi)))
        ys_ref[...] = _pack_pair(y_lo, y_hi)

    @pl.when(i >= nused_ref[0])
    def _():
        ys_ref[...] = jnp.zeros(ys_ref.shape, jnp.uint32)


def _experts(n_used, tile_elo, tile_ehi, tile_valid, xs, g_ffn, w_gate, w_up, w_down):
    p_rows = xs.shape[0]
    tm = TM_EXP
    n_tiles = p_rows // tm

    def last_used(i, nu):
        return jnp.minimum(i, jnp.maximum(nu[0] - 1, 0))

    def row_map(i, nu, elo, ehi, valid):
        return (last_used(i, nu), 0)

    def wmap(which):
        def f(i, nu, elo, ehi, valid):
            return ((elo, ehi)[which][last_used(i, nu)], 0, 0)
        return f

    wspec_in = lambda which: pl.BlockSpec((1, D_MODEL, MOE_HIDDEN), wmap(which))
    wspec_out = lambda which: pl.BlockSpec((1, MOE_HIDDEN, D_MODEL), wmap(which))
    grid_spec = pltpu.PrefetchScalarGridSpec(
        num_scalar_prefetch=4,
        grid=(n_tiles,),
        in_specs=[pl.BlockSpec((tm, D_MODEL), row_map),
                  pl.BlockSpec((1, D_MODEL), lambda i, nu, elo, ehi, valid: (0, 0)),
                  wspec_in(0), wspec_in(0), wspec_out(0), wspec_in(1), wspec_in(1), wspec_out(1)],
        out_specs=pl.BlockSpec((tm, D_MODEL), lambda i, nu, elo, ehi, valid: (i, 0)),
    )
    return pl.pallas_call(
        _expert_kernel,
        grid_spec=grid_spec,
        out_shape=jax.ShapeDtypeStruct((p_rows, D_MODEL), jnp.uint32),
        compiler_params=_cparams("arbitrary"),
    )(n_used, tile_elo, tile_ehi, tile_valid, xs, g_ffn, w_gate, w_up, w_down, w_gate, w_up, w_down)


def _final_kernel(x1_ref, meta_ref, p_ref, yt_ref, gple_ref, wpg_ref, wpp_ref, gfin_ref, out_ref):
    tm = x1_ref.shape[0]
    nsub = 2
    subs = [{"rows": pl.ds(k * (tm // nsub), tm // nsub)} for k in range(nsub)]

    def embed(s):
        s["pp"] = jnp.dot(p_ref[s["rows"], :].astype(BF16), wpp_ref[...], preferred_element_type=F32)

    def combine(s):
        y_lo, y_hi = _unpack_pair(yt_ref[s["rows"], :])
        meta = meta_ref[s["rows"], :]
        s["x2"] = x1_ref[s["rows"], :] + meta[:, 2:3] * y_lo + meta[:, 3:4] * y_hi
        s["hn"] = _rms(s["x2"], gple_ref[...]).astype(BF16)

    def gate(s):
        s["gate"] = jnp.dot(s.pop("hn"), wpg_ref[...], preferred_element_type=F32)

    def finish(s):
        out_ref[s["rows"], :] = _rms(s.pop("x2") + _sigmoid(s.pop("gate")) * s.pop("pp"), gfin_ref[...])

    for stage in (embed, combine, gate, finish):
        for s in subs:
            stage(s)


def _final(x1, meta, p2d, ys_tok, g_ple, w_pg, w_pp, g_fin):
    t = x1.shape[0]
    tm = TM_FIN
    row = lambda w: pl.BlockSpec((tm, w), lambda i: (i, 0))
    const = lambda a: pl.BlockSpec(a.shape, lambda i: (0,) * a.ndim)
    return pl.pallas_call(
        _final_kernel,
        grid=(t // tm,),
        in_specs=[row(D_MODEL), row(LANES), row(PLE_DIM), row(D_MODEL),
                  const(g_ple), const(w_pg), const(w_pp), const(g_fin)],
        out_specs=row(D_MODEL),
        out_shape=jax.ShapeDtypeStruct((t, D_MODEL), F32),
        compiler_params=_cparams("arbitrary"),
    )(x1, meta, p2d, ys_tok, g_ple, w_pg, w_pp, g_fin)


_PAIR_LO = np.array([0, 0, 0, 1, 1, 2], np.int32)
_PAIR_HI = np.array([1, 2, 3, 2, 3, 3], np.int32)


def _routing_tables(meta, counts_f, n_tiles):
    cls = meta[:, 0].astype(jnp.int32)
    rank = meta[:, 1].astype(jnp.int32)
    counts = counts_f[0, :MOE_CLASSES].astype(jnp.int32)
    tiles_per = (counts + TM_EXP - 1) // TM_EXP
    tile_end = jnp.cumsum(tiles_per)
    tile_start = tile_end - tiles_per
    class_ids = jnp.arange(MOE_CLASSES, dtype=jnp.int32)
    pos = jnp.sum(jnp.where(cls[:, None] == class_ids[None, :], (tile_start * TM_EXP)[None, :], 0), axis=1) + rank
    n_used = tile_end[-1:]
    tile_ids = jnp.arange(n_tiles, dtype=jnp.int32)
    tile_cls = jnp.minimum(jnp.sum((tile_end[None, :] <= tile_ids[:, None]).astype(jnp.int32), axis=1),
                           MOE_CLASSES - 1)
    grp = tile_cls // MOE_PAIRS
    pair = tile_cls % MOE_PAIRS
    tile_elo = grp * MOE_EPG + jnp.asarray(_PAIR_LO)[pair]
    tile_ehi = grp * MOE_EPG + jnp.asarray(_PAIR_HI)[pair]
    class_left = jnp.sum(jnp.where(tile_cls[:, None] == class_ids[None, :],
                                   (counts - (tile_ids[:, None] - tile_start[None, :]) * TM_EXP), 0), axis=1)
    tile_valid = jnp.clip(class_left, 0, TM_EXP)
    return (pos.astype(jnp.int32), n_used.astype(jnp.int32), tile_elo.astype(jnp.int32),
            tile_ehi.astype(jnp.int32), tile_valid.astype(jnp.int32))


def kernel(x, p, norm_mix_g, w_in, conv_w, conv_b, dt_bias, a_log, d_skip, ssd_norm_g, w_att_branch,
           w_ssd_branch, w_out, norm_ffn_g, w_router_group, b_router_group, w_router_expert,
           b_router_expert, w_exp_gate, w_exp_up, w_exp_down, norm_ple_g, w_ple_gate, w_ple_proj,
           final_norm_g):
    bsz, seq, _ = x.shape
    t = bsz * seq
    assert w_in.shape[0] == 1, "single-layer block"
    assert seq // ATT_PATTERNS[-1][1] == ATT_BLOCK and seq % TM_IN == 0
    x2d = x.reshape(t, D_MODEL)

    wi = w_in[0]
    c_dt = QKV_W + SSD_INNER + SSD_CONV_CH
    w_qkv = wi[:, :QKV_W].astype(BF16)
    w_wide = jnp.concatenate([wi[:, QKV_W:c_dt], wi[:, c_dt + SSD_HEADS:]], axis=1).astype(BF16)
    w_dt = jnp.pad(wi[:, c_dt:c_dt + SSD_HEADS], ((0, 0), (0, LANES - SSD_HEADS))).astype(BF16)
    row = lambda v: v.reshape(1, -1).astype(F32)

    qkv1, qkv2, qkv3, z, xbc, gates, dt_raw = _in_proj(x2d, row(norm_mix_g[0]), w_qkv, w_wide, w_dt,
                                                       conv_w[0], conv_b[0], bsz, seq)

    att = _attention(qkv1.reshape(bsz, seq, ATT_WIDTH), qkv2, qkv3, bsz, seq).reshape(t, ATT_OUT)

    dt_t = dt_raw[:, :SSD_HEADS].reshape(bsz, seq, SSD_GROUPS, SSD_HPG).transpose(0, 2, 3, 1)
    u = _ssd(xbc.reshape(bsz, seq, SSD_CONV_CH), z.reshape(bsz, seq, SSD_INNER), dt_t,
             dt_bias[0], a_log[0], d_skip[0], bsz, seq).reshape(t, SSD_INNER)

    w_r32 = jnp.pad(jnp.concatenate([w_router_group[0], w_router_expert[0]], axis=1),
                    ((0, 0), (0, LANES - MOE_GROUPS - MOE_EXPERTS))).astype(F32)
    w_r_hi = w_r32.astype(BF16)
    w_r_lo = (w_r32 - w_r_hi.astype(F32)).astype(BF16)
    w_r = jnp.concatenate([w_r_hi, w_r_hi, w_r_lo], axis=0)
    b_r = jnp.pad(jnp.concatenate([b_router_group[0], b_router_expert[0]]),
                  (0, LANES - MOE_GROUPS - MOE_EXPERTS)).reshape(1, LANES).astype(F32)
    x1, meta, counts = _post(att, u, gates, x2d,
                             w_att_branch[0].astype(BF16), w_ssd_branch[0].astype(BF16), w_out[0].astype(BF16),
                             row(ssd_norm_g[0]), row(norm_ffn_g[0]), w_r, b_r)

    n_tiles = t // TM_EXP + MOE_CLASSES
    pos, n_used, tile_elo, tile_ehi, tile_valid = _routing_tables(meta, counts, n_tiles)
    xs = _sc_scatter_rows(x1, pos, n_tiles * TM_EXP)
    ys = _experts(n_used, tile_elo, tile_ehi, tile_valid, xs, row(norm_ffn_g[0]),
                  w_exp_gate[0].astype(BF16), w_exp_up[0].astype(BF16), w_exp_down[0].astype(BF16))
    ys_tok = _sc_gather_rows(ys, pos)
    out = _final(x1, meta, p[0].reshape(t, PLE_DIM), ys_tok, row(norm_ple_g[0]),
                 w_ple_gate[0].astype(BF16), w_ple_proj[0].astype(BF16), row(final_norm_g))
    return out.reshape(bsz, seq, D_MODEL)
```

```python
import functools

import numpy as np
import jax
import jax.numpy as jnp
from jax import lax
from jax.experimental import pallas as pl
from jax.experimental.pallas import tpu as pltpu
from jax.experimental.pallas import tpu_sc as plsc

F32 = jnp.float32
BF16 = jnp.bfloat16

D_MODEL = 1024
PLE_DIM = 256
RMS_EPS = 1e-6

ATT_PATTERNS = ((128, 1), (512, 4), (2048, 16))
ATT_GROUPS = 3
ATT_HPG = 8
ATT_HEAD_DIM = 64
ATT_WIDTH = ATT_GROUPS * ATT_HPG * ATT_HEAD_DIM
ATT_OUT = ATT_HPG * ATT_HEAD_DIM
ATT_BLOCK = 128
ATT_LOCKSTEP = 2
ALIBI_MAX_BIAS = 8.0
QKV_W = 3 * ATT_WIDTH

SSD_INNER = 2048
SSD_HEADS = 32
SSD_GROUPS = 8
SSD_HPG = 4
SSD_HEAD_DIM = 64
SSD_STATE = 128
SSD_CONV = 4
SSD_CHUNK = 128
SSD_CONV_CH = SSD_INNER + 2 * SSD_GROUPS * SSD_STATE
SSD_GW = SSD_HPG * SSD_HEAD_DIM
SSD_LOCKSTEP = 8

MOE_GROUPS = 4
MOE_EPG = 4
MOE_EXPERTS = 16
MOE_HIDDEN = 512
MOE_PAIRS = 6
MOE_CLASSES = MOE_GROUPS * MOE_PAIRS

LANES = 128
NEG_BIG = -1e30
VMEM_LIMIT = 56 * 1024 * 1024

TM_IN = 1024
TN_IN = 512
TN_WIDE = 1024
TM_POST = 512
TM_EXP = 256
TM_FIN = 512
SC_WINDOW = 128
SC_SUB = 16


def _cparams(*sem):
    return pltpu.CompilerParams(dimension_semantics=sem, vmem_limit_bytes=VMEM_LIMIT)


def _sigmoid(x):
    return 0.5 * jnp.tanh(0.5 * x) + 0.5


def _silu(x):
    h = 0.5 * x
    return h + h * jnp.tanh(h)


def _rms(x, g):
    ms = jnp.mean(x * x, axis=-1, keepdims=True)
    return x * lax.rsqrt(ms + RMS_EPS) * g


_IN_SEG = (3, 3, 3, SSD_INNER // TN_WIDE, SSD_CONV_CH // TN_WIDE, 2 * D_MODEL // TN_WIDE)
_IN_START = tuple(int(v) for v in np.cumsum((0,) + _IN_SEG))
_IN_NARROW = _IN_START[3]


def _inproj_kernel(x_ref, g_ref, wq_ref, ww_ref, wdt_ref, cw_ref, cb_ref, qkv1_ref, qkv2_ref, qkv3_ref, z_ref,
                   xbc_ref, gates_ref, dt_ref, h_ref, hcol_ref, halo_ref, *, per_seq):
    j = pl.program_id(1)
    tm = x_ref.shape[0]

    @pl.when(j == 0)
    def _():
        h = _rms(x_ref[...], g_ref[...])
        hb = h.astype(BF16)
        h_ref[0] = hb
        dt_ref[...] = jnp.dot(hb, wdt_ref[...], preferred_element_type=F32)
        ncb = hcol_ref.shape[0]
        for c in range(ncb):
            hcol_ref[c] = h[:, c * LANES:(c + 1) * LANES]
        for slot, (_, dil) in enumerate(ATT_PATTERNS[1:], start=1):
            rows = tm // dil
            for r in range(dil):
                for c in range(ncb):
                    h_ref[slot, r * rows:(r + 1) * rows, c * LANES:(c + 1) * LANES] = (
                        hcol_ref[c, pl.ds(r, rows, stride=dil), :].astype(BF16))

    def segment(k, fn):
        @pl.when((j >= _IN_START[k]) & (j < _IN_START[k + 1]))
        def _():
            fn()

    def narrow(slot, store):
        def fn():
            store(jnp.dot(h_ref[slot], wq_ref[...], preferred_element_type=F32).astype(BF16))
        return fn

    def wide(ref):
        def fn():
            ref[...] = jnp.dot(h_ref[0], ww_ref[...], preferred_element_type=F32).astype(BF16)
        return fn

    def store_plain(ref):
        def store(res):
            ref[...] = res
        return store

    def store_grouped(ref, dil):
        def store(res):
            ref[0] = res.reshape(dil, tm // dil, res.shape[1])
        return store

    def conv_silu():
        jb = j - _IN_START[4]
        nsub = 4
        rs = tm // nsub
        res = [jnp.dot(h_ref[0, k * rs:(k + 1) * rs, :], ww_ref[...], preferred_element_type=F32)
               for k in range(nsub)]
        seq_start = pl.program_id(0) % per_seq == 0
        prev = jnp.where(seq_start, 0.0, halo_ref[jb])
        row8 = lax.broadcasted_iota(jnp.int32, (8, 1), 0)
        cw = cw_ref[...]
        for k in range(nsub):
            acc = cb_ref[...] + cw[SSD_CONV - 1:SSD_CONV, :] * res[k]
            for sh in range(1, SSD_CONV):
                rolled = pltpu.roll(res[k], sh, 0)
                top = jnp.where(row8 < sh, pltpu.roll(prev, sh, 0), rolled[0:8])
                shifted = jnp.concatenate([top, rolled[8:]], axis=0)
                acc = acc + cw[SSD_CONV - 1 - sh:SSD_CONV - sh, :] * shifted
            xbc_ref[k * rs:(k + 1) * rs, :] = acc.astype(BF16)
            prev = res[k][rs - 8:rs]
        halo_ref[jb] = prev

    segment(0, narrow(0, store_plain(qkv1_ref)))
    segment(1, narrow(1, store_grouped(qkv2_ref, ATT_PATTERNS[1][1])))
    segment(2, narrow(2, store_grouped(qkv3_ref, ATT_PATTERNS[2][1])))
    segment(3, wide(z_ref))
    segment(4, conv_silu)
    segment(5, wide(gates_ref))


def _in_proj(x2d, g, w_qkv, w_wide, w_dt, conv_w, conv_b, bsz, seq):
    t = x2d.shape[0]
    nj = _IN_START[-1]
    per_seq = seq // TM_IN
    d2, d3 = ATT_PATTERNS[1][1], ATT_PATTERNS[2][1]

    def seg_map(k):
        return lambda i, j: (i, jnp.clip(j - _IN_START[k], 0, _IN_SEG[k] - 1))

    def grp_map(k):
        return lambda i, j: (i // per_seq, 0, i % per_seq, jnp.clip(j - _IN_START[k], 0, _IN_SEG[k] - 1))

    def wq_map(i, j):
        jj = jnp.minimum(j, _IN_NARROW - 1)
        return (0, (jj % 3) * ATT_GROUPS + jj // 3)

    out_shapes = (
        jax.ShapeDtypeStruct((t, ATT_WIDTH), BF16),
        jax.ShapeDtypeStruct((bsz, d2, seq // d2, ATT_WIDTH), BF16),
        jax.ShapeDtypeStruct((bsz, d3, seq // d3, ATT_WIDTH), BF16),
        jax.ShapeDtypeStruct((t, SSD_INNER), BF16),
        jax.ShapeDtypeStruct((t, SSD_CONV_CH), BF16),
        jax.ShapeDtypeStruct((t, 2 * D_MODEL), BF16),
        jax.ShapeDtypeStruct((t, LANES), F32),
    )
    out_specs = (
        pl.BlockSpec((TM_IN, TN_IN), seg_map(0)),
        pl.BlockSpec((1, d2, TM_IN // d2, TN_IN), grp_map(1)),
        pl.BlockSpec((1, d3, TM_IN // d3, TN_IN), grp_map(2)),
        pl.BlockSpec((TM_IN, TN_WIDE), seg_map(3)),
        pl.BlockSpec((TM_IN, TN_WIDE), seg_map(4)),
        pl.BlockSpec((TM_IN, TN_WIDE), seg_map(5)),
        pl.BlockSpec((TM_IN, LANES), lambda i, j: (i, 0)),
    )
    conv_map = lambda i, j: (0, jnp.clip(j - _IN_START[4], 0, _IN_SEG[4] - 1))
    return pl.pallas_call(
        functools.partial(_inproj_kernel, per_seq=per_seq),
        grid=(t // TM_IN, nj),
        in_specs=[
            pl.BlockSpec((TM_IN, D_MODEL), lambda i, j: (i, 0)),
            pl.BlockSpec((1, D_MODEL), lambda i, j: (0, 0)),
            pl.BlockSpec((D_MODEL, TN_IN), wq_map),
            pl.BlockSpec((D_MODEL, TN_WIDE), lambda i, j: (0, jnp.maximum(j - _IN_NARROW, 0))),
            pl.BlockSpec((D_MODEL, LANES), lambda i, j: (0, 0)),
            pl.BlockSpec((SSD_CONV, TN_WIDE), conv_map),
            pl.BlockSpec((1, TN_WIDE), conv_map),
        ],
        out_specs=out_specs,
        out_shape=out_shapes,
        scratch_shapes=[pltpu.VMEM((3, TM_IN, D_MODEL), BF16),
                        pltpu.VMEM((D_MODEL // LANES, TM_IN, LANES), F32),
                        pltpu.VMEM((_IN_SEG[4], 8, TN_WIDE), F32)],
        compiler_params=_cparams("arbitrary", "arbitrary"),
    )(x2d, g, w_qkv, w_wide, w_dt, conv_w.astype(F32), conv_b.reshape(1, SSD_CONV_CH).astype(F32))


def _att_bias_tables():
    h = np.arange(1, ATT_GROUPS * ATT_HPG + 1, dtype=np.float32)
    slopes = np.exp2(-ALIBI_MAX_BIAS * h / (ATT_GROUPS * ATT_HPG)).astype(np.float32).reshape(ATT_GROUPS, ATT_HPG)
    qi = np.arange(ATT_BLOCK)[:, None] + ATT_BLOCK
    kj = np.arange(2 * ATT_BLOCK)[None, :]
    delta = qi - kj
    tabs = []
    for g, (window, dil) in enumerate(ATT_PATTERNS):
        span = window // dil
        band = (delta >= 0) & (delta <= span)
        bias = (-slopes[g][:, None, None] * (delta * dil).astype(np.float32)[None]).astype(np.float32)
        with_prev = np.where(band[None], bias, np.float32(NEG_BIG))
        first = np.where((band & (kj >= ATT_BLOCK))[None], bias, np.float32(NEG_BIG))
        tabs.append(np.stack([first, with_prev]).astype(np.float32))
    return tabs[0], tabs[1], tabs[2][1][:, :, ATT_BLOCK:]


def _att_units(units):
    lane = lax.broadcasted_iota(jnp.int32, (1, LANES), 1)
    low = lane < ATT_HEAD_DIM
    scale = ATT_HEAD_DIM ** -0.5
    qmask = (jnp.where(low, scale, 0.0).astype(BF16), jnp.where(low, 0.0, scale).astype(BF16))
    heads = [(q, k2, v2, tab_fn, hh) for q, k2, v2, tab_fn in units for hh in range(2)]
    scores = [lax.dot_general(q * qmask[hh], k2, (((1,), (1,)), ((), ())), preferred_element_type=F32)
              for q, k2, _, _, hh in heads]
    probs = []
    for s, (_, _, _, tab_fn, hh) in zip(scores, heads):
        s = s + tab_fn(hh)
        m = jnp.max(s, axis=-1, keepdims=True)
        e = jnp.exp(s - m)
        probs.append((e.astype(BF16), m, jnp.sum(e, axis=-1, keepdims=True)))
    pvs = [jnp.dot(e, v2, preferred_element_type=F32) for (e, _, _), (_, _, v2, _, _) in zip(probs, heads)]
    outs = [pv / den for pv, (_, _, den) in zip(pvs, probs)]
    lses = [m + jnp.log(den) for _, m, den in probs]
    return [(jnp.where(low, outs[2 * i], outs[2 * i + 1]), jnp.where(low, lses[2 * i], lses[2 * i + 1]))
            for i in range(len(units))]


def _att_kernel(q1, k1, v1, q2, k2, v2, q3, k3, v3, t1, t2, t3, out_ref,
                o1, l1, o2p, l2p, o2n, l2n, o3p, l3p, o3n, l3n, *, seq):
    blk = ATT_BLOCK
    d2, d3 = ATT_PATTERNS[1][1], ATT_PATTERNS[2][1]
    nb1, nb2 = seq // blk, seq // d2 // blk

    def rows(n):
        return pl.ds(pl.multiple_of(n * blk, blk), blk)

    assert nb1 == d2 * nb2 == d3

    def unit_body(i, _):
        units, dests = [], []
        for k in range(ATT_LOCKSTEP):
            u = i * ATT_LOCKSTEP + k
            cur, prv = rows(u), rows(jnp.maximum(u - 1, 0))
            sel = jnp.minimum(u, 1)
            units.append((q1[0, cur, :], jnp.concatenate([k1[0, prv, :], k1[0, cur, :]], axis=0),
                          jnp.concatenate([v1[0, prv, :], v1[0, cur, :]], axis=0),
                          lambda hh, sel=sel: t1[sel, hh]))
            r, n = u // nb2, u % nb2
            cur2, prv2 = rows(n), rows(jnp.maximum(n - 1, 0))
            sel2 = jnp.minimum(n, 1)
            units.append((q2[0, r, cur2, :], jnp.concatenate([k2[0, r, prv2, :], k2[0, r, cur2, :]], axis=0),
                          jnp.concatenate([v2[0, r, prv2, :], v2[0, r, cur2, :]], axis=0),
                          lambda hh, sel2=sel2: t2[sel2, hh]))
            units.append((q3[0, u], k3[0, u], v3[0, u], lambda hh: t3[hh]))
            dests += [(o1, l1, cur), (o2p, l2p, cur), (o3p, l3p, cur)]
        for (o, l), (o_ref, l_ref, where) in zip(_att_units(units), dests):
            o_ref[where, :] = o
            l_ref[where, :] = l
        return 0

    lax.fori_loop(0, nb1 // ATT_LOCKSTEP, unit_body, 0)

    for dil, pairs in ((d2, ((o2p, o2n), (l2p, l2n))), (d3, ((o3p, o3n), (l3p, l3n)))):
        n_sub = seq // dil
        for r in range(dil):
            for src, dst in pairs:
                dst[pl.ds(r, n_sub, stride=dil), :] = src[r * n_sub:(r + 1) * n_sub, :]

    mrows = 2 * blk

    def merge(c, _):
        rr = pl.ds(pl.multiple_of(c * mrows, mrows), mrows)
        la, lb, lc = l1[rr, :], l2n[rr, :], l3n[rr, :]
        lm = jnp.maximum(jnp.maximum(la, lb), lc)
        ea, eb, ec = jnp.exp(la - lm), jnp.exp(lb - lm), jnp.exp(lc - lm)
        att = (ea * o1[rr, :] + eb * o2n[rr, :] + ec * o3n[rr, :]) / (ea + eb + ec)
        out_ref[0, rr, :] = att.astype(BF16)
        return 0

    lax.fori_loop(0, seq // mrows, merge, 0)


def _attention(qkv1, qkv2, qkv3, bsz, seq):
    t1, t2, t3 = (jnp.asarray(t) for t in _att_bias_tables())
    d2, d3 = ATT_PATTERNS[1][1], ATT_PATTERNS[2][1]
    npair = ATT_HPG // 2
    in_arrays, in_specs = [], []
    for arr, lead in ((qkv1, ()), (qkv2, (d2,)), (qkv3, (d3,))):
        n_rows = arr.shape[-2]
        for sel in range(3):
            zeros = (0,) * len(lead)
            in_arrays.append(arr)
            in_specs.append(pl.BlockSpec((1,) + lead + (n_rows, LANES),
                                         lambda hp, b, sel=sel, zeros=zeros: (b,) + zeros + (0, sel * npair + hp)))
    in_arrays += [t1, t2, t3]
    in_specs += [pl.BlockSpec((2, 2, ATT_BLOCK, 2 * ATT_BLOCK), lambda hp, b: (0, hp, 0, 0)),
                 pl.BlockSpec((2, 2, ATT_BLOCK, 2 * ATT_BLOCK), lambda hp, b: (0, hp, 0, 0)),
                 pl.BlockSpec((2, ATT_BLOCK, ATT_BLOCK), lambda hp, b: (hp, 0, 0))]
    return pl.pallas_call(
        functools.partial(_att_kernel, seq=seq),
        grid=(npair, bsz),
        in_specs=in_specs,
        out_specs=pl.BlockSpec((1, seq, LANES), lambda hp, b: (b, 0, hp)),
        out_shape=jax.ShapeDtypeStruct((bsz, seq, ATT_OUT), BF16),
        scratch_shapes=[pltpu.VMEM((seq, LANES), F32)] * 10,
        compiler_params=_cparams("arbitrary", "arbitrary"),
    )(*in_arrays)


def _softplus(x):
    return jnp.maximum(x, 0.0) + jnp.log1p(jnp.exp(-jnp.abs(x)))


def _head_spread_table():
    gw, n_src = SSD_GW, LANES
    tab = np.zeros((2 * n_src, 2 * gw), np.float32)
    for half in range(2):
        for blk, src0 in enumerate((4, 8)):
            for col in range(gw):
                tab[half * n_src + src0 + col // SSD_HEAD_DIM, blk * gw + col] = 1.0
    return tab


def _ssd_kernel(x_ref, b_ref, c_ref, z_ref, dt_ref, spread_ref, dtb_ref, alog_ref, dsk_ref, u_ref, carry_ref,
                *, n_chunks):
    L = SSD_CHUNK
    gw = SSD_GW
    carry_ref[...] = jnp.zeros(carry_ref.shape, F32)

    a_neg = -jnp.exp(alog_ref[0])
    dtb = dtb_ref[0]
    dsk = dsk_ref[0]
    ri = lax.broadcasted_iota(jnp.int32, (L, L), 0)
    ci = lax.broadcasted_iota(jnp.int32, (L, L), 1)
    upper_incl = (ri <= ci).astype(F32)
    causal = ri >= ci
    lane = lax.broadcasted_iota(jnp.int32, (1, LANES), 1)
    low = lane < SSD_HEAD_DIM
    heads = range(SSD_HPG)

    def load(s):
        rows = pl.ds(s["r0"], L)
        s["xs"] = _silu(x_ref[0, rows, :].astype(F32))
        s["xs_b"] = s["xs"].astype(BF16)
        s["bm"] = _silu(b_ref[0, rows, :].astype(F32))
        s["cm_b"] = _silu(c_ref[0, rows, :].astype(F32)).astype(BF16)

    def decay_cumsum(s):
        dt = _softplus(dt_ref[0, 0, :, pl.ds(s["r0"], L)] + dtb)
        rows8 = jnp.concatenate([dt * a_neg, dt], axis=0)
        cs8 = jnp.dot(rows8, upper_incl, preferred_element_type=F32, precision=lax.Precision.HIGHEST)
        s["dt"], s["acs_t"] = dt, cs8[0:SSD_HPG]

    def decay_spread(s):
        dt, acs_t = s["dt"], s["acs_t"]
        t16 = jnp.concatenate([acs_t, acs_t, acs_t, dt], axis=0)
        cols = jnp.concatenate([t16, jnp.zeros((L - 16, L), F32)], axis=0).T
        dt_at8 = pltpu.roll(cols, LANES - 4, 1)
        comb = jnp.where(lane < 8, jnp.exp(cols), jnp.exp(cols[L - 1:L, :] - cols) * dt_at8)
        hi = comb.astype(BF16)
        lo = (comb - hi.astype(F32)).astype(BF16)
        s["spread"] = jnp.dot(jnp.concatenate([hi, lo], axis=1), spread_ref[...], preferred_element_type=F32)
        s["acs_cols"] = cols

    def scores(s):
        s["bm_t"] = s.pop("bm").T.astype(BF16)
        s["cb"] = jnp.dot(s["cm_b"], s["bm_t"], preferred_element_type=F32)

    def intra(s):
        spread, acs_t, dt, cb, cols = s["spread"], s.pop("acs_t"), s.pop("dt"), s.pop("cb"), s.pop("acs_cols")
        xs, xs_b = s["xs"], s.pop("xs_b")
        mixes = []
        for j in heads:
            seg = cols[:, j:j + 1] - acs_t[j:j + 1, :]
            mixes.append((jnp.exp(jnp.where(causal, seg, -jnp.inf)) * (cb * dt[j:j + 1, :])).astype(BF16))
        halves = []
        for hp in range(SSD_HPG // 2):
            yy = jnp.dot(jnp.concatenate(mixes[2 * hp:2 * hp + 2], axis=0), xs_b[:, hp * LANES:(hp + 1) * LANES],
                         preferred_element_type=F32)
            halves.append(jnp.where(low, yy[:L], yy[L:]))
        s["y"] = jnp.concatenate(halves, axis=1)
        s["st_new"] = jnp.dot(s.pop("bm_t"), (xs * spread[:, gw:]).astype(BF16),
                              preferred_element_type=F32)

    def inter(s):
        eacs_bc = s.pop("spread")[:, :gw]
        carry = carry_ref[...]
        y = s.pop("y") + jnp.dot(s.pop("cm_b"), carry.astype(BF16), preferred_element_type=F32) * eacs_bc
        carry_ref[...] = carry * eacs_bc[L - 1:L, :] + s.pop("st_new")
        y = y + s.pop("xs") * dsk
        rows = pl.ds(s["r0"], L)
        u_ref[0, rows, :] = (y * _silu(z_ref[0, rows, :].astype(F32))).astype(BF16)

    def body(i, _):
        states = [{"c": i * SSD_LOCKSTEP + k, "r0": pl.multiple_of((i * SSD_LOCKSTEP + k) * L, L)}
                  for k in range(SSD_LOCKSTEP)]
        for stage in (load, decay_cumsum, decay_spread, scores, intra, inter):
            for s in states:
                stage(s)
        return 0

    lax.fori_loop(0, n_chunks // SSD_LOCKSTEP, body, 0)


def _ssd(xbc, z, dt_t, dt_bias, a_log, d_skip, bsz, seq):
    gw = SSD_GW
    nxb = SSD_INNER // SSD_STATE
    dsk = jnp.repeat(d_skip.astype(F32), SSD_HEAD_DIM).reshape(SSD_GROUPS, 1, gw)
    dtb = dt_bias.astype(F32).reshape(SSD_GROUPS, SSD_HPG, 1)
    alog = a_log.astype(F32).reshape(SSD_GROUPS, SSD_HPG, 1)
    spread = jnp.asarray(_head_spread_table(), BF16)
    x_map = lambda b, g: (b, 0, g)
    bm_map = lambda b, g: (b, 0, nxb + g)
    cm_map = lambda b, g: (b, 0, nxb + SSD_GROUPS + g)
    return pl.pallas_call(
        functools.partial(_ssd_kernel, n_chunks=seq // SSD_CHUNK),
        grid=(bsz, SSD_GROUPS),
        in_specs=[
            pl.BlockSpec((1, seq, gw), x_map),
            pl.BlockSpec((1, seq, SSD_STATE), bm_map),
            pl.BlockSpec((1, seq, SSD_STATE), cm_map),
            pl.BlockSpec((1, seq, gw), x_map),
            pl.BlockSpec((1, 1, SSD_HPG, seq), lambda b, g: (b, g, 0, 0)),
            pl.BlockSpec(spread.shape, lambda b, g: (0, 0)),
            pl.BlockSpec((1, SSD_HPG, 1), lambda b, g: (g, 0, 0)),
            pl.BlockSpec((1, SSD_HPG, 1), lambda b, g: (g, 0, 0)),
            pl.BlockSpec((1, 1, gw), lambda b, g: (g, 0, 0)),
        ],
        out_specs=pl.BlockSpec((1, seq, gw), x_map),
        out_shape=jax.ShapeDtypeStruct((bsz, seq, SSD_INNER), BF16),
        scratch_shapes=[pltpu.VMEM((SSD_STATE, gw), F32)],
        compiler_params=_cparams("arbitrary", "arbitrary"),
    )(xbc, xbc, xbc, z, dt_t, spread, dtb, alog, dsk)


def _first_index_of_max(vals, lane_f):
    m = jnp.max(vals, axis=-1, keepdims=True)
    idx = jnp.min(jnp.where(vals == m, lane_f, float(LANES)), axis=-1, keepdims=True)
    return m, idx


def _post_kernel(att_ref, u_ref, gates_ref, x_ref, watt_ref, wssd_ref, wout_ref,
                 gssd_ref, gffn_ref, wr_ref, br_ref, x1_ref, meta_ref, cnt_ref, run_ref):
    i = pl.program_id(0)

    @pl.when(i == 0)
    def _():
        run_ref[...] = jnp.zeros(run_ref.shape, F32)

    tm = x_ref.shape[0]
    nsub = 2
    subs = [{"rows": pl.ds(k * (tm // nsub), tm // nsub)} for k in range(nsub)]

    def att_branch(s):
        s["y_att"] = jnp.dot(att_ref[s["rows"], :], watt_ref[...], preferred_element_type=F32)

    def ssd_branch(s):
        ssd = _rms(u_ref[s["rows"], :].astype(F32), gssd_ref[...])
        s["y_ssd"] = jnp.dot(ssd.astype(BF16), wssd_ref[...], preferred_element_type=F32)

    def mix_out(s):
        gates = gates_ref[s["rows"], :].astype(F32)
        merged = _sigmoid(gates[:, :D_MODEL]) * s.pop("y_att") + _sigmoid(gates[:, D_MODEL:]) * s.pop("y_ssd")
        x1 = x_ref[s["rows"], :] + jnp.dot(merged.astype(BF16), wout_ref[...], preferred_element_type=F32)
        x1_ref[s["rows"], :] = x1
        s["h2"] = _rms(x1, gffn_ref[...])

    def router(s):
        h2 = s.pop("h2")
        hi = h2.astype(BF16)
        lo = (h2 - hi.astype(F32)).astype(BF16)
        both = jnp.dot(jnp.concatenate([hi, lo], axis=1), wr_ref[...], preferred_element_type=F32)
        s["logits"] = both[:, :LANES] + both[:, LANES:]

    for stage in (att_branch, ssd_branch, mix_out, router):
        for s in subs:
            stage(s)
    logits = jnp.concatenate([s["logits"] for s in subs], axis=0) + br_ref[...]
    lane = lax.broadcasted_iota(jnp.int32, (tm, LANES), 1)
    lane_f = lane.astype(F32)
    ninf = -jnp.inf
    gl = jnp.where(lane < MOE_GROUPS, logits, ninf)
    gmax, gidx = _first_index_of_max(gl, lane_f)
    g_val = 1.0 / jnp.sum(jnp.exp(gl - gmax), axis=-1, keepdims=True)
    base = MOE_GROUPS + MOE_EPG * gidx
    el = jnp.where((lane_f >= base) & (lane_f < base + MOE_EPG), logits, ninf)
    e1, i1 = _first_index_of_max(el, lane_f)
    e2, i2 = _first_index_of_max(jnp.where(lane_f == i1, ninf, el), lane_f)
    t2 = jnp.exp(e2 - e1)
    w1 = g_val / (1.0 + t2)
    w2 = g_val * t2 / (1.0 + t2)
    a1, a2 = i1 - base, i2 - base
    lo, hi = jnp.minimum(a1, a2), jnp.maximum(a1, a2)
    c_lo = jnp.where(a1 < a2, w1, w2)
    c_hi = jnp.where(a1 < a2, w2, w1)
    pair = lo * (7.0 - lo) * 0.5 + (hi - lo - 1.0)
    cls = gidx * MOE_PAIRS + pair

    onehot = (lane_f == cls)
    oh_b = jnp.where(onehot, 1.0, 0.0).astype(BF16)
    rr = lax.broadcasted_iota(jnp.int32, (tm, tm), 0)
    cc = lax.broadcasted_iota(jnp.int32, (tm, tm), 1)
    strict = jnp.where(rr > cc, 1.0, 0.0).astype(BF16)
    prefix = jnp.dot(strict, oh_b, preferred_element_type=F32) + run_ref[...]
    rank = jnp.sum(jnp.where(onehot, prefix, 0.0), axis=-1, keepdims=True)
    run = run_ref[...] + jnp.sum(oh_b.astype(F32), axis=0, keepdims=True)
    run_ref[...] = run
    cnt_ref[...] = jnp.broadcast_to(run, cnt_ref.shape)

    meta = jnp.where(lane == 0, cls, jnp.where(lane == 1, rank, jnp.where(lane == 2, c_lo,
                     jnp.where(lane == 3, c_hi, 0.0))))
    meta_ref[...] = meta


def _post(att, u, gates, x2d, w_att, w_ssd, w_out, g_ssd, g_ffn, w_r, b_r):
    t = x2d.shape[0]
    tm = TM_POST
    row = lambda w: pl.BlockSpec((tm, w), lambda i: (i, 0))
    const = lambda a: pl.BlockSpec(a.shape, lambda i: (0,) * a.ndim)
    return pl.pallas_call(
        _post_kernel,
        grid=(t // tm,),
        in_specs=[row(ATT_OUT), row(SSD_INNER), row(2 * D_MODEL), row(D_MODEL),
                  const(w_att), const(w_ssd), const(w_out), const(g_ssd), const(g_ffn), const(w_r), const(b_r)],
        out_specs=(row(D_MODEL), row(LANES), pl.BlockSpec((8, LANES), lambda i: (0, 0))),
        out_shape=(jax.ShapeDtypeStruct((t, D_MODEL), F32),
                   jax.ShapeDtypeStruct((t, LANES), F32),
                   jax.ShapeDtypeStruct((8, LANES), F32)),
        scratch_shapes=[pltpu.VMEM((1, LANES), F32)],
        compiler_params=_cparams("arbitrary"),
    )(att, u, gates, x2d, w_att, w_ssd, w_out, g_ssd, g_ffn, w_r, b_r)


def _pack_pair(a, b):
    ua = lax.bitcast_convert_type(a.astype(BF16).astype(F32), jnp.uint32)
    ub = lax.bitcast_convert_type(b.astype(BF16).astype(F32), jnp.uint32)
    return (ua & jnp.uint32(0xFFFF0000)) | (ub >> 16)


def _unpack_pair(p):
    a = lax.bitcast_convert_type(p & jnp.uint32(0xFFFF0000), F32)
    b = lax.bitcast_convert_type(p << 16, F32)
    return a, b


def _sc_mesh():
    return plsc.VectorSubcoreMesh(core_axis_name="c", subcore_axis_name="s")


def _sc_permute_rows(data, idx, n_out, scatter):
    n, width = idx.shape[0], data.shape[1]
    mesh = _sc_mesh()
    workers = mesh.num_cores * mesh.num_subcores
    per_worker = n // SC_WINDOW // workers
    assert per_worker * workers * SC_WINDOW == n

    n_sub = SC_WINDOW // SC_SUB

    @pl.kernel(out_type=jax.ShapeDtypeStruct((n_out, width), data.dtype), mesh=mesh,
               scratch_types=[pltpu.VMEM((1, SC_WINDOW), jnp.int32), pltpu.VMEM((2, SC_SUB, width), data.dtype),
                              pltpu.SemaphoreType.DMA((2,)), pltpu.SemaphoreType.DMA((2,))])
    def permute(data_hbm, idx_hbm, out_hbm, idx_v, rows_v, sem_in, sem_out):
        worker = lax.axis_index("c") * mesh.num_subcores + lax.axis_index("s")

        @pl.loop(0, per_worker)
        def _(j):
            base = (worker * per_worker + j) * SC_WINDOW
            pltpu.sync_copy(idx_hbm.at[:, pl.ds(base, SC_WINDOW)], idx_v)

            def copies(k):
                slot = k % 2
                sub_idx = idx_v.at[0, pl.ds(k * SC_SUB, SC_SUB)]
                plain = pl.ds(base + k * SC_SUB, SC_SUB)
                src, dst = (data_hbm.at[plain], out_hbm.at[sub_idx]) if scatter else (data_hbm.at[sub_idx],
                                                                                      out_hbm.at[plain])
                return (pltpu.make_async_copy(src, rows_v.at[slot], sem_in.at[slot]),
                        pltpu.make_async_copy(rows_v.at[slot], dst, sem_out.at[slot]))

            copies(0)[0].start()
            for k in range(n_sub):
                read, write = copies(k)
                read.wait()
                if k + 1 < n_sub:
                    if k >= 1:
                        copies(k - 1)[1].wait()
                    copies(k + 1)[0].start()
                write.start()
            copies(n_sub - 2)[1].wait()
            copies(n_sub - 1)[1].wait()

    return permute(data, idx.reshape(1, n))


def _sc_scatter_rows(data, idx, n_out):
    return _sc_permute_rows(data, idx, n_out, scatter=True)


def _sc_gather_rows(data, idx):
    return _sc_permute_rows(data, idx, idx.shape[0], scatter=False)


def _expert_kernel(nused_ref, elo_ref, ehi_ref, valid_ref, xs_ref, g_ref, wg_lo, wu_lo, wd_lo, wg_hi, wu_hi, wd_hi,
                   ys_ref):
    del elo_ref, ehi_ref
    i = pl.program_id(0)

    @pl.when(i < nused_ref[0])
    def _():
        row = lax.broadcasted_iota(jnp.int32, (xs_ref.shape[0], 1), 0)
        xb = _rms(jnp.where(row < valid_ref[i], xs_ref[...], 0.0), g_ref[...]).astype(BF16)
        pre = [(jnp.dot(xb, wg[0], preferred_element_type=F32), jnp.dot(xb, wu[0], preferred_element_type=F32))
               for wg, wu in ((wg_lo, wu_lo), (wg_hi, wu_hi))]
        hid = [(_silu(gt) * up).astype(BF16) for gt, up in pre]
        y_lo, y_hi = (jnp.dot(h, wd[0], preferred_element_type=F32) for h, wd in zip(hid, (wd_lo, wd_hi)))
        ys_ref[...] = _pack_pair(y_lo, y_hi)

    @pl.when(i >= nused_ref[0])
    def _():
        ys_ref[...] = jnp.zeros(ys_ref.shape, jnp.uint32)


def _experts(n_used, tile_elo, tile_ehi, tile_valid, xs, g_ffn, w_gate, w_up, w_down):
    p_rows = xs.shape[0]
    tm = TM_EXP
    n_tiles = p_rows // tm

    def last_used(i, nu):
        return jnp.minimum(i, jnp.maximum(nu[0] - 1, 0))

    def row_map(i, nu, elo, ehi, valid):
        return (last_used(i, nu), 0)

    def wmap(which):
        def f(i, nu, elo, ehi, valid):
            return ((elo, ehi)[which][last_used(i, nu)], 0, 0)
        return f

    wspec_in = lambda which: pl.BlockSpec((1, D_MODEL, MOE_HIDDEN), wmap(which))
    wspec_out = lambda which: pl.BlockSpec((1, MOE_HIDDEN, D_MODEL), wmap(which))
    grid_spec = pltpu.PrefetchScalarGridSpec(
        num_scalar_prefetch=4,
        grid=(n_tiles,),
        in_specs=[pl.BlockSpec((tm, D_MODEL), row_map),
                  pl.BlockSpec((1, D_MODEL), lambda i, nu, elo, ehi, valid: (0, 0)),
                  wspec_in(0), wspec_in(0), wspec_out(0), wspec_in(1), wspec_in(1), wspec_out(1)],
        out_specs=pl.BlockSpec((tm, D_MODEL), lambda i, nu, elo, ehi, valid: (i, 0)),
    )
    return pl.pallas_call(
        _expert_kernel,
        grid_spec=grid_spec,
        out_shape=jax.ShapeDtypeStruct((p_rows, D_MODEL), jnp.uint32),
        compiler_params=_cparams("arbitrary"),
    )(n_used, tile_elo, tile_ehi, tile_valid, xs, g_ffn, w_gate, w_up, w_down, w_gate, w_up, w_down)


def _final_kernel(x1_ref, meta_ref, p_ref, yt_ref, gple_ref, wpg_ref, wpp_ref, gfin_ref, out_ref):
    tm = x1_ref.shape[0]
    nsub = 2
    subs = [{"rows": pl.ds(k * (tm // nsub), tm // nsub)} for k in range(nsub)]

    def embed(s):
        s["pp"] = jnp.dot(p_ref[s["rows"], :].astype(BF16), wpp_ref[...], preferred_element_type=F32)

    def combine(s):
        y_lo, y_hi = _unpack_pair(yt_ref[s["rows"], :])
        meta = meta_ref[s["rows"], :]
        s["x2"] = x1_ref[s["rows"], :] + meta[:, 2:3] * y_lo + meta[:, 3:4] * y_hi
        s["hn"] = _rms(s["x2"], gple_ref[...]).astype(BF16)

    def gate(s):
        s["gate"] = jnp.dot(s.pop("hn"), wpg_ref[...], preferred_element_type=F32)

    def finish(s):
        out_ref[s["rows"], :] = _rms(s.pop("x2") + _sigmoid(s.pop("gate")) * s.pop("pp"), gfin_ref[...])

    for stage in (embed, combine, gate, finish):
        for s in subs:
            stage(s)


def _final(x1, meta, p2d, ys_tok, g_ple, w_pg, w_pp, g_fin):
    t = x1.shape[0]
    tm = TM_FIN
    row = lambda w: pl.BlockSpec((tm, w), lambda i: (i, 0))
    const = lambda a: pl.BlockSpec(a.shape, lambda i: (0,) * a.ndim)
    return pl.pallas_call(
        _final_kernel,
        grid=(t // tm,),
        in_specs=[row(D_MODEL), row(LANES), row(PLE_DIM), row(D_MODEL),
                  const(g_ple), const(w_pg), const(w_pp), const(g_fin)],
        out_specs=row(D_MODEL),
        out_shape=jax.ShapeDtypeStruct((t, D_MODEL), F32),
        compiler_params=_cparams("arbitrary"),
    )(x1, meta, p2d, ys_tok, g_ple, w_pg, w_pp, g_fin)


_PAIR_LO = np.array([0, 0, 0, 1, 1, 2], np.int32)
_PAIR_HI = np.array([1, 2, 3, 2, 3, 3], np.int32)


def _routing_tables(meta, counts_f, n_tiles):
    cls = meta[:, 0].astype(jnp.int32)
    rank = meta[:, 1].astype(jnp.int32)
    counts = counts_f[0, :MOE_CLASSES].astype(jnp.int32)
    tiles_per = (counts + TM_EXP - 1) // TM_EXP
    tile_end = jnp.cumsum(tiles_per)
    tile_start = tile_end - tiles_per
    class_ids = jnp.arange(MOE_CLASSES, dtype=jnp.int32)
    pos = jnp.sum(jnp.where(cls[:, None] == class_ids[None, :], (tile_start * TM_EXP)[None, :], 0), axis=1) + rank
    n_used = tile_end[-1:]
    tile_ids = jnp.arange(n_tiles, dtype=jnp.int32)
    tile_cls = jnp.minimum(jnp.sum((tile_end[None, :] <= tile_ids[:, None]).astype(jnp.int32), axis=1),
                           MOE_CLASSES - 1)
    grp = tile_cls // MOE_PAIRS
    pair = tile_cls % MOE_PAIRS
    tile_elo = grp * MOE_EPG + jnp.asarray(_PAIR_LO)[pair]
    tile_ehi = grp * MOE_EPG + jnp.asarray(_PAIR_HI)[pair]
    class_left = jnp.sum(jnp.where(tile_cls[:, None] == class_ids[None, :],
                                   (counts - (tile_ids[:, None] - tile_start[None, :]) * TM_EXP), 0), axis=1)
    tile_valid = jnp.clip(class_left, 0, TM_EXP)
    return (pos.astype(jnp.int32), n_used.astype(jnp.int32), tile_elo.astype(jnp.int32),
            tile_ehi.astype(jnp.int32), tile_valid.astype(jnp.int32))


def kernel(x, p, norm_mix_g, w_in, conv_w, conv_b, dt_bias, a_log, d_skip, ssd_norm_g, w_att_branch,
           w_ssd_branch, w_out, norm_ffn_g, w_router_group, b_router_group, w_router_expert,
           b_router_expert, w_exp_gate, w_exp_up, w_exp_down, norm_ple_g, w_ple_gate, w_ple_proj,
           final_norm_g):
    bsz, seq, _ = x.shape
    t = bsz * seq
    assert w_in.shape[0] == 1, "single-layer block"
    assert seq // ATT_PATTERNS[-1][1] == ATT_BLOCK and seq % TM_IN == 0
    x2d = x.reshape(t, D_MODEL)

    wi = w_in[0]
    c_dt = QKV_W + SSD_INNER + SSD_CONV_CH
    w_qkv = wi[:, :QKV_W].astype(BF16)
    w_wide = jnp.concatenate([wi[:, QKV_W:c_dt], wi[:, c_dt + SSD_HEADS:]], axis=1).astype(BF16)
    w_dt = jnp.pad(wi[:, c_dt:c_dt + SSD_HEADS], ((0, 0), (0, LANES - SSD_HEADS))).astype(BF16)
    row = lambda v: v.reshape(1, -1).astype(F32)

    qkv1, qkv2, qkv3, z, xbc, gates, dt_raw = _in_proj(x2d, row(norm_mix_g[0]), w_qkv, w_wide, w_dt,
                                                       conv_w[0], conv_b[0], bsz, seq)

    att = _attention(qkv1.reshape(bsz, seq, ATT_WIDTH), qkv2, qkv3, bsz, seq).reshape(t, ATT_OUT)

    dt_t = dt_raw[:, :SSD_HEADS].reshape(bsz, seq, SSD_GROUPS, SSD_HPG).transpose(0, 2, 3, 1)
    u = _ssd(xbc.reshape(bsz, seq, SSD_CONV_CH), z.reshape(bsz, seq, SSD_INNER), dt_t,
             dt_bias[0], a_log[0], d_skip[0], bsz, seq).reshape(t, SSD_INNER)

    w_r32 = jnp.pad(jnp.concatenate([w_router_group[0], w_router_expert[0]], axis=1),
                    ((0, 0), (0, LANES - MOE_GROUPS - MOE_EXPERTS))).astype(F32)
    w_r_hi = w_r32.astype(BF16)
    w_r_lo = (w_r32 - w_r_hi.astype(F32)).astype(BF16)
    w_r = jnp.concatenate([jnp.concatenate([w_r_hi, w_r_lo], axis=1),
                           jnp.concatenate([w_r_hi, jnp.zeros_like(w_r_lo)], axis=1)], axis=0)
    b_r = jnp.pad(jnp.concatenate([b_router_group[0], b_router_expert[0]]),
                  (0, LANES - MOE_GROUPS - MOE_EXPERTS)).reshape(1, LANES).astype(F32)
    x1, meta, counts = _post(att, u, gates, x2d,
                             w_att_branch[0].astype(BF16), w_ssd_branch[0].astype(BF16), w_out[0].astype(BF16),
                             row(ssd_norm_g[0]), row(norm_ffn_g[0]), w_r, b_r)

    n_tiles = t // TM_EXP + MOE_CLASSES
    pos, n_used, tile_elo, tile_ehi, tile_valid = _routing_tables(meta, counts, n_tiles)
    xs = _sc_scatter_rows(x1, pos, n_tiles * TM_EXP)
    ys = _experts(n_used, tile_elo, tile_ehi, tile_valid, xs, row(norm_ffn_g[0]),
                  w_exp_gate[0].astype(BF16), w_exp_up[0].astype(BF16), w_exp_down[0].astype(BF16))
    ys_tok = _sc_gather_rows(ys, pos)
    out = _final(x1, meta, p[0].reshape(t, PLE_DIM), ys_tok, row(norm_ple_g[0]),
                 w_ple_gate[0].astype(BF16), w_ple_proj[0].astype(BF16), row(final_norm_g))
    return out.reshape(bsz, seq, D_MODEL)
```

```python
import functools

import numpy as np
import jax
import jax.numpy as jnp
from jax import lax
from jax.experimental import pallas as pl
from jax.experimental.pallas import tpu as pltpu
from jax.experimental.pallas import tpu_sc as plsc

F32 = jnp.float32
BF16 = jnp.bfloat16

D_MODEL = 1024
PLE_DIM = 256
RMS_EPS = 1e-6

ATT_PATTERNS = ((128, 1), (512, 4), (2048, 16))
ATT_GROUPS = 3
ATT_HPG = 8
ATT_HEAD_DIM = 64
ATT_WIDTH = ATT_GROUPS * ATT_HPG * ATT_HEAD_DIM
ATT_OUT = ATT_HPG * ATT_HEAD_DIM
ATT_BLOCK = 128
ATT_LOCKSTEP = 2
ALIBI_MAX_BIAS = 8.0
QKV_W = 3 * ATT_WIDTH

SSD_INNER = 2048
SSD_HEADS = 32
SSD_GROUPS = 8
SSD_HPG = 4
SSD_HEAD_DIM = 64
SSD_STATE = 128
SSD_CONV = 4
SSD_CHUNK = 128
SSD_CONV_CH = SSD_INNER + 2 * SSD_GROUPS * SSD_STATE
SSD_GW = SSD_HPG * SSD_HEAD_DIM
SSD_LOCKSTEP = 8

MOE_GROUPS = 4
MOE_EPG = 4
MOE_EXPERTS = 16
MOE_HIDDEN = 512
MOE_PAIRS = 6
MOE_CLASSES = MOE_GROUPS * MOE_PAIRS

LANES = 128
NEG_BIG = -1e30
VMEM_LIMIT = 56 * 1024 * 1024

TM_IN = 1024
TN_IN = 512
TN_WIDE = 1024
TM_POST = 512
TM_EXP = 256
TM_FIN = 512
SC_WINDOW = 128
SC_SUB = 32


def _cparams(*sem):
    return pltpu.CompilerParams(dimension_semantics=sem, vmem_limit_bytes=VMEM_LIMIT)


def _sigmoid(x):
    return 0.5 * jnp.tanh(0.5 * x) + 0.5


def _silu(x):
    h = 0.5 * x
    return h + h * jnp.tanh(h)


def _rms(x, g):
    ms = jnp.mean(x * x, axis=-1, keepdims=True)
    return x * lax.rsqrt(ms + RMS_EPS) * g


_IN_SEG = (3, 3, 3, SSD_INNER // TN_WIDE, SSD_CONV_CH // TN_WIDE, 2 * D_MODEL // TN_WIDE)
_IN_START = tuple(int(v) for v in np.cumsum((0,) + _IN_SEG))
_IN_NARROW = _IN_START[3]


def _inproj_kernel(x_ref, g_ref, wq_ref, ww_ref, wdt_ref, cw_ref, cb_ref, qkv1_ref, qkv2_ref, qkv3_ref, z_ref,
                   xbc_ref, gates_ref, dt_ref, h_ref, hcol_ref, halo_ref, *, per_seq):
    j = pl.program_id(1)
    tm = x_ref.shape[0]

    @pl.when(j == 0)
    def _():
        h = _rms(x_ref[...], g_ref[...])
        hb = h.astype(BF16)
        h_ref[0] = hb
        dt_ref[...] = jnp.dot(hb, wdt_ref[...], preferred_element_type=F32)
        ncb = hcol_ref.shape[0]
        for c in range(ncb):
            hcol_ref[c] = h[:, c * LANES:(c + 1) * LANES]
        for slot, (_, dil) in enumerate(ATT_PATTERNS[1:], start=1):
            rows = tm // dil
            for r in range(dil):
                for c in range(ncb):
                    h_ref[slot, r * rows:(r + 1) * rows, c * LANES:(c + 1) * LANES] = (
                        hcol_ref[c, pl.ds(r, rows, stride=dil), :].astype(BF16))

    def segment(k, fn):
        @pl.when((j >= _IN_START[k]) & (j < _IN_START[k + 1]))
        def _():
            fn()

    def narrow(slot, store):
        def fn():
            store(jnp.dot(h_ref[slot], wq_ref[...], preferred_element_type=F32).astype(BF16))
        return fn

    def wide(ref):
        def fn():
            ref[...] = jnp.dot(h_ref[0], ww_ref[...], preferred_element_type=F32).astype(BF16)
        return fn

    def store_plain(ref):
        def store(res):
            ref[...] = res
        return store

    def store_grouped(ref, dil):
        def store(res):
            ref[0] = res.reshape(dil, tm // dil, res.shape[1])
        return store

    def conv_silu():
        jb = j - _IN_START[4]
        nsub = 4
        rs = tm // nsub
        res = [jnp.dot(h_ref[0, k * rs:(k + 1) * rs, :], ww_ref[...], preferred_element_type=F32)
               for k in range(nsub)]
        seq_start = pl.program_id(0) % per_seq == 0
        prev = jnp.where(seq_start, 0.0, halo_ref[jb])
        row8 = lax.broadcasted_iota(jnp.int32, (8, 1), 0)
        cw = cw_ref[...]
        for k in range(nsub):
            acc = cb_ref[...] + cw[SSD_CONV - 1:SSD_CONV, :] * res[k]
            for sh in range(1, SSD_CONV):
                rolled = pltpu.roll(res[k], sh, 0)
                top = jnp.where(row8 < sh, pltpu.roll(prev, sh, 0), rolled[0:8])
                shifted = jnp.concatenate([top, rolled[8:]], axis=0)
                acc = acc + cw[SSD_CONV - 1 - sh:SSD_CONV - sh, :] * shifted
            xbc_ref[k * rs:(k + 1) * rs, :] = acc.astype(BF16)
            prev = res[k][rs - 8:rs]
        halo_ref[jb] = prev

    segment(0, narrow(0, store_plain(qkv1_ref)))
    segment(1, narrow(1, store_grouped(qkv2_ref, ATT_PATTERNS[1][1])))
    segment(2, narrow(2, store_grouped(qkv3_ref, ATT_PATTERNS[2][1])))
    segment(3, wide(z_ref))
    segment(4, conv_silu)
    segment(5, wide(gates_ref))


def _in_proj(x2d, g, w_qkv, w_wide, w_dt, conv_w, conv_b, bsz, seq):
    t = x2d.shape[0]
    nj = _IN_START[-1]
    per_seq = seq // TM_IN
    d2, d3 = ATT_PATTERNS[1][1], ATT_PATTERNS[2][1]

    def seg_map(k):
        return lambda i, j: (i, jnp.clip(j - _IN_START[k], 0, _IN_SEG[k] - 1))

    def grp_map(k):
        return lambda i, j: (i // per_seq, 0, i % per_seq, jnp.clip(j - _IN_START[k], 0, _IN_SEG[k] - 1))

    def wq_map(i, j):
        jj = jnp.minimum(j, _IN_NARROW - 1)
        return (0, (jj % 3) * ATT_GROUPS + jj // 3)

    out_shapes = (
        jax.ShapeDtypeStruct((t, ATT_WIDTH), BF16),
        jax.ShapeDtypeStruct((bsz, d2, seq // d2, ATT_WIDTH), BF16),
        jax.ShapeDtypeStruct((bsz, d3, seq // d3, ATT_WIDTH), BF16),
        jax.ShapeDtypeStruct((t, SSD_INNER), BF16),
        jax.ShapeDtypeStruct((t, SSD_CONV_CH), BF16),
        jax.ShapeDtypeStruct((t, 2 * D_MODEL), BF16),
        jax.ShapeDtypeStruct((t, LANES), F32),
    )
    out_specs = (
        pl.BlockSpec((TM_IN, TN_IN), seg_map(0)),
        pl.BlockSpec((1, d2, TM_IN // d2, TN_IN), grp_map(1)),
        pl.BlockSpec((1, d3, TM_IN // d3, TN_IN), grp_map(2)),
        pl.BlockSpec((TM_IN, TN_WIDE), seg_map(3)),
        pl.BlockSpec((TM_IN, TN_WIDE), seg_map(4)),
        pl.BlockSpec((TM_IN, TN_WIDE), seg_map(5)),
        pl.BlockSpec((TM_IN, LANES), lambda i, j: (i, 0)),
    )
    conv_map = lambda i, j: (0, jnp.clip(j - _IN_START[4], 0, _IN_SEG[4] - 1))
    return pl.pallas_call(
        functools.partial(_inproj_kernel, per_seq=per_seq),
        grid=(t // TM_IN, nj),
        in_specs=[
            pl.BlockSpec((TM_IN, D_MODEL), lambda i, j: (i, 0)),
            pl.BlockSpec((1, D_MODEL), lambda i, j: (0, 0)),
            pl.BlockSpec((D_MODEL, TN_IN), wq_map),
            pl.BlockSpec((D_MODEL, TN_WIDE), lambda i, j: (0, jnp.maximum(j - _IN_NARROW, 0))),
            pl.BlockSpec((D_MODEL, LANES), lambda i, j: (0, 0)),
            pl.BlockSpec((SSD_CONV, TN_WIDE), conv_map),
            pl.BlockSpec((1, TN_WIDE), conv_map),
        ],
        out_specs=out_specs,
        out_shape=out_shapes,
        scratch_shapes=[pltpu.VMEM((3, TM_IN, D_MODEL), BF16),
                        pltpu.VMEM((D_MODEL // LANES, TM_IN, LANES), F32),
                        pltpu.VMEM((_IN_SEG[4], 8, TN_WIDE), F32)],
        compiler_params=_cparams("arbitrary", "arbitrary"),
    )(x2d, g, w_qkv, w_wide, w_dt, conv_w.astype(F32), conv_b.reshape(1, SSD_CONV_CH).astype(F32))


def _att_bias_tables():
    h = np.arange(1, ATT_GROUPS * ATT_HPG + 1, dtype=np.float32)
    slopes = np.exp2(-ALIBI_MAX_BIAS * h / (ATT_GROUPS * ATT_HPG)).astype(np.float32).reshape(ATT_GROUPS, ATT_HPG)
    qi = np.arange(ATT_BLOCK)[:, None] + ATT_BLOCK
    kj = np.arange(2 * ATT_BLOCK)[None, :]
    delta = qi - kj
    tabs = []
    for g, (window, dil) in enumerate(ATT_PATTERNS):
        span = window // dil
        band = (delta >= 0) & (delta <= span)
        bias = (-slopes[g][:, None, None] * (delta * dil).astype(np.float32)[None]).astype(np.float32)
        with_prev = np.where(band[None], bias, np.float32(NEG_BIG))
        first = np.where((band & (kj >= ATT_BLOCK))[None], bias, np.float32(NEG_BIG))
        tabs.append(np.stack([first, with_prev]).astype(np.float32))
    return tabs[0], tabs[1], tabs[2][1][:, :, ATT_BLOCK:]


def _att_units(units):
    lane = lax.broadcasted_iota(jnp.int32, (1, LANES), 1)
    low = lane < ATT_HEAD_DIM
    scale = ATT_HEAD_DIM ** -0.5
    qmask = (jnp.where(low, scale, 0.0).astype(BF16), jnp.where(low, 0.0, scale).astype(BF16))
    heads = [(q, k2, v2, tab_fn, hh) for q, k2, v2, tab_fn in units for hh in range(2)]
    scores = [lax.dot_general(q * qmask[hh], k2, (((1,), (1,)), ((), ())), preferred_element_type=F32)
              for q, k2, _, _, hh in heads]
    probs = []
    for s, (_, _, _, tab_fn, hh) in zip(scores, heads):
        s = s + tab_fn(hh)
        m = jnp.max(s, axis=-1, keepdims=True)
        e = jnp.exp(s - m)
        probs.append((e.astype(BF16), m, jnp.sum(e, axis=-1, keepdims=True)))
    pvs = [jnp.dot(e, v2, preferred_element_type=F32) for (e, _, _), (_, _, v2, _, _) in zip(probs, heads)]
    outs = [pv / den for pv, (_, _, den) in zip(pvs, probs)]
    lses = [m + jnp.log(den) for _, m, den in probs]
    return [(jnp.where(low, outs[2 * i], outs[2 * i + 1]), jnp.where(low, lses[2 * i], lses[2 * i + 1]))
            for i in range(len(units))]


def _att_kernel(q1, k1, v1, q2, k2, v2, q3, k3, v3, t1, t2, t3, out_ref,
                o1, l1, o2p, l2p, o2n, l2n, o3p, l3p, o3n, l3n, *, seq):
    blk = ATT_BLOCK
    d2, d3 = ATT_PATTERNS[1][1], ATT_PATTERNS[2][1]
    nb1, nb2 = seq // blk, seq // d2 // blk

    def rows(n):
        return pl.ds(pl.multiple_of(n * blk, blk), blk)

    assert nb1 == d2 * nb2 == d3

    def unit_body(i, _):
        units, dests = [], []
        for k in range(ATT_LOCKSTEP):
            u = i * ATT_LOCKSTEP + k
            cur, prv = rows(u), rows(jnp.maximum(u - 1, 0))
            sel = jnp.minimum(u, 1)
            units.append((q1[0, cur, :], jnp.concatenate([k1[0, prv, :], k1[0, cur, :]], axis=0),
                          jnp.concatenate([v1[0, prv, :], v1[0, cur, :]], axis=0),
                          lambda hh, sel=sel: t1[sel, hh]))
            r, n = u // nb2, u % nb2
            cur2, prv2 = rows(n), rows(jnp.maximum(n - 1, 0))
            sel2 = jnp.minimum(n, 1)
            units.append((q2[0, r, cur2, :], jnp.concatenate([k2[0, r, prv2, :], k2[0, r, cur2, :]], axis=0),
                          jnp.concatenate([v2[0, r, prv2, :], v2[0, r, cur2, :]], axis=0),
                          lambda hh, sel2=sel2: t2[sel2, hh]))
            units.append((q3[0, u], k3[0, u], v3[0, u], lambda hh: t3[hh]))
            dests += [(o1, l1, cur), (o2p, l2p, cur), (o3p, l3p, cur)]
        for (o, l), (o_ref, l_ref, where) in zip(_att_units(units), dests):
            o_ref[where, :] = o
            l_ref[where, :] = l
        return 0

    lax.fori_loop(0, nb1 // ATT_LOCKSTEP, unit_body, 0)

    for dil, pairs in ((d2, ((o2p, o2n), (l2p, l2n))), (d3, ((o3p, o3n), (l3p, l3n)))):
        n_sub = seq // dil
        for r in range(dil):
            for src, dst in pairs:
                dst[pl.ds(r, n_sub, stride=dil), :] = src[r * n_sub:(r + 1) * n_sub, :]

    mrows = 2 * blk

    def merge(c, _):
        rr = pl.ds(pl.multiple_of(c * mrows, mrows), mrows)
        la, lb, lc = l1[rr, :], l2n[rr, :], l3n[rr, :]
        lm = jnp.maximum(jnp.maximum(la, lb), lc)
        ea, eb, ec = jnp.exp(la - lm), jnp.exp(lb - lm), jnp.exp(lc - lm)
        att = (ea * o1[rr, :] + eb * o2n[rr, :] + ec * o3n[rr, :]) / (ea + eb + ec)
        out_ref[0, rr, :] = att.astype(BF16)
        return 0

    lax.fori_loop(0, seq // mrows, merge, 0)


def _attention(qkv1, qkv2, qkv3, bsz, seq):
    t1, t2, t3 = (jnp.asarray(t) for t in _att_bias_tables())
    d2, d3 = ATT_PATTERNS[1][1], ATT_PATTERNS[2][1]
    npair = ATT_HPG // 2
    in_arrays, in_specs = [], []
    for arr, lead in ((qkv1, ()), (qkv2, (d2,)), (qkv3, (d3,))):
        n_rows = arr.shape[-2]
        for sel in range(3):
            zeros = (0,) * len(lead)
            in_arrays.append(arr)
            in_specs.append(pl.BlockSpec((1,) + lead + (n_rows, LANES),
                                         lambda hp, b, sel=sel, zeros=zeros: (b,) + zeros + (0, sel * npair + hp)))
    in_arrays += [t1, t2, t3]
    in_specs += [pl.BlockSpec((2, 2, ATT_BLOCK, 2 * ATT_BLOCK), lambda hp, b: (0, hp, 0, 0)),
                 pl.BlockSpec((2, 2, ATT_BLOCK, 2 * ATT_BLOCK), lambda hp, b: (0, hp, 0, 0)),
                 pl.BlockSpec((2, ATT_BLOCK, ATT_BLOCK), lambda hp, b: (hp, 0, 0))]
    return pl.pallas_call(
        functools.partial(_att_kernel, seq=seq),
        grid=(npair, bsz),
        in_specs=in_specs,
        out_specs=pl.BlockSpec((1, seq, LANES), lambda hp, b: (b, 0, hp)),
        out_shape=jax.ShapeDtypeStruct((bsz, seq, ATT_OUT), BF16),
        scratch_shapes=[pltpu.VMEM((seq, LANES), F32)] * 10,
        compiler_params=_cparams("arbitrary", "arbitrary"),
    )(*in_arrays)


def _softplus(x):
    return jnp.maximum(x, 0.0) + jnp.log1p(jnp.exp(-jnp.abs(x)))


def _head_spread_table():
    gw, n_src = SSD_GW, LANES
    tab = np.zeros((2 * n_src, 2 * gw), np.float32)
    for half in range(2):
        for blk, src0 in enumerate((4, 8)):
            for col in range(gw):
                tab[half * n_src + src0 + col // SSD_HEAD_DIM, blk * gw + col] = 1.0
    return tab


def _ssd_kernel(x_ref, b_ref, c_ref, z_ref, dt_ref, spread_ref, dtb_ref, alog_ref, dsk_ref, u_ref, carry_ref,
                *, n_chunks):
    L = SSD_CHUNK
    gw = SSD_GW
    carry_ref[...] = jnp.zeros(carry_ref.shape, F32)

    a_neg = -jnp.exp(alog_ref[0])
    dtb = dtb_ref[0]
    dsk = dsk_ref[0]
    ri = lax.broadcasted_iota(jnp.int32, (L, L), 0)
    ci = lax.broadcasted_iota(jnp.int32, (L, L), 1)
    upper_incl = (ri <= ci).astype(F32)
    causal = ri >= ci
    lane = lax.broadcasted_iota(jnp.int32, (1, LANES), 1)
    low = lane < SSD_HEAD_DIM
    heads = range(SSD_HPG)

    def load(s):
        rows = pl.ds(s["r0"], L)
        s["xs"] = _silu(x_ref[0, rows, :].astype(F32))
        s["xs_b"] = s["xs"].astype(BF16)
        s["bm"] = _silu(b_ref[0, rows, :].astype(F32))
        s["cm_b"] = _silu(c_ref[0, rows, :].astype(F32)).astype(BF16)

    def decay_cumsum(s):
        dt = _softplus(dt_ref[0, 0, :, pl.ds(s["r0"], L)] + dtb)
        rows8 = jnp.concatenate([dt * a_neg, dt], axis=0)
        cs8 = jnp.dot(rows8, upper_incl, preferred_element_type=F32, precision=lax.Precision.HIGHEST)
        s["dt"], s["acs_t"] = dt, cs8[0:SSD_HPG]

    def decay_spread(s):
        dt, acs_t = s["dt"], s["acs_t"]
        t16 = jnp.concatenate([acs_t, acs_t, acs_t, dt], axis=0)
        cols = jnp.concatenate([t16, jnp.zeros((L - 16, L), F32)], axis=0).T
        dt_at8 = pltpu.roll(cols, LANES - 4, 1)
        comb = jnp.where(lane < 8, jnp.exp(cols), jnp.exp(cols[L - 1:L, :] - cols) * dt_at8)
        hi = comb.astype(BF16)
        lo = (comb - hi.astype(F32)).astype(BF16)
        s["spread"] = jnp.dot(jnp.concatenate([hi, lo], axis=1), spread_ref[...], preferred_element_type=F32)
        s["acs_cols"] = cols

    def scores(s):
        s["bm_t"] = s.pop("bm").T.astype(BF16)
        s["cb"] = jnp.dot(s["cm_b"], s["bm_t"], preferred_element_type=F32)

    def intra(s):
        spread, acs_t, dt, cb, cols = s["spread"], s.pop("acs_t"), s.pop("dt"), s.pop("cb"), s.pop("acs_cols")
        xs, xs_b = s["xs"], s.pop("xs_b")
        mixes = []
        for j in heads:
            seg = cols[:, j:j + 1] - acs_t[j:j + 1, :]
            mixes.append((jnp.exp(jnp.where(causal, seg, -jnp.inf)) * (cb * dt[j:j + 1, :])).astype(BF16))
        halves = []
        for hp in range(SSD_HPG // 2):
            yy = jnp.dot(jnp.concatenate(mixes[2 * hp:2 * hp + 2], axis=0), xs_b[:, hp * LANES:(hp + 1) * LANES],
                         preferred_element_type=F32)
            halves.append(jnp.where(low, yy[:L], yy[L:]))
        s["y"] = jnp.concatenate(halves, axis=1)
        s["st_new"] = jnp.dot(s.pop("bm_t"), (xs * spread[:, gw:]).astype(BF16),
                              preferred_element_type=F32)

    def inter(s):
        eacs_bc = s.pop("spread")[:, :gw]
        carry = carry_ref[...]
        y = s.pop("y") + jnp.dot(s.pop("cm_b"), carry.astype(BF16), preferred_element_type=F32) * eacs_bc
        carry_ref[...] = carry * eacs_bc[L - 1:L, :] + s.pop("st_new")
        y = y + s.pop("xs") * dsk
        rows = pl.ds(s["r0"], L)
        u_ref[0, rows, :] = (y * _silu(z_ref[0, rows, :].astype(F32))).astype(BF16)

    def body(i, _):
        states = [{"c": i * SSD_LOCKSTEP + k, "r0": pl.multiple_of((i * SSD_LOCKSTEP + k) * L, L)}
                  for k in range(SSD_LOCKSTEP)]
        for stage in (load, decay_cumsum, decay_spread, scores, intra, inter):
            for s in states:
                stage(s)
        return 0

    lax.fori_loop(0, n_chunks // SSD_LOCKSTEP, body, 0)


def _ssd(xbc, z, dt_t, dt_bias, a_log, d_skip, bsz, seq):
    gw = SSD_GW
    nxb = SSD_INNER // SSD_STATE
    dsk = jnp.repeat(d_skip.astype(F32), SSD_HEAD_DIM).reshape(SSD_GROUPS, 1, gw)
    dtb = dt_bias.astype(F32).reshape(SSD_GROUPS, SSD_HPG, 1)
    alog = a_log.astype(F32).reshape(SSD_GROUPS, SSD_HPG, 1)
    spread = jnp.asarray(_head_spread_table(), BF16)
    x_map = lambda b, g: (b, 0, g)
    bm_map = lambda b, g: (b, 0, nxb + g)
    cm_map = lambda b, g: (b, 0, nxb + SSD_GROUPS + g)
    return pl.pallas_call(
        functools.partial(_ssd_kernel, n_chunks=seq // SSD_CHUNK),
        grid=(bsz, SSD_GROUPS),
        in_specs=[
            pl.BlockSpec((1, seq, gw), x_map),
            pl.BlockSpec((1, seq, SSD_STATE), bm_map),
            pl.BlockSpec((1, seq, SSD_STATE), cm_map),
            pl.BlockSpec((1, seq, gw), x_map),
            pl.BlockSpec((1, 1, SSD_HPG, seq), lambda b, g: (b, g, 0, 0)),
            pl.BlockSpec(spread.shape, lambda b, g: (0, 0)),
            pl.BlockSpec((1, SSD_HPG, 1), lambda b, g: (g, 0, 0)),
            pl.BlockSpec((1, SSD_HPG, 1), lambda b, g: (g, 0, 0)),
            pl.BlockSpec((1, 1, gw), lambda b, g: (g, 0, 0)),
        ],
        out_specs=pl.BlockSpec((1, seq, gw), x_map),
        out_shape=jax.ShapeDtypeStruct((bsz, seq, SSD_INNER), BF16),
        scratch_shapes=[pltpu.VMEM((SSD_STATE, gw), F32)],
        compiler_params=_cparams("arbitrary", "arbitrary"),
    )(xbc, xbc, xbc, z, dt_t, spread, dtb, alog, dsk)


def _first_index_of_max(vals, lane_f):
    m = jnp.max(vals, axis=-1, keepdims=True)
    idx = jnp.min(jnp.where(vals == m, lane_f, float(LANES)), axis=-1, keepdims=True)
    return m, idx


def _post_kernel(att_ref, u_ref, gates_ref, x_ref, watt_ref, wssd_ref, wout_ref,
                 gssd_ref, gffn_ref, wr_ref, br_ref, x1_ref, meta_ref, cnt_ref, run_ref):
    i = pl.program_id(0)

    @pl.when(i == 0)
    def _():
        run_ref[...] = jnp.zeros(run_ref.shape, F32)

    tm = x_ref.shape[0]
    nsub = 2
    subs = [{"rows": pl.ds(k * (tm // nsub), tm // nsub)} for k in range(nsub)]

    def att_branch(s):
        s["y_att"] = jnp.dot(att_ref[s["rows"], :], watt_ref[...], preferred_element_type=F32)

    def ssd_branch(s):
        ssd = _rms(u_ref[s["rows"], :].astype(F32), gssd_ref[...])
        s["y_ssd"] = jnp.dot(ssd.astype(BF16), wssd_ref[...], preferred_element_type=F32)

    def mix_out(s):
        gates = gates_ref[s["rows"], :].astype(F32)
        merged = _sigmoid(gates[:, :D_MODEL]) * s.pop("y_att") + _sigmoid(gates[:, D_MODEL:]) * s.pop("y_ssd")
        x1 = x_ref[s["rows"], :] + jnp.dot(merged.astype(BF16), wout_ref[...], preferred_element_type=F32)
        x1_ref[s["rows"], :] = x1
        s["h2"] = _rms(x1, gffn_ref[...])

    def router(s):
        h2 = s.pop("h2")
        hi = h2.astype(BF16)
        lo = (h2 - hi.astype(F32)).astype(BF16)
        both = jnp.dot(jnp.concatenate([hi, lo], axis=1), wr_ref[...], preferred_element_type=F32)
        s["logits"] = both[:, :LANES] + both[:, LANES:]

    for stage in (att_branch, ssd_branch, mix_out, router):
        for s in subs:
            stage(s)
    logits = jnp.concatenate([s["logits"] for s in subs], axis=0) + br_ref[...]
    lane = lax.broadcasted_iota(jnp.int32, (tm, LANES), 1)
    lane_f = lane.astype(F32)
    ninf = -jnp.inf
    gl = jnp.where(lane < MOE_GROUPS, logits, ninf)
    gmax, gidx = _first_index_of_max(gl, lane_f)
    g_val = 1.0 / jnp.sum(jnp.exp(gl - gmax), axis=-1, keepdims=True)
    base = MOE_GROUPS + MOE_EPG * gidx
    el = jnp.where((lane_f >= base) & (lane_f < base + MOE_EPG), logits, ninf)
    e1, i1 = _first_index_of_max(el, lane_f)
    e2, i2 = _first_index_of_max(jnp.where(lane_f == i1, ninf, el), lane_f)
    t2 = jnp.exp(e2 - e1)
    w1 = g_val / (1.0 + t2)
    w2 = g_val * t2 / (1.0 + t2)
    a1, a2 = i1 - base, i2 - base
    lo, hi = jnp.minimum(a1, a2), jnp.maximum(a1, a2)
    c_lo = jnp.where(a1 < a2, w1, w2)
    c_hi = jnp.where(a1 < a2, w2, w1)
    pair = lo * (7.0 - lo) * 0.5 + (hi - lo - 1.0)
    cls = gidx * MOE_PAIRS + pair

    onehot = (lane_f == cls)
    oh_b = jnp.where(onehot, 1.0, 0.0).astype(BF16)
    rr = lax.broadcasted_iota(jnp.int32, (tm, tm), 0)
    cc = lax.broadcasted_iota(jnp.int32, (tm, tm), 1)
    strict = jnp.where(rr > cc, 1.0, 0.0).astype(BF16)
    prefix = jnp.dot(strict, oh_b, preferred_element_type=F32) + run_ref[...]
    rank = jnp.sum(jnp.where(onehot, prefix, 0.0), axis=-1, keepdims=True)
    run = run_ref[...] + jnp.sum(oh_b.astype(F32), axis=0, keepdims=True)
    run_ref[...] = run
    cnt_ref[...] = jnp.broadcast_to(run, cnt_ref.shape)

    meta = jnp.where(lane == 0, cls, jnp.where(lane == 1, rank, jnp.where(lane == 2, c_lo,
                     jnp.where(lane == 3, c_hi, 0.0))))
    meta_ref[...] = meta


def _post(att, u, gates, x2d, w_att, w_ssd, w_out, g_ssd, g_ffn, w_r, b_r):
    t = x2d.shape[0]
    tm = TM_POST
    row = lambda w: pl.BlockSpec((tm, w), lambda i: (i, 0))
    const = lambda a: pl.BlockSpec(a.shape, lambda i: (0,) * a.ndim)
    return pl.pallas_call(
        _post_kernel,
        grid=(t // tm,),
        in_specs=[row(ATT_OUT), row(SSD_INNER), row(2 * D_MODEL), row(D_MODEL),
                  const(w_att), const(w_ssd), const(w_out), const(g_ssd), const(g_ffn), const(w_r), const(b_r)],
        out_specs=(row(D_MODEL), row(LANES), pl.BlockSpec((8, LANES), lambda i: (0, 0))),
        out_shape=(jax.ShapeDtypeStruct((t, D_MODEL), F32),
                   jax.ShapeDtypeStruct((t, LANES), F32),
                   jax.ShapeDtypeStruct((8, LANES), F32)),
        scratch_shapes=[pltpu.VMEM((1, LANES), F32)],
        compiler_params=_cparams("arbitrary"),
    )(att, u, gates, x2d, w_att, w_ssd, w_out, g_ssd, g_ffn, w_r, b_r)


def _pack_pair(a, b):
    ua = lax.bitcast_convert_type(a.astype(BF16).astype(F32), jnp.uint32)
    ub = lax.bitcast_convert_type(b.astype(BF16).astype(F32), jnp.uint32)
    return (ua & jnp.uint32(0xFFFF0000)) | (ub >> 16)


def _unpack_pair(p):
    a = lax.bitcast_convert_type(p & jnp.uint32(0xFFFF0000), F32)
    b = lax.bitcast_convert_type(p << 16, F32)
    return a, b


def _sc_mesh():
    return plsc.VectorSubcoreMesh(core_axis_name="c", subcore_axis_name="s")


def _sc_permute_rows(data, idx, n_out, scatter):
    n, width = idx.shape[0], data.shape[1]
    mesh = _sc_mesh()
    workers = mesh.num_cores * mesh.num_subcores
    per_worker = n // SC_WINDOW // workers
    assert per_worker * workers * SC_WINDOW == n

    n_sub = SC_WINDOW // SC_SUB

    @pl.kernel(out_type=jax.ShapeDtypeStruct((n_out, width), data.dtype), mesh=mesh,
               scratch_types=[pltpu.VMEM((1, SC_WINDOW), jnp.int32), pltpu.VMEM((2, SC_SUB, width), data.dtype),
                              pltpu.SemaphoreType.DMA((2,)), pltpu.SemaphoreType.DMA((2,))])
    def permute(data_hbm, idx_hbm, out_hbm, idx_v, rows_v, sem_in, sem_out):
        worker = lax.axis_index("c") * mesh.num_subcores + lax.axis_index("s")

        @pl.loop(0, per_worker)
        def _(j):
            base = (worker * per_worker + j) * SC_WINDOW
            pltpu.sync_copy(idx_hbm.at[:, pl.ds(base, SC_WINDOW)], idx_v)

            def ends(k):
                sub_idx = idx_v.at[0, pl.ds(k * SC_SUB, SC_SUB)]
                plain = pl.ds(base + k * SC_SUB, SC_SUB)
                return (data_hbm.at[plain], out_hbm.at[sub_idx]) if scatter else (data_hbm.at[sub_idx],
                                                                                  out_hbm.at[plain])

            def read(k):
                return pltpu.make_async_copy(ends(k)[0], rows_v.at[k % 2], sem_in.at[k % 2])

            def write(k):
                return pltpu.make_async_copy(rows_v.at[k % 2], ends(k)[1], sem_out.at[k % 2])

            read(0).start()
            for k in range(n_sub):
                read(k).wait()
                if k + 1 < n_sub:
                    if k >= 1:
                        write(k - 1).wait()
                    read(k + 1).start()
                write(k).start()
            write(n_sub - 2).wait()
            write(n_sub - 1).wait()

    return permute(data, idx.reshape(1, n))


def _sc_scatter_rows(data, idx, n_out):
    return _sc_permute_rows(data, idx, n_out, scatter=True)


def _sc_gather_rows(data, idx):
    return _sc_permute_rows(data, idx, idx.shape[0], scatter=False)


def _expert_kernel(nused_ref, elo_ref, ehi_ref, valid_ref, xs_ref, g_ref, wg_lo, wu_lo, wd_lo, wg_hi, wu_hi, wd_hi,
                   ys_ref):
    del elo_ref, ehi_ref
    i = pl.program_id(0)

    @pl.when(i < nused_ref[0])
    def _():
        row = lax.broadcasted_iota(jnp.int32, (xs_ref.shape[0], 1), 0)
        xb = _rms(jnp.where(row < valid_ref[i], xs_ref[...], 0.0), g_ref[...]).astype(BF16)
        pre = [(jnp.dot(xb, wg[0], preferred_element_type=F32), jnp.dot(xb, wu[0], preferred_element_type=F32))
               for wg, wu in ((wg_lo, wu_lo), (wg_hi, wu_hi))]
        hid = [(_silu(gt) * up).astype(BF16) for gt, up in pre]
        y_lo, y_hi = (jnp.dot(h, wd[0], preferred_element_type=F32) for h, wd in zip(hid, (wd_lo, wd_hi)))
        ys_ref[...] = _pack_pair(y_lo, y_hi)

    @pl.when(i >= nused_ref[0])
    def _():
        ys_ref[...] = jnp.zeros(ys_ref.shape, jnp.uint32)


def _experts(n_used, tile_elo, tile_ehi, tile_valid, xs, g_ffn, w_gate, w_up, w_down):
    p_rows = xs.shape[0]
    tm = TM_EXP
    n_tiles = p_rows // tm

    def last_used(i, nu):
        return jnp.minimum(i, jnp.maximum(nu[0] - 1, 0))

    def row_map(i, nu, elo, ehi, valid):
        return (last_used(i, nu), 0)

    def wmap(which):
        def f(i, nu, elo, ehi, valid):
            return ((elo, ehi)[which][last_used(i, nu)], 0, 0)
        return f

    wspec_in = lambda which: pl.BlockSpec((1, D_MODEL, MOE_HIDDEN), wmap(which))
    wspec_out = lambda which: pl.BlockSpec((1, MOE_HIDDEN, D_MODEL), wmap(which))
    grid_spec = pltpu.PrefetchScalarGridSpec(
        num_scalar_prefetch=4,
        grid=(n_tiles,),
        in_specs=[pl.BlockSpec((tm, D_MODEL), row_map),
                  pl.BlockSpec((1, D_MODEL), lambda i, nu, elo, ehi, valid: (0, 0)),
                  wspec_in(0), wspec_in(0), wspec_out(0), wspec_in(1), wspec_in(1), wspec_out(1)],
        out_specs=pl.BlockSpec((tm, D_MODEL), lambda i, nu, elo, ehi, valid: (i, 0)),
    )
    return pl.pallas_call(
        _expert_kernel,
        grid_spec=grid_spec,
        out_shape=jax.ShapeDtypeStruct((p_rows, D_MODEL), jnp.uint32),
        compiler_params=_cparams("arbitrary"),
    )(n_used, tile_elo, tile_ehi, tile_valid, xs, g_ffn, w_gate, w_up, w_down, w_gate, w_up, w_down)


def _final_kernel(x1_ref, meta_ref, p_ref, yt_ref, gple_ref, wpg_ref, wpp_ref, gfin_ref, out_ref):
    tm = x1_ref.shape[0]
    nsub = 2
    subs = [{"rows": pl.ds(k * (tm // nsub), tm // nsub)} for k in range(nsub)]

    def embed(s):
        s["pp"] = jnp.dot(p_ref[s["rows"], :].astype(BF16), wpp_ref[...], preferred_element_type=F32)

    def combine(s):
        y_lo, y_hi = _unpack_pair(yt_ref[s["rows"], :])
        meta = meta_ref[s["rows"], :]
        s["x2"] = x1_ref[s["rows"], :] + meta[:, 2:3] * y_lo + meta[:, 3:4] * y_hi
        s["hn"] = _rms(s["x2"], gple_ref[...]).astype(BF16)

    def gate(s):
        s["gate"] = jnp.dot(s.pop("hn"), wpg_ref[...], preferred_element_type=F32)

    def finish(s):
        out_ref[s["rows"], :] = _rms(s.pop("x2") + _sigmoid(s.pop("gate")) * s.pop("pp"), gfin_ref[...])

    for stage in (embed, combine, gate, finish):
        for s in subs:
            stage(s)


def _final(x1, meta, p2d, ys_tok, g_ple, w_pg, w_pp, g_fin):
    t = x1.shape[0]
    tm = TM_FIN
    row = lambda w: pl.BlockSpec((tm, w), lambda i: (i, 0))
    const = lambda a: pl.BlockSpec(a.shape, lambda i: (0,) * a.ndim)
    return pl.pallas_call(
        _final_kernel,
        grid=(t // tm,),
        in_specs=[row(D_MODEL), row(LANES), row(PLE_DIM), row(D_MODEL),
                  const(g_ple), const(w_pg), const(w_pp), const(g_fin)],
        out_specs=row(D_MODEL),
        out_shape=jax.ShapeDtypeStruct((t, D_MODEL), F32),
        compiler_params=_cparams("arbitrary"),
    )(x1, meta, p2d, ys_tok, g_ple, w_pg, w_pp, g_fin)


_PAIR_LO = np.array([0, 0, 0, 1, 1, 2], np.int32)
_PAIR_HI = np.array([1, 2, 3, 2, 3, 3], np.int32)


def _routing_tables(meta, counts_f, n_tiles):
    cls = meta[:, 0].astype(jnp.int32)
    rank = meta[:, 1].astype(jnp.int32)
    counts = counts_f[0, :MOE_CLASSES].astype(jnp.int32)
    tiles_per = (counts + TM_EXP - 1) // TM_EXP
    tile_end = jnp.cumsum(tiles_per)
    tile_start = tile_end - tiles_per
    class_ids = jnp.arange(MOE_CLASSES, dtype=jnp.int32)
    pos = jnp.sum(jnp.where(cls[:, None] == class_ids[None, :], (tile_start * TM_EXP)[None, :], 0), axis=1) + rank
    n_used = tile_end[-1:]
    tile_ids = jnp.arange(n_tiles, dtype=jnp.int32)
    tile_cls = jnp.minimum(jnp.sum((tile_end[None, :] <= tile_ids[:, None]).astype(jnp.int32), axis=1),
                           MOE_CLASSES - 1)
    grp = tile_cls // MOE_PAIRS
    pair = tile_cls % MOE_PAIRS
    tile_elo = grp * MOE_EPG + jnp.asarray(_PAIR_LO)[pair]
    tile_ehi = grp * MOE_EPG + jnp.asarray(_PAIR_HI)[pair]
    class_left = jnp.sum(jnp.where(tile_cls[:, None] == class_ids[None, :],
                                   (counts - (tile_ids[:, None] - tile_start[None, :]) * TM_EXP), 0), axis=1)
    tile_valid = jnp.clip(class_left, 0, TM_EXP)
    return (pos.astype(jnp.int32), n_used.astype(jnp.int32), tile_elo.astype(jnp.int32),
            tile_ehi.astype(jnp.int32), tile_valid.astype(jnp.int32))


def kernel(x, p, norm_mix_g, w_in, conv_w, conv_b, dt_bias, a_log, d_skip, ssd_norm_g, w_att_branch,
           w_ssd_branch, w_out, norm_ffn_g, w_router_group, b_router_group, w_router_expert,
           b_router_expert, w_exp_gate, w_exp_up, w_exp_down, norm_ple_g, w_ple_gate, w_ple_proj,
           final_norm_g):
    bsz, seq, _ = x.shape
    t = bsz * seq
    assert w_in.shape[0] == 1, "single-layer block"
    assert seq // ATT_PATTERNS[-1][1] == ATT_BLOCK and seq % TM_IN == 0
    x2d = x.reshape(t, D_MODEL)

    wi = w_in[0]
    c_dt = QKV_W + SSD_INNER + SSD_CONV_CH
    w_qkv = wi[:, :QKV_W].astype(BF16)
    w_wide = jnp.concatenate([wi[:, QKV_W:c_dt], wi[:, c_dt + SSD_HEADS:]], axis=1).astype(BF16)
    w_dt = jnp.pad(wi[:, c_dt:c_dt + SSD_HEADS], ((0, 0), (0, LANES - SSD_HEADS))).astype(BF16)
    row = lambda v: v.reshape(1, -1).astype(F32)

    qkv1, qkv2, qkv3, z, xbc, gates, dt_raw = _in_proj(x2d, row(norm_mix_g[0]), w_qkv, w_wide, w_dt,
                                                       conv_w[0], conv_b[0], bsz, seq)

    att = _attention(qkv1.reshape(bsz, seq, ATT_WIDTH), qkv2, qkv3, bsz, seq).reshape(t, ATT_OUT)

    dt_t = dt_raw[:, :SSD_HEADS].reshape(bsz, seq, SSD_GROUPS, SSD_HPG).transpose(0, 2, 3, 1)
    u = _ssd(xbc.reshape(bsz, seq, SSD_CONV_CH), z.reshape(bsz, seq, SSD_INNER), dt_t,
             dt_bias[0], a_log[0], d_skip[0], bsz, seq).reshape(t, SSD_INNER)

    w_r32 = jnp.pad(jnp.concatenate([w_router_group[0], w_router_expert[0]], axis=1),
                    ((0, 0), (0, LANES - MOE_GROUPS - MOE_EXPERTS))).astype(F32)
    w_r_hi = w_r32.astype(BF16)
    w_r_lo = (w_r32 - w_r_hi.astype(F32)).astype(BF16)
    w_r = jnp.concatenate([jnp.concatenate([w_r_hi, w_r_lo], axis=1),
                           jnp.concatenate([w_r_hi, jnp.zeros_like(w_r_lo)], axis=1)], axis=0)
    b_r = jnp.pad(jnp.concatenate([b_router_group[0], b_router_expert[0]]),
                  (0, LANES - MOE_GROUPS - MOE_EXPERTS)).reshape(1, LANES).astype(F32)
    x1, meta, counts = _post(att, u, gates, x2d,
                             w_att_branch[0].astype(BF16), w_ssd_branch[0].astype(BF16), w_out[0].astype(BF16),
                             row(ssd_norm_g[0]), row(norm_ffn_g[0]), w_r, b_r)

    n_tiles = t // TM_EXP + MOE_CLASSES
    pos, n_used, tile_elo, tile_ehi, tile_valid = _routing_tables(meta, counts, n_tiles)
    xs = _sc_scatter_rows(x1, pos, n_tiles * TM_EXP)
    ys = _experts(n_used, tile_elo, tile_ehi, tile_valid, xs, row(norm_ffn_g[0]),
                  w_exp_gate[0].astype(BF16), w_exp_up[0].astype(BF16), w_exp_down[0].astype(BF16))
    ys_tok = _sc_gather_rows(ys, pos)
    out = _final(x1, meta, p[0].reshape(t, PLE_DIM), ys_tok, row(norm_ple_g[0]),
                 w_ple_gate[0].astype(BF16), w_ple_proj[0].astype(BF16), row(final_norm_g))
    return out.reshape(bsz, seq, D_MODEL)
```

```python
import functools

import numpy as np
import jax
import jax.numpy as jnp
from jax import lax
from jax.experimental import pallas as pl
from jax.experimental.pallas import tpu as pltpu
from jax.experimental.pallas import tpu_sc as plsc

F32 = jnp.float32
BF16 = jnp.bfloat16

D_MODEL = 1024
PLE_DIM = 256
RMS_EPS = 1e-6

ATT_PATTERNS = ((128, 1), (512, 4), (2048, 16))
ATT_GROUPS = 3
ATT_HPG = 8
ATT_HEAD_DIM = 64
ATT_WIDTH = ATT_GROUPS * ATT_HPG * ATT_HEAD_DIM
ATT_OUT = ATT_HPG * ATT_HEAD_DIM
ATT_BLOCK = 128
ATT_LOCKSTEP = 2
ALIBI_MAX_BIAS = 8.0
QKV_W = 3 * ATT_WIDTH

SSD_INNER = 2048
SSD_HEADS = 32
SSD_GROUPS = 8
SSD_HPG = 4
SSD_HEAD_DIM = 64
SSD_STATE = 128
SSD_CONV = 4
SSD_CHUNK = 128
SSD_CONV_CH = SSD_INNER + 2 * SSD_GROUPS * SSD_STATE
SSD_GW = SSD_HPG * SSD_HEAD_DIM
SSD_LOCKSTEP = 8

MOE_GROUPS = 4
MOE_EPG = 4
MOE_EXPERTS = 16
MOE_HIDDEN = 512
MOE_PAIRS = 6
MOE_CLASSES = MOE_GROUPS * MOE_PAIRS

LANES = 128
NEG_BIG = -1e30
VMEM_LIMIT = 56 * 1024 * 1024

TM_IN = 1024
TN_IN = 512
TN_WIDE = 1024
TM_POST = 512
TM_EXP = 256
TM_FIN = 512
FINAL_PARTS = 2
SC_WINDOW = 128
SC_SUB = 32


def _cparams(*sem):
    return pltpu.CompilerParams(dimension_semantics=sem, vmem_limit_bytes=VMEM_LIMIT)


def _sigmoid(x):
    return 0.5 * jnp.tanh(0.5 * x) + 0.5


def _silu(x):
    h = 0.5 * x
    return h + h * jnp.tanh(h)


def _rms(x, g):
    ms = jnp.mean(x * x, axis=-1, keepdims=True)
    return x * lax.rsqrt(ms + RMS_EPS) * g


_IN_SEG = (3, 3, 3, SSD_INNER // TN_WIDE, SSD_CONV_CH // TN_WIDE, 2 * D_MODEL // TN_WIDE)
_IN_START = tuple(int(v) for v in np.cumsum((0,) + _IN_SEG))
_IN_NARROW = _IN_START[3]


def _inproj_kernel(x_ref, g_ref, wq_ref, ww_ref, wdt_ref, cw_ref, cb_ref, qkv1_ref, qkv2_ref, qkv3_ref, z_ref,
                   xbc_ref, gates_ref, dt_ref, h_ref, hcol_ref, halo_ref, *, per_seq):
    j = pl.program_id(1)
    tm = x_ref.shape[0]

    @pl.when(j == 0)
    def _():
        h = _rms(x_ref[...], g_ref[...])
        hb = h.astype(BF16)
        h_ref[0] = hb
        dt_ref[...] = jnp.dot(hb, wdt_ref[...], preferred_element_type=F32)
        ncb = hcol_ref.shape[0]
        for c in range(ncb):
            hcol_ref[c] = h[:, c * LANES:(c + 1) * LANES]
        for slot, (_, dil) in enumerate(ATT_PATTERNS[1:], start=1):
            rows = tm // dil
            for r in range(dil):
                for c in range(ncb):
                    h_ref[slot, r * rows:(r + 1) * rows, c * LANES:(c + 1) * LANES] = (
                        hcol_ref[c, pl.ds(r, rows, stride=dil), :].astype(BF16))

    def segment(k, fn):
        @pl.when((j >= _IN_START[k]) & (j < _IN_START[k + 1]))
        def _():
            fn()

    def narrow(slot, store):
        def fn():
            store(jnp.dot(h_ref[slot], wq_ref[...], preferred_element_type=F32).astype(BF16))
        return fn

    def wide(ref):
        def fn():
            ref[...] = jnp.dot(h_ref[0], ww_ref[...], preferred_element_type=F32).astype(BF16)
        return fn

    def store_plain(ref):
        def store(res):
            ref[...] = res
        return store

    def store_grouped(ref, dil):
        def store(res):
            ref[0] = res.reshape(dil, tm // dil, res.shape[1])
        return store

    def conv_silu():
        jb = j - _IN_START[4]
        nsub = 4
        rs = tm // nsub
        res = [jnp.dot(h_ref[0, k * rs:(k + 1) * rs, :], ww_ref[...], preferred_element_type=F32)
               for k in range(nsub)]
        seq_start = pl.program_id(0) % per_seq == 0
        prev = jnp.where(seq_start, 0.0, halo_ref[jb])
        row8 = lax.broadcasted_iota(jnp.int32, (8, 1), 0)
        cw = cw_ref[...]
        for k in range(nsub):
            acc = cb_ref[...] + cw[SSD_CONV - 1:SSD_CONV, :] * res[k]
            for sh in range(1, SSD_CONV):
                rolled = pltpu.roll(res[k], sh, 0)
                top = jnp.where(row8 < sh, pltpu.roll(prev, sh, 0), rolled[0:8])
                shifted = jnp.concatenate([top, rolled[8:]], axis=0)
                acc = acc + cw[SSD_CONV - 1 - sh:SSD_CONV - sh, :] * shifted
            xbc_ref[k * rs:(k + 1) * rs, :] = acc.astype(BF16)
            prev = res[k][rs - 8:rs]
        halo_ref[jb] = prev

    segment(0, narrow(0, store_plain(qkv1_ref)))
    segment(1, narrow(1, store_grouped(qkv2_ref, ATT_PATTERNS[1][1])))
    segment(2, narrow(2, store_grouped(qkv3_ref, ATT_PATTERNS[2][1])))
    segment(3, wide(z_ref))
    segment(4, conv_silu)
    segment(5, wide(gates_ref))


def _in_proj(x2d, g, w_qkv, w_wide, w_dt, conv_w, conv_b, bsz, seq):
    t = x2d.shape[0]
    nj = _IN_START[-1]
    per_seq = seq // TM_IN
    d2, d3 = ATT_PATTERNS[1][1], ATT_PATTERNS[2][1]

    def seg_map(k):
        return lambda i, j: (i, jnp.clip(j - _IN_START[k], 0, _IN_SEG[k] - 1))

    def grp_map(k):
        return lambda i, j: (i // per_seq, 0, i % per_seq, jnp.clip(j - _IN_START[k], 0, _IN_SEG[k] - 1))

    def wq_map(i, j):
        jj = jnp.minimum(j, _IN_NARROW - 1)
        return (0, (jj % 3) * ATT_GROUPS + jj // 3)

    out_shapes = (
        jax.ShapeDtypeStruct((t, ATT_WIDTH), BF16),
        jax.ShapeDtypeStruct((bsz, d2, seq // d2, ATT_WIDTH), BF16),
        jax.ShapeDtypeStruct((bsz, d3, seq // d3, ATT_WIDTH), BF16),
        jax.ShapeDtypeStruct((t, SSD_INNER), BF16),
        jax.ShapeDtypeStruct((t, SSD_CONV_CH), BF16),
        jax.ShapeDtypeStruct((t, 2 * D_MODEL), BF16),
        jax.ShapeDtypeStruct((t, LANES), F32),
    )
    out_specs = (
        pl.BlockSpec((TM_IN, TN_IN), seg_map(0)),
        pl.BlockSpec((1, d2, TM_IN // d2, TN_IN), grp_map(1)),
        pl.BlockSpec((1, d3, TM_IN // d3, TN_IN), grp_map(2)),
        pl.BlockSpec((TM_IN, TN_WIDE), seg_map(3)),
        pl.BlockSpec((TM_IN, TN_WIDE), seg_map(4)),
        pl.BlockSpec((TM_IN, TN_WIDE), seg_map(5)),
        pl.BlockSpec((TM_IN, LANES), lambda i, j: (i, 0)),
    )
    conv_map = lambda i, j: (0, jnp.clip(j - _IN_START[4], 0, _IN_SEG[4] - 1))
    return pl.pallas_call(
        functools.partial(_inproj_kernel, per_seq=per_seq),
        grid=(t // TM_IN, nj),
        in_specs=[
            pl.BlockSpec((TM_IN, D_MODEL), lambda i, j: (i, 0)),
            pl.BlockSpec((1, D_MODEL), lambda i, j: (0, 0)),
            pl.BlockSpec((D_MODEL, TN_IN), wq_map),
            pl.BlockSpec((D_MODEL, TN_WIDE), lambda i, j: (0, jnp.maximum(j - _IN_NARROW, 0))),
            pl.BlockSpec((D_MODEL, LANES), lambda i, j: (0, 0)),
            pl.BlockSpec((SSD_CONV, TN_WIDE), conv_map),
            pl.BlockSpec((1, TN_WIDE), conv_map),
        ],
        out_specs=out_specs,
        out_shape=out_shapes,
        scratch_shapes=[pltpu.VMEM((3, TM_IN, D_MODEL), BF16),
                        pltpu.VMEM((D_MODEL // LANES, TM_IN, LANES), F32),
                        pltpu.VMEM((_IN_SEG[4], 8, TN_WIDE), F32)],
        compiler_params=_cparams("arbitrary", "arbitrary"),
    )(x2d, g, w_qkv, w_wide, w_dt, conv_w.astype(F32), conv_b.reshape(1, SSD_CONV_CH).astype(F32))


def _att_bias_tables():
    h = np.arange(1, ATT_GROUPS * ATT_HPG + 1, dtype=np.float32)
    slopes = np.exp2(-ALIBI_MAX_BIAS * h / (ATT_GROUPS * ATT_HPG)).astype(np.float32).reshape(ATT_GROUPS, ATT_HPG)
    qi = np.arange(ATT_BLOCK)[:, None] + ATT_BLOCK
    kj = np.arange(2 * ATT_BLOCK)[None, :]
    delta = qi - kj
    tabs = []
    for g, (window, dil) in enumerate(ATT_PATTERNS):
        span = window // dil
        band = (delta >= 0) & (delta <= span)
        bias = (-slopes[g][:, None, None] * (delta * dil).astype(np.float32)[None]).astype(np.float32)
        with_prev = np.where(band[None], bias, np.float32(NEG_BIG))
        first = np.where((band & (kj >= ATT_BLOCK))[None], bias, np.float32(NEG_BIG))
        tabs.append(np.stack([first, with_prev]).astype(np.float32))
    return tabs[0], tabs[1], tabs[2][1][:, :, ATT_BLOCK:]


def _att_units(units):
    lane = lax.broadcasted_iota(jnp.int32, (1, LANES), 1)
    low = lane < ATT_HEAD_DIM
    scale = ATT_HEAD_DIM ** -0.5
    qmask = (jnp.where(low, scale, 0.0).astype(BF16), jnp.where(low, 0.0, scale).astype(BF16))
    heads = [(q, k2, v2, tab_fn, hh) for q, k2, v2, tab_fn in units for hh in range(2)]
    scores = [lax.dot_general(q * qmask[hh], k2, (((1,), (1,)), ((), ())), preferred_element_type=F32)
              for q, k2, _, _, hh in heads]
    probs = []
    for s, (_, _, _, tab_fn, hh) in zip(scores, heads):
        s = s + tab_fn(hh)
        m = jnp.max(s, axis=-1, keepdims=True)
        e = jnp.exp(s - m)
        probs.append((e.astype(BF16), m, jnp.sum(e, axis=-1, keepdims=True)))
    pvs = [jnp.dot(e, v2, preferred_element_type=F32) for (e, _, _), (_, _, v2, _, _) in zip(probs, heads)]
    outs = [pv / den for pv, (_, _, den) in zip(pvs, probs)]
    lses = [m + jnp.log(den) for _, m, den in probs]
    return [(jnp.where(low, outs[2 * i], outs[2 * i + 1]), jnp.where(low, lses[2 * i], lses[2 * i + 1]))
            for i in range(len(units))]


def _att_kernel(q1, k1, v1, q2, k2, v2, q3, k3, v3, t1, t2, t3, out_ref,
                o1, l1, o2p, l2p, o2n, l2n, o3p, l3p, o3n, l3n, *, seq):
    blk = ATT_BLOCK
    d2, d3 = ATT_PATTERNS[1][1], ATT_PATTERNS[2][1]
    nb1, nb2 = seq // blk, seq // d2 // blk

    def rows(n):
        return pl.ds(pl.multiple_of(n * blk, blk), blk)

    assert nb1 == d2 * nb2 == d3

    def unit_body(i, _):
        units, dests = [], []
        for k in range(ATT_LOCKSTEP):
            u = i * ATT_LOCKSTEP + k
            cur, prv = rows(u), rows(jnp.maximum(u - 1, 0))
            sel = jnp.minimum(u, 1)
            units.append((q1[0, cur, :], jnp.concatenate([k1[0, prv, :], k1[0, cur, :]], axis=0),
                          jnp.concatenate([v1[0, prv, :], v1[0, cur, :]], axis=0),
                          lambda hh, sel=sel: t1[sel, hh]))
            r, n = u // nb2, u % nb2
            cur2, prv2 = rows(n), rows(jnp.maximum(n - 1, 0))
            sel2 = jnp.minimum(n, 1)
            units.append((q2[0, r, cur2, :], jnp.concatenate([k2[0, r, prv2, :], k2[0, r, cur2, :]], axis=0),
                          jnp.concatenate([v2[0, r, prv2, :], v2[0, r, cur2, :]], axis=0),
                          lambda hh, sel2=sel2: t2[sel2, hh]))
            units.append((q3[0, u], k3[0, u], v3[0, u], lambda hh: t3[hh]))
            dests += [(o1, l1, cur), (o2p, l2p, cur), (o3p, l3p, cur)]
        for (o, l), (o_ref, l_ref, where) in zip(_att_units(units), dests):
            o_ref[where, :] = o
            l_ref[where, :] = l
        return 0

    lax.fori_loop(0, nb1 // ATT_LOCKSTEP, unit_body, 0)

    for dil, pairs in ((d2, ((o2p, o2n), (l2p, l2n))), (d3, ((o3p, o3n), (l3p, l3n)))):
        n_sub = seq // dil
        for r in range(dil):
            for src, dst in pairs:
                dst[pl.ds(r, n_sub, stride=dil), :] = src[r * n_sub:(r + 1) * n_sub, :]

    mrows = 2 * blk

    def merge(c, _):
        rr = pl.ds(pl.multiple_of(c * mrows, mrows), mrows)
        la, lb, lc = l1[rr, :], l2n[rr, :], l3n[rr, :]
        lm = jnp.maximum(jnp.maximum(la, lb), lc)
        ea, eb, ec = jnp.exp(la - lm), jnp.exp(lb - lm), jnp.exp(lc - lm)
        att = (ea * o1[rr, :] + eb * o2n[rr, :] + ec * o3n[rr, :]) / (ea + eb + ec)
        out_ref[0, rr, :] = att.astype(BF16)
        return 0

    lax.fori_loop(0, seq // mrows, merge, 0)


def _attention(qkv1, qkv2, qkv3, bsz, seq):
    t1, t2, t3 = (jnp.asarray(t) for t in _att_bias_tables())
    d2, d3 = ATT_PATTERNS[1][1], ATT_PATTERNS[2][1]
    npair = ATT_HPG // 2
    in_arrays, in_specs = [], []
    for arr, lead in ((qkv1, ()), (qkv2, (d2,)), (qkv3, (d3,))):
        n_rows = arr.shape[-2]
        for sel in range(3):
            zeros = (0,) * len(lead)
            in_arrays.append(arr)
            in_specs.append(pl.BlockSpec((1,) + lead + (n_rows, LANES),
                                         lambda hp, b, sel=sel, zeros=zeros: (b,) + zeros + (0, sel * npair + hp)))
    in_arrays += [t1, t2, t3]
    in_specs += [pl.BlockSpec((2, 2, ATT_BLOCK, 2 * ATT_BLOCK), lambda hp, b: (0, hp, 0, 0)),
                 pl.BlockSpec((2, 2, ATT_BLOCK, 2 * ATT_BLOCK), lambda hp, b: (0, hp, 0, 0)),
                 pl.BlockSpec((2, ATT_BLOCK, ATT_BLOCK), lambda hp, b: (hp, 0, 0))]
    return pl.pallas_call(
        functools.partial(_att_kernel, seq=seq),
        grid=(npair, bsz),
        in_specs=in_specs,
        out_specs=pl.BlockSpec((1, seq, LANES), lambda hp, b: (b, 0, hp)),
        out_shape=jax.ShapeDtypeStruct((bsz, seq, ATT_OUT), BF16),
        scratch_shapes=[pltpu.VMEM((seq, LANES), F32)] * 10,
        compiler_params=_cparams("arbitrary", "arbitrary"),
    )(*in_arrays)


def _softplus(x):
    return jnp.maximum(x, 0.0) + jnp.log1p(jnp.exp(-jnp.abs(x)))


def _head_spread_table():
    gw, n_src = SSD_GW, LANES
    tab = np.zeros((2 * n_src, 2 * gw), np.float32)
    for half in range(2):
        for blk, src0 in enumerate((4, 8)):
            for col in range(gw):
                tab[half * n_src + src0 + col // SSD_HEAD_DIM, blk * gw + col] = 1.0
    return tab


def _ssd_kernel(x_ref, b_ref, c_ref, z_ref, dt_ref, spread_ref, dtb_ref, alog_ref, dsk_ref, u_ref, carry_ref,
                *, n_chunks):
    L = SSD_CHUNK
    gw = SSD_GW
    carry_ref[...] = jnp.zeros(carry_ref.shape, F32)

    a_neg = -jnp.exp(alog_ref[0])
    dtb = dtb_ref[0]
    dsk = dsk_ref[0]
    ri = lax.broadcasted_iota(jnp.int32, (L, L), 0)
    ci = lax.broadcasted_iota(jnp.int32, (L, L), 1)
    upper_incl = (ri <= ci).astype(F32)
    causal = ri >= ci
    lane = lax.broadcasted_iota(jnp.int32, (1, LANES), 1)
    low = lane < SSD_HEAD_DIM
    heads = range(SSD_HPG)

    def load(s):
        rows = pl.ds(s["r0"], L)
        s["xs"] = _silu(x_ref[0, rows, :].astype(F32))
        s["xs_b"] = s["xs"].astype(BF16)
        s["bm"] = _silu(b_ref[0, rows, :].astype(F32))
        s["cm_b"] = _silu(c_ref[0, rows, :].astype(F32)).astype(BF16)

    def decay_cumsum(s):
        dt = _softplus(dt_ref[0, 0, :, pl.ds(s["r0"], L)] + dtb)
        rows8 = jnp.concatenate([dt * a_neg, dt], axis=0)
        cs8 = jnp.dot(rows8, upper_incl, preferred_element_type=F32, precision=lax.Precision.HIGHEST)
        s["dt"], s["acs_t"] = dt, cs8[0:SSD_HPG]

    def decay_spread(s):
        dt, acs_t = s["dt"], s["acs_t"]
        t16 = jnp.concatenate([acs_t, acs_t, acs_t, dt], axis=0)
        cols = jnp.concatenate([t16, jnp.zeros((L - 16, L), F32)], axis=0).T
        dt_at8 = pltpu.roll(cols, LANES - 4, 1)
        comb = jnp.where(lane < 8, jnp.exp(cols), jnp.exp(cols[L - 1:L, :] - cols) * dt_at8)
        hi = comb.astype(BF16)
        lo = (comb - hi.astype(F32)).astype(BF16)
        s["spread"] = jnp.dot(jnp.concatenate([hi, lo], axis=1), spread_ref[...], preferred_element_type=F32)
        s["acs_cols"] = cols

    def scores(s):
        s["bm_t"] = s.pop("bm").T.astype(BF16)
        s["cb"] = jnp.dot(s["cm_b"], s["bm_t"], preferred_element_type=F32)

    def intra(s):
        spread, acs_t, dt, cb, cols = s["spread"], s.pop("acs_t"), s.pop("dt"), s.pop("cb"), s.pop("acs_cols")
        xs, xs_b = s["xs"], s.pop("xs_b")
        mixes = []
        for j in heads:
            seg = cols[:, j:j + 1] - acs_t[j:j + 1, :]
            mixes.append((jnp.exp(jnp.where(causal, seg, -jnp.inf)) * (cb * dt[j:j + 1, :])).astype(BF16))
        halves = []
        for hp in range(SSD_HPG // 2):
            yy = jnp.dot(jnp.concatenate(mixes[2 * hp:2 * hp + 2], axis=0), xs_b[:, hp * LANES:(hp + 1) * LANES],
                         preferred_element_type=F32)
            halves.append(jnp.where(low, yy[:L], yy[L:]))
        s["y"] = jnp.concatenate(halves, axis=1)
        s["st_new"] = jnp.dot(s.pop("bm_t"), (xs * spread[:, gw:]).astype(BF16),
                              preferred_element_type=F32)

    def inter(s):
        eacs_bc = s.pop("spread")[:, :gw]
        carry = carry_ref[...]
        y = s.pop("y") + jnp.dot(s.pop("cm_b"), carry.astype(BF16), preferred_element_type=F32) * eacs_bc
        carry_ref[...] = carry * eacs_bc[L - 1:L, :] + s.pop("st_new")
        y = y + s.pop("xs") * dsk
        rows = pl.ds(s["r0"], L)
        u_ref[0, rows, :] = (y * _silu(z_ref[0, rows, :].astype(F32))).astype(BF16)

    def body(i, _):
        states = [{"c": i * SSD_LOCKSTEP + k, "r0": pl.multiple_of((i * SSD_LOCKSTEP + k) * L, L)}
                  for k in range(SSD_LOCKSTEP)]
        for stage in (load, decay_cumsum, decay_spread, scores, intra, inter):
            for s in states:
                stage(s)
        return 0

    lax.fori_loop(0, n_chunks // SSD_LOCKSTEP, body, 0)


def _ssd(xbc, z, dt_t, dt_bias, a_log, d_skip, bsz, seq):
    gw = SSD_GW
    nxb = SSD_INNER // SSD_STATE
    dsk = jnp.repeat(d_skip.astype(F32), SSD_HEAD_DIM).reshape(SSD_GROUPS, 1, gw)
    dtb = dt_bias.astype(F32).reshape(SSD_GROUPS, SSD_HPG, 1)
    alog = a_log.astype(F32).reshape(SSD_GROUPS, SSD_HPG, 1)
    spread = jnp.asarray(_head_spread_table(), BF16)
    x_map = lambda b, g: (b, 0, g)
    bm_map = lambda b, g: (b, 0, nxb + g)
    cm_map = lambda b, g: (b, 0, nxb + SSD_GROUPS + g)
    return pl.pallas_call(
        functools.partial(_ssd_kernel, n_chunks=seq // SSD_CHUNK),
        grid=(bsz, SSD_GROUPS),
        in_specs=[
            pl.BlockSpec((1, seq, gw), x_map),
            pl.BlockSpec((1, seq, SSD_STATE), bm_map),
            pl.BlockSpec((1, seq, SSD_STATE), cm_map),
            pl.BlockSpec((1, seq, gw), x_map),
            pl.BlockSpec((1, 1, SSD_HPG, seq), lambda b, g: (b, g, 0, 0)),
            pl.BlockSpec(spread.shape, lambda b, g: (0, 0)),
            pl.BlockSpec((1, SSD_HPG, 1), lambda b, g: (g, 0, 0)),
            pl.BlockSpec((1, SSD_HPG, 1), lambda b, g: (g, 0, 0)),
            pl.BlockSpec((1, 1, gw), lambda b, g: (g, 0, 0)),
        ],
        out_specs=pl.BlockSpec((1, seq, gw), x_map),
        out_shape=jax.ShapeDtypeStruct((bsz, seq, SSD_INNER), BF16),
        scratch_shapes=[pltpu.VMEM((SSD_STATE, gw), F32)],
        compiler_params=_cparams("arbitrary", "arbitrary"),
    )(xbc, xbc, xbc, z, dt_t, spread, dtb, alog, dsk)


def _first_index_of_max(vals, lane_f):
    m = jnp.max(vals, axis=-1, keepdims=True)
    idx = jnp.min(jnp.where(vals == m, lane_f, float(LANES)), axis=-1, keepdims=True)
    return m, idx


def _post_kernel(att_ref, u_ref, gates_ref, x_ref, watt_ref, wssd_ref, wout_ref,
                 gssd_ref, gffn_ref, wr_ref, br_ref, x1_ref, meta_ref, cnt_ref, run_ref):
    i = pl.program_id(0)

    @pl.when(i == 0)
    def _():
        run_ref[...] = jnp.zeros(run_ref.shape, F32)

    tm = x_ref.shape[0]
    nsub = 2
    subs = [{"rows": pl.ds(k * (tm // nsub), tm // nsub)} for k in range(nsub)]

    def att_branch(s):
        s["y_att"] = jnp.dot(att_ref[s["rows"], :], watt_ref[...], preferred_element_type=F32)

    def ssd_branch(s):
        ssd = _rms(u_ref[s["rows"], :].astype(F32), gssd_ref[...])
        s["y_ssd"] = jnp.dot(ssd.astype(BF16), wssd_ref[...], preferred_element_type=F32)

    def mix_out(s):
        gates = gates_ref[s["rows"], :].astype(F32)
        merged = _sigmoid(gates[:, :D_MODEL]) * s.pop("y_att") + _sigmoid(gates[:, D_MODEL:]) * s.pop("y_ssd")
        x1 = x_ref[s["rows"], :] + jnp.dot(merged.astype(BF16), wout_ref[...], preferred_element_type=F32)
        x1_ref[s["rows"], :] = x1
        s["h2"] = _rms(x1, gffn_ref[...])

    def router(s):
        h2 = s.pop("h2")
        hi = h2.astype(BF16)
        lo = (h2 - hi.astype(F32)).astype(BF16)
        both = jnp.dot(jnp.concatenate([hi, lo], axis=1), wr_ref[...], preferred_element_type=F32)
        s["logits"] = both[:, :LANES] + both[:, LANES:]

    for stage in (att_branch, ssd_branch, mix_out, router):
        for s in subs:
            stage(s)
    logits = jnp.concatenate([s["logits"] for s in subs], axis=0) + br_ref[...]
    lane = lax.broadcasted_iota(jnp.int32, (tm, LANES), 1)
    lane_f = lane.astype(F32)
    ninf = -jnp.inf
    gl = jnp.where(lane < MOE_GROUPS, logits, ninf)
    gmax, gidx = _first_index_of_max(gl, lane_f)
    g_val = 1.0 / jnp.sum(jnp.exp(gl - gmax), axis=-1, keepdims=True)
    base = MOE_GROUPS + MOE_EPG * gidx
    el = jnp.where((lane_f >= base) & (lane_f < base + MOE_EPG), logits, ninf)
    e1, i1 = _first_index_of_max(el, lane_f)
    e2, i2 = _first_index_of_max(jnp.where(lane_f == i1, ninf, el), lane_f)
    t2 = jnp.exp(e2 - e1)
    w1 = g_val / (1.0 + t2)
    w2 = g_val * t2 / (1.0 + t2)
    a1, a2 = i1 - base, i2 - base
    lo, hi = jnp.minimum(a1, a2), jnp.maximum(a1, a2)
    c_lo = jnp.where(a1 < a2, w1, w2)
    c_hi = jnp.where(a1 < a2, w2, w1)
    pair = lo * (7.0 - lo) * 0.5 + (hi - lo - 1.0)
    cls = gidx * MOE_PAIRS + pair

    onehot = (lane_f == cls)
    oh_b = jnp.where(onehot, 1.0, 0.0).astype(BF16)
    rr = lax.broadcasted_iota(jnp.int32, (tm, tm), 0)
    cc = lax.broadcasted_iota(jnp.int32, (tm, tm), 1)
    strict = jnp.where(rr > cc, 1.0, 0.0).astype(BF16)
    prefix = jnp.dot(strict, oh_b, preferred_element_type=F32) + run_ref[...]
    rank = jnp.sum(jnp.where(onehot, prefix, 0.0), axis=-1, keepdims=True)
    run = run_ref[...] + jnp.sum(oh_b.astype(F32), axis=0, keepdims=True)
    run_ref[...] = run
    cnt_ref[...] = jnp.broadcast_to(run, cnt_ref.shape)

    meta = jnp.where(lane == 0, cls, jnp.where(lane == 1, rank, jnp.where(lane == 2, c_lo,
                     jnp.where(lane == 3, c_hi, 0.0))))
    meta_ref[...] = meta


def _post(att, u, gates, x2d, w_att, w_ssd, w_out, g_ssd, g_ffn, w_r, b_r):
    t = x2d.shape[0]
    tm = TM_POST
    row = lambda w: pl.BlockSpec((tm, w), lambda i: (i, 0))
    const = lambda a: pl.BlockSpec(a.shape, lambda i: (0,) * a.ndim)
    return pl.pallas_call(
        _post_kernel,
        grid=(t // tm,),
        in_specs=[row(ATT_OUT), row(SSD_INNER), row(2 * D_MODEL), row(D_MODEL),
                  const(w_att), const(w_ssd), const(w_out), const(g_ssd), const(g_ffn), const(w_r), const(b_r)],
        out_specs=(row(D_MODEL), row(LANES), pl.BlockSpec((8, LANES), lambda i: (0, 0))),
        out_shape=(jax.ShapeDtypeStruct((t, D_MODEL), F32),
                   jax.ShapeDtypeStruct((t, LANES), F32),
                   jax.ShapeDtypeStruct((8, LANES), F32)),
        scratch_shapes=[pltpu.VMEM((1, LANES), F32)],
        compiler_params=_cparams("arbitrary"),
    )(att, u, gates, x2d, w_att, w_ssd, w_out, g_ssd, g_ffn, w_r, b_r)


def _pack_pair(a, b):
    ua = lax.bitcast_convert_type(a.astype(BF16).astype(F32), jnp.uint32)
    ub = lax.bitcast_convert_type(b.astype(BF16).astype(F32), jnp.uint32)
    return (ua & jnp.uint32(0xFFFF0000)) | (ub >> 16)


def _unpack_pair(p):
    a = lax.bitcast_convert_type(p & jnp.uint32(0xFFFF0000), F32)
    b = lax.bitcast_convert_type(p << 16, F32)
    return a, b


def _sc_mesh():
    return plsc.VectorSubcoreMesh(core_axis_name="c", subcore_axis_name="s")


def _sc_permute_rows(data, idx, n_out, scatter):
    n, width = idx.shape[0], data.shape[1]
    mesh = _sc_mesh()
    workers = mesh.num_cores * mesh.num_subcores
    per_worker = n // SC_WINDOW // workers
    assert per_worker * workers * SC_WINDOW == n

    n_sub = SC_WINDOW // SC_SUB

    @pl.kernel(out_type=jax.ShapeDtypeStruct((n_out, width), data.dtype), mesh=mesh,
               scratch_types=[pltpu.VMEM((1, SC_WINDOW), jnp.int32), pltpu.VMEM((2, SC_SUB, width), data.dtype),
                              pltpu.SemaphoreType.DMA((2,)), pltpu.SemaphoreType.DMA((2,))])
    def permute(data_hbm, idx_hbm, out_hbm, idx_v, rows_v, sem_in, sem_out):
        worker = lax.axis_index("c") * mesh.num_subcores + lax.axis_index("s")

        @pl.loop(0, per_worker)
        def _(j):
            base = (worker * per_worker + j) * SC_WINDOW
            pltpu.sync_copy(idx_hbm.at[:, pl.ds(base, SC_WINDOW)], idx_v)

            def ends(k):
                sub_idx = idx_v.at[0, pl.ds(k * SC_SUB, SC_SUB)]
                plain = pl.ds(base + k * SC_SUB, SC_SUB)
                return (data_hbm.at[plain], out_hbm.at[sub_idx]) if scatter else (data_hbm.at[sub_idx],
                                                                                  out_hbm.at[plain])

            def read(k):
                return pltpu.make_async_copy(ends(k)[0], rows_v.at[k % 2], sem_in.at[k % 2])

            def write(k):
                return pltpu.make_async_copy(rows_v.at[k % 2], ends(k)[1], sem_out.at[k % 2])

            read(0).start()
            for k in range(n_sub):
                read(k).wait()
                if k + 1 < n_sub:
                    if k >= 1:
                        write(k - 1).wait()
                    read(k + 1).start()
                write(k).start()
            write(n_sub - 2).wait()
            write(n_sub - 1).wait()

    return permute(data, idx.reshape(1, n))


def _sc_scatter_rows(data, idx, n_out):
    return _sc_permute_rows(data, idx, n_out, scatter=True)


def _sc_gather_rows(data, idx):
    return _sc_permute_rows(data, idx, idx.shape[0], scatter=False)


def _expert_kernel(nused_ref, elo_ref, ehi_ref, valid_ref, xs_ref, g_ref, wg_lo, wu_lo, wd_lo, wg_hi, wu_hi, wd_hi,
                   ys_ref):
    del elo_ref, ehi_ref
    i = pl.program_id(0)

    @pl.when(i < nused_ref[0])
    def _():
        row = lax.broadcasted_iota(jnp.int32, (xs_ref.shape[0], 1), 0)
        xb = _rms(jnp.where(row < valid_ref[i], xs_ref[...], 0.0), g_ref[...]).astype(BF16)
        pre = [(jnp.dot(xb, wg[0], preferred_element_type=F32), jnp.dot(xb, wu[0], preferred_element_type=F32))
               for wg, wu in ((wg_lo, wu_lo), (wg_hi, wu_hi))]
        hid = [(_silu(gt) * up).astype(BF16) for gt, up in pre]
        y_lo, y_hi = (jnp.dot(h, wd[0], preferred_element_type=F32) for h, wd in zip(hid, (wd_lo, wd_hi)))
        ys_ref[...] = _pack_pair(y_lo, y_hi)

    @pl.when(i >= nused_ref[0])
    def _():
        ys_ref[...] = jnp.zeros(ys_ref.shape, jnp.uint32)


def _experts(n_used, tile_elo, tile_ehi, tile_valid, xs, g_ffn, w_gate, w_up, w_down):
    p_rows = xs.shape[0]
    tm = TM_EXP
    n_tiles = p_rows // tm

    def last_used(i, nu):
        return jnp.minimum(i, jnp.maximum(nu[0] - 1, 0))

    def row_map(i, nu, elo, ehi, valid):
        return (last_used(i, nu), 0)

    def wmap(which):
        def f(i, nu, elo, ehi, valid):
            return ((elo, ehi)[which][last_used(i, nu)], 0, 0)
        return f

    wspec_in = lambda which: pl.BlockSpec((1, D_MODEL, MOE_HIDDEN), wmap(which))
    wspec_out = lambda which: pl.BlockSpec((1, MOE_HIDDEN, D_MODEL), wmap(which))
    grid_spec = pltpu.PrefetchScalarGridSpec(
        num_scalar_prefetch=4,
        grid=(n_tiles,),
        in_specs=[pl.BlockSpec((tm, D_MODEL), row_map),
                  pl.BlockSpec((1, D_MODEL), lambda i, nu, elo, ehi, valid: (0, 0)),
                  wspec_in(0), wspec_in(0), wspec_out(0), wspec_in(1), wspec_in(1), wspec_out(1)],
        out_specs=pl.BlockSpec((tm, D_MODEL), lambda i, nu, elo, ehi, valid: (i, 0)),
    )
    return pl.pallas_call(
        _expert_kernel,
        grid_spec=grid_spec,
        out_shape=jax.ShapeDtypeStruct((p_rows, D_MODEL), jnp.uint32),
        compiler_params=_cparams("arbitrary"),
    )(n_used, tile_elo, tile_ehi, tile_valid, xs, g_ffn, w_gate, w_up, w_down, w_gate, w_up, w_down)


def _final_kernel(x1_ref, meta_ref, p_ref, yt_ref, gple_ref, wpg_ref, wpp_ref, gfin_ref, *rest):
    out_ref = rest[-1]
    tm = x1_ref.shape[0]
    nsub = 2
    subs = [{"rows": pl.ds(k * (tm // nsub), tm // nsub)} for k in range(nsub)]

    def embed(s):
        s["pp"] = jnp.dot(p_ref[s["rows"], :].astype(BF16), wpp_ref[...], preferred_element_type=F32)

    def combine(s):
        y_lo, y_hi = _unpack_pair(yt_ref[s["rows"], :])
        meta = meta_ref[s["rows"], :]
        s["x2"] = x1_ref[s["rows"], :] + meta[:, 2:3] * y_lo + meta[:, 3:4] * y_hi
        s["hn"] = _rms(s["x2"], gple_ref[...]).astype(BF16)

    def gate(s):
        s["gate"] = jnp.dot(s.pop("hn"), wpg_ref[...], preferred_element_type=F32)

    def finish(s):
        out_ref[s["rows"], :] = _rms(s.pop("x2") + _sigmoid(s.pop("gate")) * s.pop("pp"), gfin_ref[...])

    for stage in (embed, combine, gate, finish):
        for s in subs:
            stage(s)


def _final(x1, meta, p2d, ys_part, g_ple, w_pg, w_pp, g_fin, part, out_prev):
    t = x1.shape[0]
    tm = TM_FIN
    steps = ys_part.shape[0] // tm
    row = lambda w: pl.BlockSpec((tm, w), lambda i: (part * steps + i, 0))
    const = lambda a: pl.BlockSpec(a.shape, lambda i: (0,) * a.ndim)
    in_specs = [row(D_MODEL), row(LANES), row(PLE_DIM), pl.BlockSpec((tm, D_MODEL), lambda i: (i, 0)),
                const(g_ple), const(w_pg), const(w_pp), const(g_fin)]
    args = [x1, meta, p2d, ys_part, g_ple, w_pg, w_pp, g_fin]
    aliases = {}
    if out_prev is not None:
        in_specs.append(pl.BlockSpec(memory_space=pl.ANY))
        args.append(out_prev)
        aliases = {len(args) - 1: 0}
    return pl.pallas_call(
        _final_kernel,
        grid=(steps,),
        in_specs=in_specs,
        out_specs=row(D_MODEL),
        out_shape=jax.ShapeDtypeStruct((t, D_MODEL), F32),
        input_output_aliases=aliases,
        compiler_params=_cparams("arbitrary"),
    )(*args)


_PAIR_LO = np.array([0, 0, 0, 1, 1, 2], np.int32)
_PAIR_HI = np.array([1, 2, 3, 2, 3, 3], np.int32)


def _routing_tables(meta, counts_f, n_tiles):
    cls = meta[:, 0].astype(jnp.int32)
    rank = meta[:, 1].astype(jnp.int32)
    counts = counts_f[0, :MOE_CLASSES].astype(jnp.int32)
    tiles_per = (counts + TM_EXP - 1) // TM_EXP
    tile_end = jnp.cumsum(tiles_per)
    tile_start = tile_end - tiles_per
    class_ids = jnp.arange(MOE_CLASSES, dtype=jnp.int32)
    pos = jnp.sum(jnp.where(cls[:, None] == class_ids[None, :], (tile_start * TM_EXP)[None, :], 0), axis=1) + rank
    n_used = tile_end[-1:]
    tile_ids = jnp.arange(n_tiles, dtype=jnp.int32)
    tile_cls = jnp.minimum(jnp.sum((tile_end[None, :] <= tile_ids[:, None]).astype(jnp.int32), axis=1),
                           MOE_CLASSES - 1)
    grp = tile_cls // MOE_PAIRS
    pair = tile_cls % MOE_PAIRS
    tile_elo = grp * MOE_EPG + jnp.asarray(_PAIR_LO)[pair]
    tile_ehi = grp * MOE_EPG + jnp.asarray(_PAIR_HI)[pair]
    class_left = jnp.sum(jnp.where(tile_cls[:, None] == class_ids[None, :],
                                   (counts - (tile_ids[:, None] - tile_start[None, :]) * TM_EXP), 0), axis=1)
    tile_valid = jnp.clip(class_left, 0, TM_EXP)
    return (pos.astype(jnp.int32), n_used.astype(jnp.int32), tile_elo.astype(jnp.int32),
            tile_ehi.astype(jnp.int32), tile_valid.astype(jnp.int32))


def kernel(x, p, norm_mix_g, w_in, conv_w, conv_b, dt_bias, a_log, d_skip, ssd_norm_g, w_att_branch,
           w_ssd_branch, w_out, norm_ffn_g, w_router_group, b_router_group, w_router_expert,
           b_router_expert, w_exp_gate, w_exp_up, w_exp_down, norm_ple_g, w_ple_gate, w_ple_proj,
           final_norm_g):
    bsz, seq, _ = x.shape
    t = bsz * seq
    assert w_in.shape[0] == 1, "single-layer block"
    assert seq // ATT_PATTERNS[-1][1] == ATT_BLOCK and seq % TM_IN == 0
    x2d = x.reshape(t, D_MODEL)

    wi = w_in[0]
    c_dt = QKV_W + SSD_INNER + SSD_CONV_CH
    w_qkv = wi[:, :QKV_W].astype(BF16)
    w_wide = jnp.concatenate([wi[:, QKV_W:c_dt], wi[:, c_dt + SSD_HEADS:]], axis=1).astype(BF16)
    w_dt = jnp.pad(wi[:, c_dt:c_dt + SSD_HEADS], ((0, 0), (0, LANES - SSD_HEADS))).astype(BF16)
    row = lambda v: v.reshape(1, -1).astype(F32)

    qkv1, qkv2, qkv3, z, xbc, gates, dt_raw = _in_proj(x2d, row(norm_mix_g[0]), w_qkv, w_wide, w_dt,
                                                       conv_w[0], conv_b[0], bsz, seq)

    att = _attention(qkv1.reshape(bsz, seq, ATT_WIDTH), qkv2, qkv3, bsz, seq).reshape(t, ATT_OUT)

    dt_t = dt_raw[:, :SSD_HEADS].reshape(bsz, seq, SSD_GROUPS, SSD_HPG).transpose(0, 2, 3, 1)
    u = _ssd(xbc.reshape(bsz, seq, SSD_CONV_CH), z.reshape(bsz, seq, SSD_INNER), dt_t,
             dt_bias[0], a_log[0], d_skip[0], bsz, seq).reshape(t, SSD_INNER)

    w_r32 = jnp.pad(jnp.concatenate([w_router_group[0], w_router_expert[0]], axis=1),
                    ((0, 0), (0, LANES - MOE_GROUPS - MOE_EXPERTS))).astype(F32)
    w_r_hi = w_r32.astype(BF16)
    w_r_lo = (w_r32 - w_r_hi.astype(F32)).astype(BF16)
    w_r = jnp.concatenate([jnp.concatenate([w_r_hi, w_r_lo], axis=1),
                           jnp.concatenate([w_r_hi, jnp.zeros_like(w_r_lo)], axis=1)], axis=0)
    b_r = jnp.pad(jnp.concatenate([b_router_group[0], b_router_expert[0]]),
                  (0, LANES - MOE_GROUPS - MOE_EXPERTS)).reshape(1, LANES).astype(F32)
    x1, meta, counts = _post(att, u, gates, x2d,
                             w_att_branch[0].astype(BF16), w_ssd_branch[0].astype(BF16), w_out[0].astype(BF16),
                             row(ssd_norm_g[0]), row(norm_ffn_g[0]), w_r, b_r)

    n_tiles = t // TM_EXP + MOE_CLASSES
    pos, n_used, tile_elo, tile_ehi, tile_valid = _routing_tables(meta, counts, n_tiles)
    xs = _sc_scatter_rows(x1, pos, n_tiles * TM_EXP)
    ys = _experts(n_used, tile_elo, tile_ehi, tile_valid, xs, row(norm_ffn_g[0]),
                  w_exp_gate[0].astype(BF16), w_exp_up[0].astype(BF16), w_exp_down[0].astype(BF16))
    out = None
    part_len = t // FINAL_PARTS
    for part in range(FINAL_PARTS):
        ys_part = _sc_gather_rows(ys, pos[part * part_len:(part + 1) * part_len])
        out = _final(x1, meta, p[0].reshape(t, PLE_DIM), ys_part, row(norm_ple_g[0]),
                     w_ple_gate[0].astype(BF16), w_ple_proj[0].astype(BF16), row(final_norm_g), part, out)
    return out.reshape(bsz, seq, D_MODEL)
```

```python
import functools

import numpy as np
import jax
import jax.numpy as jnp
from jax import lax
from jax.experimental import pallas as pl
from jax.experimental.pallas import tpu as pltpu
from jax.experimental.pallas import tpu_sc as plsc

F32 = jnp.float32
BF16 = jnp.bfloat16

D_MODEL = 1024
PLE_DIM = 256
RMS_EPS = 1e-6

ATT_PATTERNS = ((128, 1), (512, 4), (2048, 16))
ATT_GROUPS = 3
ATT_HPG = 8
ATT_HEAD_DIM = 64
ATT_WIDTH = ATT_GROUPS * ATT_HPG * ATT_HEAD_DIM
ATT_OUT = ATT_HPG * ATT_HEAD_DIM
ATT_BLOCK = 128
ATT_LOCKSTEP = 2
ALIBI_MAX_BIAS = 8.0
QKV_W = 3 * ATT_WIDTH

SSD_INNER = 2048
SSD_HEADS = 32
SSD_GROUPS = 8
SSD_HPG = 4
SSD_HEAD_DIM = 64
SSD_STATE = 128
SSD_CONV = 4
SSD_CHUNK = 128
SSD_CONV_CH = SSD_INNER + 2 * SSD_GROUPS * SSD_STATE
SSD_GW = SSD_HPG * SSD_HEAD_DIM
SSD_LOCKSTEP = 16

MOE_GROUPS = 4
MOE_EPG = 4
MOE_EXPERTS = 16
MOE_HIDDEN = 512
MOE_PAIRS = 6
MOE_CLASSES = MOE_GROUPS * MOE_PAIRS

LANES = 128
NEG_BIG = -1e30
VMEM_LIMIT = 56 * 1024 * 1024

TM_IN = 1024
TN_IN = 512
TN_WIDE = 1024
TM_POST = 512
TM_EXP = 256
TM_FIN = 512
FINAL_PARTS = 4
SC_WINDOW = 128
SC_SUB = 32


def _cparams(*sem):
    return pltpu.CompilerParams(dimension_semantics=sem, vmem_limit_bytes=VMEM_LIMIT)


def _sigmoid(x):
    return 0.5 * jnp.tanh(0.5 * x) + 0.5


def _silu(x):
    h = 0.5 * x
    return h + h * jnp.tanh(h)


def _rms(x, g):
    ms = jnp.mean(x * x, axis=-1, keepdims=True)
    return x * lax.rsqrt(ms + RMS_EPS) * g


_IN_SEG = (3, 3, 3, SSD_INNER // TN_WIDE, SSD_CONV_CH // TN_WIDE, 2 * D_MODEL // TN_WIDE)
_IN_START = tuple(int(v) for v in np.cumsum((0,) + _IN_SEG))
_IN_NARROW = _IN_START[3]


def _inproj_kernel(x_ref, g_ref, wq_ref, ww_ref, wdt_ref, cw_ref, cb_ref, qkv1_ref, qkv2_ref, qkv3_ref, z_ref,
                   xbc_ref, gates_ref, dt_ref, h_ref, hcol_ref, halo_ref, *, per_seq):
    j = pl.program_id(1)
    tm = x_ref.shape[0]

    @pl.when(j == 0)
    def _():
        h = _rms(x_ref[...], g_ref[...])
        hb = h.astype(BF16)
        h_ref[0] = hb
        dt_ref[...] = jnp.dot(hb, wdt_ref[...], preferred_element_type=F32)
        ncb = hcol_ref.shape[0]
        for c in range(ncb):
            hcol_ref[c] = h[:, c * LANES:(c + 1) * LANES]
        for slot, (_, dil) in enumerate(ATT_PATTERNS[1:], start=1):
            rows = tm // dil
            for r in range(dil):
                for c in range(ncb):
                    h_ref[slot, r * rows:(r + 1) * rows, c * LANES:(c + 1) * LANES] = (
                        hcol_ref[c, pl.ds(r, rows, stride=dil), :].astype(BF16))

    def segment(k, fn):
        @pl.when((j >= _IN_START[k]) & (j < _IN_START[k + 1]))
        def _():
            fn()

    def narrow(slot, store):
        def fn():
            store(jnp.dot(h_ref[slot], wq_ref[...], preferred_element_type=F32).astype(BF16))
        return fn

    def wide(ref):
        def fn():
            ref[...] = jnp.dot(h_ref[0], ww_ref[...], preferred_element_type=F32).astype(BF16)
        return fn

    def store_plain(ref):
        def store(res):
            ref[...] = res
        return store

    def store_grouped(ref, dil):
        def store(res):
            ref[0] = res.reshape(dil, tm // dil, res.shape[1])
        return store

    def conv_silu():
        jb = j - _IN_START[4]
        nsub = 4
        rs = tm // nsub
        res = [jnp.dot(h_ref[0, k * rs:(k + 1) * rs, :], ww_ref[...], preferred_element_type=F32)
               for k in range(nsub)]
        seq_start = pl.program_id(0) % per_seq == 0
        prev = jnp.where(seq_start, 0.0, halo_ref[jb])
        row8 = lax.broadcasted_iota(jnp.int32, (8, 1), 0)
        cw = cw_ref[...]
        for k in range(nsub):
            acc = cb_ref[...] + cw[SSD_CONV - 1:SSD_CONV, :] * res[k]
            for sh in range(1, SSD_CONV):
                rolled = pltpu.roll(res[k], sh, 0)
                top = jnp.where(row8 < sh, pltpu.roll(prev, sh, 0), rolled[0:8])
                shifted = jnp.concatenate([top, rolled[8:]], axis=0)
                acc = acc + cw[SSD_CONV - 1 - sh:SSD_CONV - sh, :] * shifted
            xbc_ref[k * rs:(k + 1) * rs, :] = acc.astype(BF16)
            prev = res[k][rs - 8:rs]
        halo_ref[jb] = prev

    segment(0, narrow(0, store_plain(qkv1_ref)))
    segment(1, narrow(1, store_grouped(qkv2_ref, ATT_PATTERNS[1][1])))
    segment(2, narrow(2, store_grouped(qkv3_ref, ATT_PATTERNS[2][1])))
    segment(3, wide(z_ref))
    segment(4, conv_silu)
    segment(5, wide(gates_ref))


def _in_proj(x2d, g, w_qkv, w_wide, w_dt, conv_w, conv_b, bsz, seq):
    t = x2d.shape[0]
    nj = _IN_START[-1]
    per_seq = seq // TM_IN
    d2, d3 = ATT_PATTERNS[1][1], ATT_PATTERNS[2][1]

    def seg_map(k):
        return lambda i, j: (i, jnp.clip(j - _IN_START[k], 0, _IN_SEG[k] - 1))

    def grp_map(k):
        return lambda i, j: (i // per_seq, 0, i % per_seq, jnp.clip(j - _IN_START[k], 0, _IN_SEG[k] - 1))

    def wq_map(i, j):
        jj = jnp.minimum(j, _IN_NARROW - 1)
        return (0, (jj % 3) * ATT_GROUPS + jj // 3)

    out_shapes = (
        jax.ShapeDtypeStruct((t, ATT_WIDTH), BF16),
        jax.ShapeDtypeStruct((bsz, d2, seq // d2, ATT_WIDTH), BF16),
        jax.ShapeDtypeStruct((bsz, d3, seq // d3, ATT_WIDTH), BF16),
        jax.ShapeDtypeStruct((t, SSD_INNER), BF16),
        jax.ShapeDtypeStruct((t, SSD_CONV_CH), BF16),
        jax.ShapeDtypeStruct((t, 2 * D_MODEL), BF16),
        jax.ShapeDtypeStruct((t, LANES), F32),
    )
    out_specs = (
        pl.BlockSpec((TM_IN, TN_IN), seg_map(0)),
        pl.BlockSpec((1, d2, TM_IN // d2, TN_IN), grp_map(1)),
        pl.BlockSpec((1, d3, TM_IN // d3, TN_IN), grp_map(2)),
        pl.BlockSpec((TM_IN, TN_WIDE), seg_map(3)),
        pl.BlockSpec((TM_IN, TN_WIDE), seg_map(4)),
        pl.BlockSpec((TM_IN, TN_WIDE), seg_map(5)),
        pl.BlockSpec((TM_IN, LANES), lambda i, j: (i, 0)),
    )
    conv_map = lambda i, j: (0, jnp.clip(j - _IN_START[4], 0, _IN_SEG[4] - 1))
    return pl.pallas_call(
        functools.partial(_inproj_kernel, per_seq=per_seq),
        grid=(t // TM_IN, nj),
        in_specs=[
            pl.BlockSpec((TM_IN, D_MODEL), lambda i, j: (i, 0)),
            pl.BlockSpec((1, D_MODEL), lambda i, j: (0, 0)),
            pl.BlockSpec((D_MODEL, TN_IN), wq_map),
            pl.BlockSpec((D_MODEL, TN_WIDE), lambda i, j: (0, jnp.maximum(j - _IN_NARROW, 0))),
            pl.BlockSpec((D_MODEL, LANES), lambda i, j: (0, 0)),
            pl.BlockSpec((SSD_CONV, TN_WIDE), conv_map),
            pl.BlockSpec((1, TN_WIDE), conv_map),
        ],
        out_specs=out_specs,
        out_shape=out_shapes,
        scratch_shapes=[pltpu.VMEM((3, TM_IN, D_MODEL), BF16),
                        pltpu.VMEM((D_MODEL // LANES, TM_IN, LANES), F32),
                        pltpu.VMEM((_IN_SEG[4], 8, TN_WIDE), F32)],
        compiler_params=_cparams("arbitrary", "arbitrary"),
    )(x2d, g, w_qkv, w_wide, w_dt, conv_w.astype(F32), conv_b.reshape(1, SSD_CONV_CH).astype(F32))


def _att_bias_tables():
    h = np.arange(1, ATT_GROUPS * ATT_HPG + 1, dtype=np.float32)
    slopes = np.exp2(-ALIBI_MAX_BIAS * h / (ATT_GROUPS * ATT_HPG)).astype(np.float32).reshape(ATT_GROUPS, ATT_HPG)
    qi = np.arange(ATT_BLOCK)[:, None] + ATT_BLOCK
    kj = np.arange(2 * ATT_BLOCK)[None, :]
    delta = qi - kj
    tabs = []
    for g, (window, dil) in enumerate(ATT_PATTERNS):
        span = window // dil
        band = (delta >= 0) & (delta <= span)
        bias = (-slopes[g][:, None, None] * (delta * dil).astype(np.float32)[None]).astype(np.float32)
        with_prev = np.where(band[None], bias, np.float32(NEG_BIG))
        first = np.where((band & (kj >= ATT_BLOCK))[None], bias, np.float32(NEG_BIG))
        tabs.append(np.stack([first, with_prev]).astype(np.float32))
    return tabs[0], tabs[1], tabs[2][1][:, :, ATT_BLOCK:]


def _att_units(units):
    lane = lax.broadcasted_iota(jnp.int32, (1, LANES), 1)
    low = lane < ATT_HEAD_DIM
    scale = ATT_HEAD_DIM ** -0.5
    qmask = (jnp.where(low, scale, 0.0).astype(BF16), jnp.where(low, 0.0, scale).astype(BF16))
    heads = [(q, k2, v2, tab_fn, hh) for q, k2, v2, tab_fn in units for hh in range(2)]
    scores = [lax.dot_general(q * qmask[hh], k2, (((1,), (1,)), ((), ())), preferred_element_type=F32)
              for q, k2, _, _, hh in heads]
    probs = []
    for s, (_, _, _, tab_fn, hh) in zip(scores, heads):
        s = s + tab_fn(hh)
        m = jnp.max(s, axis=-1, keepdims=True)
        e = jnp.exp(s - m)
        probs.append((e.astype(BF16), m, jnp.sum(e, axis=-1, keepdims=True)))
    pvs = [jnp.dot(e, v2, preferred_element_type=F32) for (e, _, _), (_, _, v2, _, _) in zip(probs, heads)]
    outs = [pv / den for pv, (_, _, den) in zip(pvs, probs)]
    lses = [m + jnp.log(den) for _, m, den in probs]
    return [(jnp.where(low, outs[2 * i], outs[2 * i + 1]), jnp.where(low, lses[2 * i], lses[2 * i + 1]))
            for i in range(len(units))]


def _att_kernel(q1, k1, v1, q2, k2, v2, q3, k3, v3, t1, t2, t3, out_ref,
                o1, l1, o2p, l2p, o2n, l2n, o3p, l3p, o3n, l3n, *, seq):
    blk = ATT_BLOCK
    d2, d3 = ATT_PATTERNS[1][1], ATT_PATTERNS[2][1]
    nb1, nb2 = seq // blk, seq // d2 // blk

    def rows(n):
        return pl.ds(pl.multiple_of(n * blk, blk), blk)

    assert nb1 == d2 * nb2 == d3

    def unit_body(i, _):
        units, dests = [], []
        for k in range(ATT_LOCKSTEP):
            u = i * ATT_LOCKSTEP + k
            cur, prv = rows(u), rows(jnp.maximum(u - 1, 0))
            sel = jnp.minimum(u, 1)
            units.append((q1[0, cur, :], jnp.concatenate([k1[0, prv, :], k1[0, cur, :]], axis=0),
                          jnp.concatenate([v1[0, prv, :], v1[0, cur, :]], axis=0),
                          lambda hh, sel=sel: t1[sel, hh]))
            r, n = u // nb2, u % nb2
            cur2, prv2 = rows(n), rows(jnp.maximum(n - 1, 0))
            sel2 = jnp.minimum(n, 1)
            units.append((q2[0, r, cur2, :], jnp.concatenate([k2[0, r, prv2, :], k2[0, r, cur2, :]], axis=0),
                          jnp.concatenate([v2[0, r, prv2, :], v2[0, r, cur2, :]], axis=0),
                          lambda hh, sel2=sel2: t2[sel2, hh]))
            units.append((q3[0, u], k3[0, u], v3[0, u], lambda hh: t3[hh]))
            dests += [(o1, l1, cur), (o2p, l2p, cur), (o3p, l3p, cur)]
        for (o, l), (o_ref, l_ref, where) in zip(_att_units(units), dests):
            o_ref[where, :] = o
            l_ref[where, :] = l
        return 0

    lax.fori_loop(0, nb1 // ATT_LOCKSTEP, unit_body, 0)

    for dil, pairs in ((d2, ((o2p, o2n), (l2p, l2n))), (d3, ((o3p, o3n), (l3p, l3n)))):
        n_sub = seq // dil
        for r in range(dil):
            for src, dst in pairs:
                dst[pl.ds(r, n_sub, stride=dil), :] = src[r * n_sub:(r + 1) * n_sub, :]

    mrows = 2 * blk

    def merge(c, _):
        rr = pl.ds(pl.multiple_of(c * mrows, mrows), mrows)
        la, lb, lc = l1[rr, :], l2n[rr, :], l3n[rr, :]
        lm = jnp.maximum(jnp.maximum(la, lb), lc)
        ea, eb, ec = jnp.exp(la - lm), jnp.exp(lb - lm), jnp.exp(lc - lm)
        att = (ea * o1[rr, :] + eb * o2n[rr, :] + ec * o3n[rr, :]) / (ea + eb + ec)
        out_ref[0, rr, :] = att.astype(BF16)
        return 0

    lax.fori_loop(0, seq // mrows, merge, 0)


def _attention(qkv1, qkv2, qkv3, bsz, seq):
    t1, t2, t3 = (jnp.asarray(t) for t in _att_bias_tables())
    d2, d3 = ATT_PATTERNS[1][1], ATT_PATTERNS[2][1]
    npair = ATT_HPG // 2
    in_arrays, in_specs = [], []
    for arr, lead in ((qkv1, ()), (qkv2, (d2,)), (qkv3, (d3,))):
        n_rows = arr.shape[-2]
        for sel in range(3):
            zeros = (0,) * len(lead)
            in_arrays.append(arr)
            in_specs.append(pl.BlockSpec((1,) + lead + (n_rows, LANES),
                                         lambda hp, b, sel=sel, zeros=zeros: (b,) + zeros + (0, sel * npair + hp)))
    in_arrays += [t1, t2, t3]
    in_specs += [pl.BlockSpec((2, 2, ATT_BLOCK, 2 * ATT_BLOCK), lambda hp, b: (0, hp, 0, 0)),
                 pl.BlockSpec((2, 2, ATT_BLOCK, 2 * ATT_BLOCK), lambda hp, b: (0, hp, 0, 0)),
                 pl.BlockSpec((2, ATT_BLOCK, ATT_BLOCK), lambda hp, b: (hp, 0, 0))]
    return pl.pallas_call(
        functools.partial(_att_kernel, seq=seq),
        grid=(npair, bsz),
        in_specs=in_specs,
        out_specs=pl.BlockSpec((1, seq, LANES), lambda hp, b: (b, 0, hp)),
        out_shape=jax.ShapeDtypeStruct((bsz, seq, ATT_OUT), BF16),
        scratch_shapes=[pltpu.VMEM((seq, LANES), F32)] * 10,
        compiler_params=_cparams("arbitrary", "arbitrary"),
    )(*in_arrays)


def _softplus(x):
    return jnp.maximum(x, 0.0) + jnp.log1p(jnp.exp(-jnp.abs(x)))


def _head_spread_table():
    gw, n_src = SSD_GW, LANES
    tab = np.zeros((2 * n_src, 2 * gw), np.float32)
    for half in range(2):
        for blk, src0 in enumerate((4, 8)):
            for col in range(gw):
                tab[half * n_src + src0 + col // SSD_HEAD_DIM, blk * gw + col] = 1.0
    return tab


def _ssd_kernel(x_ref, b_ref, c_ref, z_ref, dt_ref, spread_ref, dtb_ref, alog_ref, dsk_ref, u_ref, carry_ref,
                *, n_chunks):
    L = SSD_CHUNK
    gw = SSD_GW
    carry_ref[...] = jnp.zeros(carry_ref.shape, F32)

    a_neg = -jnp.exp(alog_ref[0])
    dtb = dtb_ref[0]
    dsk = dsk_ref[0]
    ri = lax.broadcasted_iota(jnp.int32, (L, L), 0)
    ci = lax.broadcasted_iota(jnp.int32, (L, L), 1)
    upper_incl = (ri <= ci).astype(F32)
    causal = ri >= ci
    lane = lax.broadcasted_iota(jnp.int32, (1, LANES), 1)
    low = lane < SSD_HEAD_DIM
    heads = range(SSD_HPG)

    def load(s):
        rows = pl.ds(s["r0"], L)
        s["xs"] = _silu(x_ref[0, rows, :].astype(F32))
        s["xs_b"] = s["xs"].astype(BF16)
        s["bm"] = _silu(b_ref[0, rows, :].astype(F32))
        s["cm_b"] = _silu(c_ref[0, rows, :].astype(F32)).astype(BF16)

    def decay_cumsum(s):
        dt = _softplus(dt_ref[0, 0, :, pl.ds(s["r0"], L)] + dtb)
        rows8 = jnp.concatenate([dt * a_neg, dt], axis=0)
        cs8 = jnp.dot(rows8, upper_incl, preferred_element_type=F32, precision=lax.Precision.HIGHEST)
        s["dt"], s["acs_t"] = dt, cs8[0:SSD_HPG]

    def decay_spread(s):
        dt, acs_t = s["dt"], s["acs_t"]
        t16 = jnp.concatenate([acs_t, acs_t, acs_t, dt], axis=0)
        cols = jnp.concatenate([t16, jnp.zeros((L - 16, L), F32)], axis=0).T
        dt_at8 = pltpu.roll(cols, LANES - 4, 1)
        comb = jnp.where(lane < 8, jnp.exp(cols), jnp.exp(cols[L - 1:L, :] - cols) * dt_at8)
        hi = comb.astype(BF16)
        lo = (comb - hi.astype(F32)).astype(BF16)
        s["spread"] = jnp.dot(jnp.concatenate([hi, lo], axis=1), spread_ref[...], preferred_element_type=F32)
        s["acs_cols"] = cols

    def scores(s):
        s["bm_t"] = s.pop("bm").T.astype(BF16)
        s["cb"] = jnp.dot(s["cm_b"], s["bm_t"], preferred_element_type=F32)

    def intra(s):
        spread, acs_t, dt, cb, cols = s["spread"], s.pop("acs_t"), s.pop("dt"), s.pop("cb"), s.pop("acs_cols")
        xs, xs_b = s["xs"], s.pop("xs_b")
        mixes = []
        for j in heads:
            seg = cols[:, j:j + 1] - acs_t[j:j + 1, :]
            mixes.append((jnp.exp(jnp.where(causal, seg, -jnp.inf)) * (cb * dt[j:j + 1, :])).astype(BF16))
        halves = []
        for hp in range(SSD_HPG // 2):
            yy = jnp.dot(jnp.concatenate(mixes[2 * hp:2 * hp + 2], axis=0), xs_b[:, hp * LANES:(hp + 1) * LANES],
                         preferred_element_type=F32)
            halves.append(jnp.where(low, yy[:L], yy[L:]))
        s["y"] = jnp.concatenate(halves, axis=1)
        s["st_new"] = jnp.dot(s.pop("bm_t"), (xs * spread[:, gw:]).astype(BF16),
                              preferred_element_type=F32)

    def inter(s):
        eacs_bc = s.pop("spread")[:, :gw]
        carry = carry_ref[...]
        y = s.pop("y") + jnp.dot(s.pop("cm_b"), carry.astype(BF16), preferred_element_type=F32) * eacs_bc
        carry_ref[...] = carry * eacs_bc[L - 1:L, :] + s.pop("st_new")
        y = y + s.pop("xs") * dsk
        rows = pl.ds(s["r0"], L)
        u_ref[0, rows, :] = (y * _silu(z_ref[0, rows, :].astype(F32))).astype(BF16)

    def body(i, _):
        states = [{"c": i * SSD_LOCKSTEP + k, "r0": pl.multiple_of((i * SSD_LOCKSTEP + k) * L, L)}
                  for k in range(SSD_LOCKSTEP)]
        for stage in (load, decay_cumsum, decay_spread, scores, intra, inter):
            for s in states:
                stage(s)
        return 0

    lax.fori_loop(0, n_chunks // SSD_LOCKSTEP, body, 0)


def _ssd(xbc, z, dt_t, dt_bias, a_log, d_skip, bsz, seq):
    gw = SSD_GW
    nxb = SSD_INNER // SSD_STATE
    dsk = jnp.repeat(d_skip.astype(F32), SSD_HEAD_DIM).reshape(SSD_GROUPS, 1, gw)
    dtb = dt_bias.astype(F32).reshape(SSD_GROUPS, SSD_HPG, 1)
    alog = a_log.astype(F32).reshape(SSD_GROUPS, SSD_HPG, 1)
    spread = jnp.asarray(_head_spread_table(), BF16)
    x_map = lambda b, g: (b, 0, g)
    bm_map = lambda b, g: (b, 0, nxb + g)
    cm_map = lambda b, g: (b, 0, nxb + SSD_GROUPS + g)
    return pl.pallas_call(
        functools.partial(_ssd_kernel, n_chunks=seq // SSD_CHUNK),
        grid=(bsz, SSD_GROUPS),
        in_specs=[
            pl.BlockSpec((1, seq, gw), x_map),
            pl.BlockSpec((1, seq, SSD_STATE), bm_map),
            pl.BlockSpec((1, seq, SSD_STATE), cm_map),
            pl.BlockSpec((1, seq, gw), x_map),
            pl.BlockSpec((1, 1, SSD_HPG, seq), lambda b, g: (b, g, 0, 0)),
            pl.BlockSpec(spread.shape, lambda b, g: (0, 0)),
            pl.BlockSpec((1, SSD_HPG, 1), lambda b, g: (g, 0, 0)),
            pl.BlockSpec((1, SSD_HPG, 1), lambda b, g: (g, 0, 0)),
            pl.BlockSpec((1, 1, gw), lambda b, g: (g, 0, 0)),
        ],
        out_specs=pl.BlockSpec((1, seq, gw), x_map),
        out_shape=jax.ShapeDtypeStruct((bsz, seq, SSD_INNER), BF16),
        scratch_shapes=[pltpu.VMEM((SSD_STATE, gw), F32)],
        compiler_params=_cparams("arbitrary", "arbitrary"),
    )(xbc, xbc, xbc, z, dt_t, spread, dtb, alog, dsk)


def _first_index_of_max(vals, lane_f):
    m = jnp.max(vals, axis=-1, keepdims=True)
    idx = jnp.min(jnp.where(vals == m, lane_f, float(LANES)), axis=-1, keepdims=True)
    return m, idx


def _post_kernel(att_ref, u_ref, gates_ref, x_ref, watt_ref, wssd_ref, wout_ref,
                 gssd_ref, gffn_ref, wr_ref, br_ref, x1_ref, meta_ref, cnt_ref, run_ref):
    i = pl.program_id(0)

    @pl.when(i == 0)
    def _():
        run_ref[...] = jnp.zeros(run_ref.shape, F32)

    tm = x_ref.shape[0]
    nsub = 2
    subs = [{"rows": pl.ds(k * (tm // nsub), tm // nsub)} for k in range(nsub)]

    def att_branch(s):
        s["y_att"] = jnp.dot(att_ref[s["rows"], :], watt_ref[...], preferred_element_type=F32)

    def ssd_branch(s):
        ssd = _rms(u_ref[s["rows"], :].astype(F32), gssd_ref[...])
        s["y_ssd"] = jnp.dot(ssd.astype(BF16), wssd_ref[...], preferred_element_type=F32)

    def mix_out(s):
        gates = gates_ref[s["rows"], :].astype(F32)
        merged = _sigmoid(gates[:, :D_MODEL]) * s.pop("y_att") + _sigmoid(gates[:, D_MODEL:]) * s.pop("y_ssd")
        x1 = x_ref[s["rows"], :] + jnp.dot(merged.astype(BF16), wout_ref[...], preferred_element_type=F32)
        x1_ref[s["rows"], :] = x1
        s["h2"] = _rms(x1, gffn_ref[...])

    def router(s):
        h2 = s.pop("h2")
        hi = h2.astype(BF16)
        lo = (h2 - hi.astype(F32)).astype(BF16)
        both = jnp.dot(jnp.concatenate([hi, lo], axis=1), wr_ref[...], preferred_element_type=F32)
        s["logits"] = both[:, :LANES] + both[:, LANES:]

    for stage in (att_branch, ssd_branch, mix_out, router):
        for s in subs:
            stage(s)
    logits = jnp.concatenate([s["logits"] for s in subs], axis=0) + br_ref[...]
    lane = lax.broadcasted_iota(jnp.int32, (tm, LANES), 1)
    lane_f = lane.astype(F32)
    ninf = -jnp.inf
    gl = jnp.where(lane < MOE_GROUPS, logits, ninf)
    gmax, gidx = _first_index_of_max(gl, lane_f)
    g_val = 1.0 / jnp.sum(jnp.exp(gl - gmax), axis=-1, keepdims=True)
    base = MOE_GROUPS + MOE_EPG * gidx
    el = jnp.where((lane_f >= base) & (lane_f < base + MOE_EPG), logits, ninf)
    e1, i1 = _first_index_of_max(el, lane_f)
    e2, i2 = _first_index_of_max(jnp.where(lane_f == i1, ninf, el), lane_f)
    t2 = jnp.exp(e2 - e1)
    w1 = g_val / (1.0 + t2)
    w2 = g_val * t2 / (1.0 + t2)
    a1, a2 = i1 - base, i2 - base
    lo, hi = jnp.minimum(a1, a2), jnp.maximum(a1, a2)
    c_lo = jnp.where(a1 < a2, w1, w2)
    c_hi = jnp.where(a1 < a2, w2, w1)
    pair = lo * (7.0 - lo) * 0.5 + (hi - lo - 1.0)
    cls = gidx * MOE_PAIRS + pair

    onehot = (lane_f == cls)
    oh_b = jnp.where(onehot, 1.0, 0.0).astype(BF16)
    rr = lax.broadcasted_iota(jnp.int32, (tm, tm), 0)
    cc = lax.broadcasted_iota(jnp.int32, (tm, tm), 1)
    strict = jnp.where(rr > cc, 1.0, 0.0).astype(BF16)
    prefix = jnp.dot(strict, oh_b, preferred_element_type=F32) + run_ref[...]
    rank = jnp.sum(jnp.where(onehot, prefix, 0.0), axis=-1, keepdims=True)
    run = run_ref[...] + jnp.sum(oh_b.astype(F32), axis=0, keepdims=True)
    run_ref[...] = run
    cnt_ref[...] = jnp.broadcast_to(run, cnt_ref.shape)

    meta = jnp.where(lane == 0, cls, jnp.where(lane == 1, rank, jnp.where(lane == 2, c_lo,
                     jnp.where(lane == 3, c_hi, 0.0))))
    meta_ref[...] = meta


def _post(att, u, gates, x2d, w_att, w_ssd, w_out, g_ssd, g_ffn, w_r, b_r):
    t = x2d.shape[0]
    tm = TM_POST
    row = lambda w: pl.BlockSpec((tm, w), lambda i: (i, 0))
    const = lambda a: pl.BlockSpec(a.shape, lambda i: (0,) * a.ndim)
    return pl.pallas_call(
        _post_kernel,
        grid=(t // tm,),
        in_specs=[row(ATT_OUT), row(SSD_INNER), row(2 * D_MODEL), row(D_MODEL),
                  const(w_att), const(w_ssd), const(w_out), const(g_ssd), const(g_ffn), const(w_r), const(b_r)],
        out_specs=(row(D_MODEL), row(LANES), pl.BlockSpec((8, LANES), lambda i: (0, 0))),
        out_shape=(jax.ShapeDtypeStruct((t, D_MODEL), F32),
                   jax.ShapeDtypeStruct((t, LANES), F32),
                   jax.ShapeDtypeStruct((8, LANES), F32)),
        scratch_shapes=[pltpu.VMEM((1, LANES), F32)],
        compiler_params=_cparams("arbitrary"),
    )(att, u, gates, x2d, w_att, w_ssd, w_out, g_ssd, g_ffn, w_r, b_r)


def _pack_pair(a, b):
    ua = lax.bitcast_convert_type(a.astype(BF16).astype(F32), jnp.uint32)
    ub = lax.bitcast_convert_type(b.astype(BF16).astype(F32), jnp.uint32)
    return (ua & jnp.uint32(0xFFFF0000)) | (ub >> 16)


def _unpack_pair(p):
    a = lax.bitcast_convert_type(p & jnp.uint32(0xFFFF0000), F32)
    b = lax.bitcast_convert_type(p << 16, F32)
    return a, b


def _sc_mesh():
    return plsc.VectorSubcoreMesh(core_axis_name="c", subcore_axis_name="s")


def _sc_permute_rows(data, idx, n_out, scatter):
    n, width = idx.shape[0], data.shape[1]
    mesh = _sc_mesh()
    workers = mesh.num_cores * mesh.num_subcores
    per_worker = n // SC_WINDOW // workers
    assert per_worker * workers * SC_WINDOW == n

    n_sub = SC_WINDOW // SC_SUB

    @pl.kernel(out_type=jax.ShapeDtypeStruct((n_out, width), data.dtype), mesh=mesh,
               scratch_types=[pltpu.VMEM((1, SC_WINDOW), jnp.int32), pltpu.VMEM((2, SC_SUB, width), data.dtype),
                              pltpu.SemaphoreType.DMA((2,)), pltpu.SemaphoreType.DMA((2,))])
    def permute(data_hbm, idx_hbm, out_hbm, idx_v, rows_v, sem_in, sem_out):
        worker = lax.axis_index("c") * mesh.num_subcores + lax.axis_index("s")

        @pl.loop(0, per_worker)
        def _(j):
            base = (worker * per_worker + j) * SC_WINDOW
            pltpu.sync_copy(idx_hbm.at[:, pl.ds(base, SC_WINDOW)], idx_v)

            def ends(k):
                sub_idx = idx_v.at[0, pl.ds(k * SC_SUB, SC_SUB)]
                plain = pl.ds(base + k * SC_SUB, SC_SUB)
                return (data_hbm.at[plain], out_hbm.at[sub_idx]) if scatter else (data_hbm.at[sub_idx],
                                                                                  out_hbm.at[plain])

            def read(k):
                return pltpu.make_async_copy(ends(k)[0], rows_v.at[k % 2], sem_in.at[k % 2])

            def write(k):
                return pltpu.make_async_copy(rows_v.at[k % 2], ends(k)[1], sem_out.at[k % 2])

            read(0).start()
            for k in range(n_sub):
                read(k).wait()
                if k + 1 < n_sub:
                    if k >= 1:
                        write(k - 1).wait()
                    read(k + 1).start()
                write(k).start()
            write(n_sub - 2).wait()
            write(n_sub - 1).wait()

    return permute(data, idx.reshape(1, n))


def _sc_scatter_rows(data, idx, n_out):
    return _sc_permute_rows(data, idx, n_out, scatter=True)


def _sc_gather_rows(data, idx):
    return _sc_permute_rows(data, idx, idx.shape[0], scatter=False)


def _expert_kernel(nused_ref, elo_ref, ehi_ref, valid_ref, xs_ref, g_ref, wg_lo, wu_lo, wd_lo, wg_hi, wu_hi, wd_hi,
                   ys_ref):
    del elo_ref, ehi_ref
    i = pl.program_id(0)

    @pl.when(i < nused_ref[0])
    def _():
        row = lax.broadcasted_iota(jnp.int32, (xs_ref.shape[0], 1), 0)
        xb = _rms(jnp.where(row < valid_ref[i], xs_ref[...], 0.0), g_ref[...]).astype(BF16)
        pre = [(jnp.dot(xb, wg[0], preferred_element_type=F32), jnp.dot(xb, wu[0], preferred_element_type=F32))
               for wg, wu in ((wg_lo, wu_lo), (wg_hi, wu_hi))]
        hid = [(_silu(gt) * up).astype(BF16) for gt, up in pre]
        y_lo, y_hi = (jnp.dot(h, wd[0], preferred_element_type=F32) for h, wd in zip(hid, (wd_lo, wd_hi)))
        ys_ref[...] = _pack_pair(y_lo, y_hi)

    @pl.when(i >= nused_ref[0])
    def _():
        ys_ref[...] = jnp.zeros(ys_ref.shape, jnp.uint32)


def _experts(n_used, tile_elo, tile_ehi, tile_valid, xs, g_ffn, w_gate, w_up, w_down):
    p_rows = xs.shape[0]
    tm = TM_EXP
    n_tiles = p_rows // tm

    def last_used(i, nu):
        return jnp.minimum(i, jnp.maximum(nu[0] - 1, 0))

    def row_map(i, nu, elo, ehi, valid):
        return (last_used(i, nu), 0)

    def wmap(which):
        def f(i, nu, elo, ehi, valid):
            return ((elo, ehi)[which][last_used(i, nu)], 0, 0)
        return f

    wspec_in = lambda which: pl.BlockSpec((1, D_MODEL, MOE_HIDDEN), wmap(which))
    wspec_out = lambda which: pl.BlockSpec((1, MOE_HIDDEN, D_MODEL), wmap(which))
    grid_spec = pltpu.PrefetchScalarGridSpec(
        num_scalar_prefetch=4,
        grid=(n_tiles,),
        in_specs=[pl.BlockSpec((tm, D_MODEL), row_map),
                  pl.BlockSpec((1, D_MODEL), lambda i, nu, elo, ehi, valid: (0, 0)),
                  wspec_in(0), wspec_in(0), wspec_out(0), wspec_in(1), wspec_in(1), wspec_out(1)],
        out_specs=pl.BlockSpec((tm, D_MODEL), lambda i, nu, elo, ehi, valid: (i, 0)),
    )
    return pl.pallas_call(
        _expert_kernel,
        grid_spec=grid_spec,
        out_shape=jax.ShapeDtypeStruct((p_rows, D_MODEL), jnp.uint32),
        compiler_params=_cparams("arbitrary"),
    )(n_used, tile_elo, tile_ehi, tile_valid, xs, g_ffn, w_gate, w_up, w_down, w_gate, w_up, w_down)


def _final_kernel(x1_ref, meta_ref, p_ref, yt_ref, gple_ref, wpg_ref, wpp_ref, gfin_ref, *rest):
    out_ref = rest[-1]
    tm = x1_ref.shape[0]
    nsub = 2
    subs = [{"rows": pl.ds(k * (tm // nsub), tm // nsub)} for k in range(nsub)]

    def embed(s):
        s["pp"] = jnp.dot(p_ref[s["rows"], :].astype(BF16), wpp_ref[...], preferred_element_type=F32)

    def combine(s):
        y_lo, y_hi = _unpack_pair(yt_ref[s["rows"], :])
        meta = meta_ref[s["rows"], :]
        s["x2"] = x1_ref[s["rows"], :] + meta[:, 2:3] * y_lo + meta[:, 3:4] * y_hi
        s["hn"] = _rms(s["x2"], gple_ref[...]).astype(BF16)

    def gate(s):
        s["gate"] = jnp.dot(s.pop("hn"), wpg_ref[...], preferred_element_type=F32)

    def finish(s):
        out_ref[s["rows"], :] = _rms(s.pop("x2") + _sigmoid(s.pop("gate")) * s.pop("pp"), gfin_ref[...])

    for stage in (embed, combine, gate, finish):
        for s in subs:
            stage(s)


def _final(x1, meta, p2d, ys_part, g_ple, w_pg, w_pp, g_fin, part, out_prev):
    t = x1.shape[0]
    tm = TM_FIN
    steps = ys_part.shape[0] // tm
    row = lambda w: pl.BlockSpec((tm, w), lambda i: (part * steps + i, 0))
    const = lambda a: pl.BlockSpec(a.shape, lambda i: (0,) * a.ndim)
    in_specs = [row(D_MODEL), row(LANES), row(PLE_DIM), pl.BlockSpec((tm, D_MODEL), lambda i: (i, 0)),
                const(g_ple), const(w_pg), const(w_pp), const(g_fin)]
    args = [x1, meta, p2d, ys_part, g_ple, w_pg, w_pp, g_fin]
    aliases = {}
    if out_prev is not None:
        in_specs.append(pl.BlockSpec(memory_space=pl.ANY))
        args.append(out_prev)
        aliases = {len(args) - 1: 0}
    return pl.pallas_call(
        _final_kernel,
        grid=(steps,),
        in_specs=in_specs,
        out_specs=row(D_MODEL),
        out_shape=jax.ShapeDtypeStruct((t, D_MODEL), F32),
        input_output_aliases=aliases,
        compiler_params=_cparams("arbitrary"),
    )(*args)


_PAIR_LO = np.array([0, 0, 0, 1, 1, 2], np.int32)
_PAIR_HI = np.array([1, 2, 3, 2, 3, 3], np.int32)


def _routing_tables(meta, counts_f, n_tiles):
    cls = meta[:, 0].astype(jnp.int32)
    rank = meta[:, 1].astype(jnp.int32)
    counts = counts_f[0, :MOE_CLASSES].astype(jnp.int32)
    tiles_per = (counts + TM_EXP - 1) // TM_EXP
    tile_end = jnp.cumsum(tiles_per)
    tile_start = tile_end - tiles_per
    class_ids = jnp.arange(MOE_CLASSES, dtype=jnp.int32)
    pos = jnp.sum(jnp.where(cls[:, None] == class_ids[None, :], (tile_start * TM_EXP)[None, :], 0), axis=1) + rank
    n_used = tile_end[-1:]
    tile_ids = jnp.arange(n_tiles, dtype=jnp.int32)
    tile_cls = jnp.minimum(jnp.sum((tile_end[None, :] <= tile_ids[:, None]).astype(jnp.int32), axis=1),
                           MOE_CLASSES - 1)
    grp = tile_cls // MOE_PAIRS
    pair = tile_cls % MOE_PAIRS
    tile_elo = grp * MOE_EPG + jnp.asarray(_PAIR_LO)[pair]
    tile_ehi = grp * MOE_EPG + jnp.asarray(_PAIR_HI)[pair]
    class_left = jnp.sum(jnp.where(tile_cls[:, None] == class_ids[None, :],
                                   (counts - (tile_ids[:, None] - tile_start[None, :]) * TM_EXP), 0), axis=1)
    tile_valid = jnp.clip(class_left, 0, TM_EXP)
    return (pos.astype(jnp.int32), n_used.astype(jnp.int32), tile_elo.astype(jnp.int32),
            tile_ehi.astype(jnp.int32), tile_valid.astype(jnp.int32))


def kernel(x, p, norm_mix_g, w_in, conv_w, conv_b, dt_bias, a_log, d_skip, ssd_norm_g, w_att_branch,
           w_ssd_branch, w_out, norm_ffn_g, w_router_group, b_router_group, w_router_expert,
           b_router_expert, w_exp_gate, w_exp_up, w_exp_down, norm_ple_g, w_ple_gate, w_ple_proj,
           final_norm_g):
    bsz, seq, _ = x.shape
    t = bsz * seq
    assert w_in.shape[0] == 1, "single-layer block"
    assert seq // ATT_PATTERNS[-1][1] == ATT_BLOCK and seq % TM_IN == 0
    x2d = x.reshape(t, D_MODEL)

    wi = w_in[0]
    c_dt = QKV_W + SSD_INNER + SSD_CONV_CH
    w_qkv = wi[:, :QKV_W].astype(BF16)
    w_wide = jnp.concatenate([wi[:, QKV_W:c_dt], wi[:, c_dt + SSD_HEADS:]], axis=1).astype(BF16)
    w_dt = jnp.pad(wi[:, c_dt:c_dt + SSD_HEADS], ((0, 0), (0, LANES - SSD_HEADS))).astype(BF16)
    row = lambda v: v.reshape(1, -1).astype(F32)

    qkv1, qkv2, qkv3, z, xbc, gates, dt_raw = _in_proj(x2d, row(norm_mix_g[0]), w_qkv, w_wide, w_dt,
                                                       conv_w[0], conv_b[0], bsz, seq)

    att = _attention(qkv1.reshape(bsz, seq, ATT_WIDTH), qkv2, qkv3, bsz, seq).reshape(t, ATT_OUT)

    dt_t = dt_raw[:, :SSD_HEADS].reshape(bsz, seq, SSD_GROUPS, SSD_HPG).transpose(0, 2, 3, 1)
    u = _ssd(xbc.reshape(bsz, seq, SSD_CONV_CH), z.reshape(bsz, seq, SSD_INNER), dt_t,
             dt_bias[0], a_log[0], d_skip[0], bsz, seq).reshape(t, SSD_INNER)

    w_r32 = jnp.pad(jnp.concatenate([w_router_group[0], w_router_expert[0]], axis=1),
                    ((0, 0), (0, LANES - MOE_GROUPS - MOE_EXPERTS))).astype(F32)
    w_r_hi = w_r32.astype(BF16)
    w_r_lo = (w_r32 - w_r_hi.astype(F32)).astype(BF16)
    w_r = jnp.concatenate([jnp.concatenate([w_r_hi, w_r_lo], axis=1),
                           jnp.concatenate([w_r_hi, jnp.zeros_like(w_r_lo)], axis=1)], axis=0)
    b_r = jnp.pad(jnp.concatenate([b_router_group[0], b_router_expert[0]]),
                  (0, LANES - MOE_GROUPS - MOE_EXPERTS)).reshape(1, LANES).astype(F32)
    x1, meta, counts = _post(att, u, gates, x2d,
                             w_att_branch[0].astype(BF16), w_ssd_branch[0].astype(BF16), w_out[0].astype(BF16),
                             row(ssd_norm_g[0]), row(norm_ffn_g[0]), w_r, b_r)

    n_tiles = t // TM_EXP + MOE_CLASSES
    pos, n_used, tile_elo, tile_ehi, tile_valid = _routing_tables(meta, counts, n_tiles)
    xs = _sc_scatter_rows(x1, pos, n_tiles * TM_EXP)
    ys = _experts(n_used, tile_elo, tile_ehi, tile_valid, xs, row(norm_ffn_g[0]),
                  w_exp_gate[0].astype(BF16), w_exp_up[0].astype(BF16), w_exp_down[0].astype(BF16))
    out = None
    part_len = t // FINAL_PARTS
    for part in range(FINAL_PARTS):
        ys_part = _sc_gather_rows(ys, pos[part * part_len:(part + 1) * part_len])
        out = _final(x1, meta, p[0].reshape(t, PLE_DIM), ys_part, row(norm_ple_g[0]),
                     w_ple_gate[0].astype(BF16), w_ple_proj[0].astype(BF16), row(final_norm_g), part, out)
    return out.reshape(bsz, seq, D_MODEL)
```

```python
import functools

import numpy as np
import jax
import jax.numpy as jnp
from jax import lax
from jax.experimental import pallas as pl
from jax.experimental.pallas import tpu as pltpu
from jax.experimental.pallas import tpu_sc as plsc

F32 = jnp.float32
BF16 = jnp.bfloat16

D_MODEL = 1024
PLE_DIM = 256
RMS_EPS = 1e-6

ATT_PATTERNS = ((128, 1), (512, 4), (2048, 16))
ATT_GROUPS = 3
ATT_HPG = 8
ATT_HEAD_DIM = 64
ATT_WIDTH = ATT_GROUPS * ATT_HPG * ATT_HEAD_DIM
ATT_OUT = ATT_HPG * ATT_HEAD_DIM
ATT_BLOCK = 128
ATT_LOCKSTEP = 2
ALIBI_MAX_BIAS = 8.0
QKV_W = 3 * ATT_WIDTH

SSD_INNER = 2048
SSD_HEADS = 32
SSD_GROUPS = 8
SSD_HPG = 4
SSD_HEAD_DIM = 64
SSD_STATE = 128
SSD_CONV = 4
SSD_CHUNK = 128
SSD_CONV_CH = SSD_INNER + 2 * SSD_GROUPS * SSD_STATE
SSD_GW = SSD_HPG * SSD_HEAD_DIM
SSD_LOCKSTEP = 16

MOE_GROUPS = 4
MOE_EPG = 4
MOE_EXPERTS = 16
MOE_HIDDEN = 512
MOE_PAIRS = 6
MOE_CLASSES = MOE_GROUPS * MOE_PAIRS

LANES = 128
NEG_BIG = -1e30
VMEM_LIMIT = 56 * 1024 * 1024

TM_IN = 1024
TN_IN = 512
TN_WIDE = 1024
TM_POST = 512
TM_EXP = 256
TM_FIN = 512
FINAL_PARTS = 2
SC_WINDOW = 128
SC_SUB = 32


def _cparams(*sem):
    return pltpu.CompilerParams(dimension_semantics=sem, vmem_limit_bytes=VMEM_LIMIT)


def _sigmoid(x):
    return 0.5 * jnp.tanh(0.5 * x) + 0.5


def _silu(x):
    h = 0.5 * x
    return h + h * jnp.tanh(h)


def _rms(x, g):
    ms = jnp.mean(x * x, axis=-1, keepdims=True)
    return x * lax.rsqrt(ms + RMS_EPS) * g


_IN_SEG = (3, 3, 3, SSD_INNER // TN_WIDE, SSD_CONV_CH // TN_WIDE, 2 * D_MODEL // TN_WIDE)
_IN_START = tuple(int(v) for v in np.cumsum((0,) + _IN_SEG))
_IN_NARROW = _IN_START[3]


def _inproj_kernel(x_ref, g_ref, wq_ref, ww_ref, wdt_ref, cw_ref, cb_ref, qkv1_ref, qkv2_ref, qkv3_ref, z_ref,
                   xbc_ref, gates_ref, dt_ref, h_ref, hcol_ref, halo_ref, *, per_seq):
    j = pl.program_id(1)
    tm = x_ref.shape[0]

    @pl.when(j == 0)
    def _():
        h = _rms(x_ref[...], g_ref[...])
        hb = h.astype(BF16)
        h_ref[0] = hb
        dt_ref[...] = jnp.dot(hb, wdt_ref[...], preferred_element_type=F32)
        ncb = hcol_ref.shape[0]
        for c in range(ncb):
            hcol_ref[c] = h[:, c * LANES:(c + 1) * LANES]
        for slot, (_, dil) in enumerate(ATT_PATTERNS[1:], start=1):
            rows = tm // dil
            for r in range(dil):
                for c in range(ncb):
                    h_ref[slot, r * rows:(r + 1) * rows, c * LANES:(c + 1) * LANES] = (
                        hcol_ref[c, pl.ds(r, rows, stride=dil), :].astype(BF16))

    def segment(k, fn):
        @pl.when((j >= _IN_START[k]) & (j < _IN_START[k + 1]))
        def _():
            fn()

    def narrow(slot, store):
        def fn():
            store(jnp.dot(h_ref[slot], wq_ref[...], preferred_element_type=F32).astype(BF16))
        return fn

    def wide(ref):
        def fn():
            ref[...] = jnp.dot(h_ref[0], ww_ref[...], preferred_element_type=F32).astype(BF16)
        return fn

    def store_plain(ref):
        def store(res):
            ref[...] = res
        return store

    def store_grouped(ref, dil):
        def store(res):
            ref[0] = res.reshape(dil, tm // dil, res.shape[1])
        return store

    def conv_silu():
        jb = j - _IN_START[4]
        nsub = 4
        rs = tm // nsub
        res = [jnp.dot(h_ref[0, k * rs:(k + 1) * rs, :], ww_ref[...], preferred_element_type=F32)
               for k in range(nsub)]
        seq_start = pl.program_id(0) % per_seq == 0
        prev = jnp.where(seq_start, 0.0, halo_ref[jb])
        row8 = lax.broadcasted_iota(jnp.int32, (8, 1), 0)
        cw = cw_ref[...]
        for k in range(nsub):
            acc = cb_ref[...] + cw[SSD_CONV - 1:SSD_CONV, :] * res[k]
            for sh in range(1, SSD_CONV):
                rolled = pltpu.roll(res[k], sh, 0)
                top = jnp.where(row8 < sh, pltpu.roll(prev, sh, 0), rolled[0:8])
                shifted = jnp.concatenate([top, rolled[8:]], axis=0)
                acc = acc + cw[SSD_CONV - 1 - sh:SSD_CONV - sh, :] * shifted
            xbc_ref[k * rs:(k + 1) * rs, :] = acc.astype(BF16)
            prev = res[k][rs - 8:rs]
        halo_ref[jb] = prev

    segment(0, narrow(0, store_plain(qkv1_ref)))
    segment(1, narrow(1, store_grouped(qkv2_ref, ATT_PATTERNS[1][1])))
    segment(2, narrow(2, store_grouped(qkv3_ref, ATT_PATTERNS[2][1])))
    segment(3, wide(z_ref))
    segment(4, conv_silu)
    segment(5, wide(gates_ref))


def _in_proj(x2d, g, w_qkv, w_wide, w_dt, conv_w, conv_b, bsz, seq):
    t = x2d.shape[0]
    nj = _IN_START[-1]
    per_seq = seq // TM_IN
    d2, d3 = ATT_PATTERNS[1][1], ATT_PATTERNS[2][1]

    def seg_map(k):
        return lambda i, j: (i, jnp.clip(j - _IN_START[k], 0, _IN_SEG[k] - 1))

    def grp_map(k):
        return lambda i, j: (i // per_seq, 0, i % per_seq, jnp.clip(j - _IN_START[k], 0, _IN_SEG[k] - 1))

    def wq_map(i, j):
        jj = jnp.minimum(j, _IN_NARROW - 1)
        return (0, (jj % 3) * ATT_GROUPS + jj // 3)

    out_shapes = (
        jax.ShapeDtypeStruct((t, ATT_WIDTH), BF16),
        jax.ShapeDtypeStruct((bsz, d2, seq // d2, ATT_WIDTH), BF16),
        jax.ShapeDtypeStruct((bsz, d3, seq // d3, ATT_WIDTH), BF16),
        jax.ShapeDtypeStruct((t, SSD_INNER), BF16),
        jax.ShapeDtypeStruct((t, SSD_CONV_CH), BF16),
        jax.ShapeDtypeStruct((t, 2 * D_MODEL), BF16),
        jax.ShapeDtypeStruct((t, LANES), F32),
    )
    out_specs = (
        pl.BlockSpec((TM_IN, TN_IN), seg_map(0)),
        pl.BlockSpec((1, d2, TM_IN // d2, TN_IN), grp_map(1)),
        pl.BlockSpec((1, d3, TM_IN // d3, TN_IN), grp_map(2)),
        pl.BlockSpec((TM_IN, TN_WIDE), seg_map(3)),
        pl.BlockSpec((TM_IN, TN_WIDE), seg_map(4)),
        pl.BlockSpec((TM_IN, TN_WIDE), seg_map(5)),
        pl.BlockSpec((TM_IN, LANES), lambda i, j: (i, 0)),
    )
    conv_map = lambda i, j: (0, jnp.clip(j - _IN_START[4], 0, _IN_SEG[4] - 1))
    return pl.pallas_call(
        functools.partial(_inproj_kernel, per_seq=per_seq),
        grid=(t // TM_IN, nj),
        in_specs=[
            pl.BlockSpec((TM_IN, D_MODEL), lambda i, j: (i, 0)),
            pl.BlockSpec((1, D_MODEL), lambda i, j: (0, 0)),
            pl.BlockSpec((D_MODEL, TN_IN), wq_map),
            pl.BlockSpec((D_MODEL, TN_WIDE), lambda i, j: (0, jnp.maximum(j - _IN_NARROW, 0))),
            pl.BlockSpec((D_MODEL, LANES), lambda i, j: (0, 0)),
            pl.BlockSpec((SSD_CONV, TN_WIDE), conv_map),
            pl.BlockSpec((1, TN_WIDE), conv_map),
        ],
        out_specs=out_specs,
        out_shape=out_shapes,
        scratch_shapes=[pltpu.VMEM((3, TM_IN, D_MODEL), BF16),
                        pltpu.VMEM((D_MODEL // LANES, TM_IN, LANES), F32),
                        pltpu.VMEM((_IN_SEG[4], 8, TN_WIDE), F32)],
        compiler_params=_cparams("arbitrary", "arbitrary"),
    )(x2d, g, w_qkv, w_wide, w_dt, conv_w.astype(F32), conv_b.reshape(1, SSD_CONV_CH).astype(F32))


def _att_bias_tables():
    h = np.arange(1, ATT_GROUPS * ATT_HPG + 1, dtype=np.float32)
    slopes = np.exp2(-ALIBI_MAX_BIAS * h / (ATT_GROUPS * ATT_HPG)).astype(np.float32).reshape(ATT_GROUPS, ATT_HPG)
    qi = np.arange(ATT_BLOCK)[:, None] + ATT_BLOCK
    kj = np.arange(2 * ATT_BLOCK)[None, :]
    delta = qi - kj
    tabs = []
    for g, (window, dil) in enumerate(ATT_PATTERNS):
        span = window // dil
        band = (delta >= 0) & (delta <= span)
        bias = (-slopes[g][:, None, None] * (delta * dil).astype(np.float32)[None]).astype(np.float32)
        with_prev = np.where(band[None], bias, np.float32(NEG_BIG))
        first = np.where((band & (kj >= ATT_BLOCK))[None], bias, np.float32(NEG_BIG))
        tabs.append(np.stack([first, with_prev]).astype(np.float32))
    return tabs[0], tabs[1], tabs[2][1][:, :, ATT_BLOCK:]


def _att_units(units):
    lane = lax.broadcasted_iota(jnp.int32, (1, LANES), 1)
    low = lane < ATT_HEAD_DIM
    scale = ATT_HEAD_DIM ** -0.5
    qmask = (jnp.where(low, scale, 0.0).astype(BF16), jnp.where(low, 0.0, scale).astype(BF16))
    heads = [(q, k2, v2, tab_fn, hh) for q, k2, v2, tab_fn in units for hh in range(2)]
    scores = [lax.dot_general(q * qmask[hh], k2, (((1,), (1,)), ((), ())), preferred_element_type=F32)
              for q, k2, _, _, hh in heads]
    probs = []
    for s, (_, _, _, tab_fn, hh) in zip(scores, heads):
        s = s + tab_fn(hh)
        m = jnp.max(s, axis=-1, keepdims=True)
        e = jnp.exp(s - m)
        probs.append((e.astype(BF16), m, jnp.sum(e, axis=-1, keepdims=True)))
    pvs = [jnp.dot(e, v2, preferred_element_type=F32) for (e, _, _), (_, _, v2, _, _) in zip(probs, heads)]
    outs = [pv / den for pv, (_, _, den) in zip(pvs, probs)]
    lses = [m + jnp.log(den) for _, m, den in probs]
    return [(jnp.where(low, outs[2 * i], outs[2 * i + 1]), jnp.where(low, lses[2 * i], lses[2 * i + 1]))
            for i in range(len(units))]


def _att_kernel(q1, k1, v1, q2, k2, v2, q3, k3, v3, t1, t2, t3, out_ref,
                o1, l1, o2p, l2p, o2n, l2n, o3p, l3p, o3n, l3n, *, seq):
    blk = ATT_BLOCK
    d2, d3 = ATT_PATTERNS[1][1], ATT_PATTERNS[2][1]
    nb1, nb2 = seq // blk, seq // d2 // blk

    def rows(n):
        return pl.ds(pl.multiple_of(n * blk, blk), blk)

    assert nb1 == d2 * nb2 == d3

    def unit_body(i, _):
        units, dests = [], []
        for k in range(ATT_LOCKSTEP):
            u = i * ATT_LOCKSTEP + k
            cur, prv = rows(u), rows(jnp.maximum(u - 1, 0))
            sel = jnp.minimum(u, 1)
            units.append((q1[0, cur, :], jnp.concatenate([k1[0, prv, :], k1[0, cur, :]], axis=0),
                          jnp.concatenate([v1[0, prv, :], v1[0, cur, :]], axis=0),
                          lambda hh, sel=sel: t1[sel, hh]))
            r, n = u // nb2, u % nb2
            cur2, prv2 = rows(n), rows(jnp.maximum(n - 1, 0))
            sel2 = jnp.minimum(n, 1)
            units.append((q2[0, r, cur2, :], jnp.concatenate([k2[0, r, prv2, :], k2[0, r, cur2, :]], axis=0),
                          jnp.concatenate([v2[0, r, prv2, :], v2[0, r, cur2, :]], axis=0),
                          lambda hh, sel2=sel2: t2[sel2, hh]))
            units.append((q3[0, u], k3[0, u], v3[0, u], lambda hh: t3[hh]))
            dests += [(o1, l1, cur), (o2p, l2p, cur), (o3p, l3p, cur)]
        for (o, l), (o_ref, l_ref, where) in zip(_att_units(units), dests):
            o_ref[where, :] = o
            l_ref[where, :] = l
        return 0

    lax.fori_loop(0, nb1 // ATT_LOCKSTEP, unit_body, 0)

    for dil, pairs in ((d2, ((o2p, o2n), (l2p, l2n))), (d3, ((o3p, o3n), (l3p, l3n)))):
        n_sub = seq // dil
        for r in range(dil):
            for src, dst in pairs:
                dst[pl.ds(r, n_sub, stride=dil), :] = src[r * n_sub:(r + 1) * n_sub, :]

    mrows = 2 * blk

    def merge(c, _):
        rr = pl.ds(pl.multiple_of(c * mrows, mrows), mrows)
        la, lb, lc = l1[rr, :], l2n[rr, :], l3n[rr, :]
        lm = jnp.maximum(jnp.maximum(la, lb), lc)
        ea, eb, ec = jnp.exp(la - lm), jnp.exp(lb - lm), jnp.exp(lc - lm)
        att = (ea * o1[rr, :] + eb * o2n[rr, :] + ec * o3n[rr, :]) / (ea + eb + ec)
        out_ref[0, rr, :] = att.astype(BF16)
        return 0

    lax.fori_loop(0, seq // mrows, merge, 0)


def _attention(qkv1, qkv2, qkv3, bsz, seq):
    t1, t2, t3 = (jnp.asarray(t) for t in _att_bias_tables())
    d2, d3 = ATT_PATTERNS[1][1], ATT_PATTERNS[2][1]
    npair = ATT_HPG // 2
    in_arrays, in_specs = [], []
    for arr, lead in ((qkv1, ()), (qkv2, (d2,)), (qkv3, (d3,))):
        n_rows = arr.shape[-2]
        for sel in range(3):
            zeros = (0,) * len(lead)
            in_arrays.append(arr)
            in_specs.append(pl.BlockSpec((1,) + lead + (n_rows, LANES),
                                         lambda hp, b, sel=sel, zeros=zeros: (b,) + zeros + (0, sel * npair + hp)))
    in_arrays += [t1, t2, t3]
    in_specs += [pl.BlockSpec((2, 2, ATT_BLOCK, 2 * ATT_BLOCK), lambda hp, b: (0, hp, 0, 0)),
                 pl.BlockSpec((2, 2, ATT_BLOCK, 2 * ATT_BLOCK), lambda hp, b: (0, hp, 0, 0)),
                 pl.BlockSpec((2, ATT_BLOCK, ATT_BLOCK), lambda hp, b: (hp, 0, 0))]
    return pl.pallas_call(
        functools.partial(_att_kernel, seq=seq),
        grid=(npair, bsz),
        in_specs=in_specs,
        out_specs=pl.BlockSpec((1, seq, LANES), lambda hp, b: (b, 0, hp)),
        out_shape=jax.ShapeDtypeStruct((bsz, seq, ATT_OUT), BF16),
        scratch_shapes=[pltpu.VMEM((seq, LANES), F32)] * 10,
        compiler_params=_cparams("arbitrary", "arbitrary"),
    )(*in_arrays)


def _softplus(x):
    return jnp.maximum(x, 0.0) + jnp.log1p(jnp.exp(-jnp.abs(x)))


def _head_spread_table():
    gw, n_src = SSD_GW, LANES
    tab = np.zeros((2 * n_src, 2 * gw), np.float32)
    for half in range(2):
        for blk, src0 in enumerate((4, 8)):
            for col in range(gw):
                tab[half * n_src + src0 + col // SSD_HEAD_DIM, blk * gw + col] = 1.0
    return tab


def _ssd_kernel(x_ref, b_ref, c_ref, z_ref, dt_ref, spread_ref, dtb_ref, alog_ref, dsk_ref, u_ref, carry_ref,
                *, n_chunks):
    L = SSD_CHUNK
    gw = SSD_GW
    carry_ref[...] = jnp.zeros(carry_ref.shape, F32)

    a_neg = -jnp.exp(alog_ref[0])
    dtb = dtb_ref[0]
    dsk = dsk_ref[0]
    ri = lax.broadcasted_iota(jnp.int32, (L, L), 0)
    ci = lax.broadcasted_iota(jnp.int32, (L, L), 1)
    upper_incl = (ri <= ci).astype(F32)
    causal = ri >= ci
    lane = lax.broadcasted_iota(jnp.int32, (1, LANES), 1)
    low = lane < SSD_HEAD_DIM
    heads = range(SSD_HPG)

    def load(s):
        rows = pl.ds(s["r0"], L)
        s["xs"] = _silu(x_ref[0, rows, :].astype(F32))
        s["xs_b"] = s["xs"].astype(BF16)
        s["bm"] = _silu(b_ref[0, rows, :].astype(F32))
        s["cm_b"] = _silu(c_ref[0, rows, :].astype(F32)).astype(BF16)

    def decay_cumsum(s):
        dt = _softplus(dt_ref[0, 0, :, pl.ds(s["r0"], L)] + dtb)
        rows8 = jnp.concatenate([dt * a_neg, dt], axis=0)
        cs8 = jnp.dot(rows8, upper_incl, preferred_element_type=F32, precision=lax.Precision.HIGHEST)
        s["dt"], s["acs_t"] = dt, cs8[0:SSD_HPG]

    def decay_spread(s):
        dt, acs_t = s["dt"], s["acs_t"]
        t16 = jnp.concatenate([acs_t, acs_t, acs_t, dt], axis=0)
        cols = jnp.concatenate([t16, jnp.zeros((L - 16, L), F32)], axis=0).T
        dt_at8 = pltpu.roll(cols, LANES - 4, 1)
        comb = jnp.where(lane < 8, jnp.exp(cols), jnp.exp(cols[L - 1:L, :] - cols) * dt_at8)
        hi = comb.astype(BF16)
        lo = (comb - hi.astype(F32)).astype(BF16)
        s["spread"] = jnp.dot(jnp.concatenate([hi, lo], axis=1), spread_ref[...], preferred_element_type=F32)
        s["acs_cols"] = cols

    def scores(s):
        s["bm_t"] = s.pop("bm").T.astype(BF16)
        s["cb"] = jnp.dot(s["cm_b"], s["bm_t"], preferred_element_type=F32)

    def intra(s):
        spread, acs_t, dt, cb, cols = s["spread"], s.pop("acs_t"), s.pop("dt"), s.pop("cb"), s.pop("acs_cols")
        xs, xs_b = s["xs"], s.pop("xs_b")
        mixes = []
        for j in heads:
            seg = cols[:, j:j + 1] - acs_t[j:j + 1, :]
            mixes.append((jnp.exp(jnp.where(causal, seg, -jnp.inf)) * (cb * dt[j:j + 1, :])).astype(BF16))
        halves = []
        for hp in range(SSD_HPG // 2):
            yy = jnp.dot(jnp.concatenate(mixes[2 * hp:2 * hp + 2], axis=0), xs_b[:, hp * LANES:(hp + 1) * LANES],
                         preferred_element_type=F32)
            halves.append(jnp.where(low, yy[:L], yy[L:]))
        s["y"] = jnp.concatenate(halves, axis=1)
        s["st_new"] = jnp.dot(s.pop("bm_t"), (xs * spread[:, gw:]).astype(BF16),
                              preferred_element_type=F32)

    def inter(s):
        eacs_bc = s.pop("spread")[:, :gw]
        carry = carry_ref[...]
        y = s.pop("y") + jnp.dot(s.pop("cm_b"), carry.astype(BF16), preferred_element_type=F32) * eacs_bc
        carry_ref[...] = carry * eacs_bc[L - 1:L, :] + s.pop("st_new")
        y = y + s.pop("xs") * dsk
        rows = pl.ds(s["r0"], L)
        u_ref[0, rows, :] = (y * _silu(z_ref[0, rows, :].astype(F32))).astype(BF16)

    def body(i, _):
        states = [{"c": i * SSD_LOCKSTEP + k, "r0": pl.multiple_of((i * SSD_LOCKSTEP + k) * L, L)}
                  for k in range(SSD_LOCKSTEP)]
        for stage in (load, decay_cumsum, decay_spread, scores, intra, inter):
            for s in states:
                stage(s)
        return 0

    lax.fori_loop(0, n_chunks // SSD_LOCKSTEP, body, 0)


def _ssd(xbc, z, dt_t, dt_bias, a_log, d_skip, bsz, seq):
    gw = SSD_GW
    nxb = SSD_INNER // SSD_STATE
    dsk = jnp.repeat(d_skip.astype(F32), SSD_HEAD_DIM).reshape(SSD_GROUPS, 1, gw)
    dtb = dt_bias.astype(F32).reshape(SSD_GROUPS, SSD_HPG, 1)
    alog = a_log.astype(F32).reshape(SSD_GROUPS, SSD_HPG, 1)
    spread = jnp.asarray(_head_spread_table(), BF16)
    x_map = lambda b, g: (b, 0, g)
    bm_map = lambda b, g: (b, 0, nxb + g)
    cm_map = lambda b, g: (b, 0, nxb + SSD_GROUPS + g)
    return pl.pallas_call(
        functools.partial(_ssd_kernel, n_chunks=seq // SSD_CHUNK),
        grid=(bsz, SSD_GROUPS),
        in_specs=[
            pl.BlockSpec((1, seq, gw), x_map),
            pl.BlockSpec((1, seq, SSD_STATE), bm_map),
            pl.BlockSpec((1, seq, SSD_STATE), cm_map),
            pl.BlockSpec((1, seq, gw), x_map),
            pl.BlockSpec((1, 1, SSD_HPG, seq), lambda b, g: (b, g, 0, 0)),
            pl.BlockSpec(spread.shape, lambda b, g: (0, 0)),
            pl.BlockSpec((1, SSD_HPG, 1), lambda b, g: (g, 0, 0)),
            pl.BlockSpec((1, SSD_HPG, 1), lambda b, g: (g, 0, 0)),
            pl.BlockSpec((1, 1, gw), lambda b, g: (g, 0, 0)),
        ],
        out_specs=pl.BlockSpec((1, seq, gw), x_map),
        out_shape=jax.ShapeDtypeStruct((bsz, seq, SSD_INNER), BF16),
        scratch_shapes=[pltpu.VMEM((SSD_STATE, gw), F32)],
        compiler_params=_cparams("arbitrary", "arbitrary"),
    )(xbc, xbc, xbc, z, dt_t, spread, dtb, alog, dsk)


def _first_index_of_max(vals, lane_f):
    m = jnp.max(vals, axis=-1, keepdims=True)
    idx = jnp.min(jnp.where(vals == m, lane_f, float(LANES)), axis=-1, keepdims=True)
    return m, idx


def _post_kernel(att_ref, u_ref, gates_ref, x_ref, watt_ref, wssd_ref, wout_ref,
                 gssd_ref, gffn_ref, wr_ref, br_ref, x1_ref, meta_ref, cnt_ref, run_ref):
    i = pl.program_id(0)

    @pl.when(i == 0)
    def _():
        run_ref[...] = jnp.zeros(run_ref.shape, F32)

    tm = x_ref.shape[0]
    nsub = 2
    subs = [{"rows": pl.ds(k * (tm // nsub), tm // nsub)} for k in range(nsub)]

    def att_branch(s):
        s["y_att"] = jnp.dot(att_ref[s["rows"], :], watt_ref[...], preferred_element_type=F32)

    def ssd_branch(s):
        ssd = _rms(u_ref[s["rows"], :].astype(F32), gssd_ref[...])
        s["y_ssd"] = jnp.dot(ssd.astype(BF16), wssd_ref[...], preferred_element_type=F32)

    def mix_out(s):
        gates = gates_ref[s["rows"], :].astype(F32)
        merged = _sigmoid(gates[:, :D_MODEL]) * s.pop("y_att") + _sigmoid(gates[:, D_MODEL:]) * s.pop("y_ssd")
        x1 = x_ref[s["rows"], :] + jnp.dot(merged.astype(BF16), wout_ref[...], preferred_element_type=F32)
        x1_ref[s["rows"], :] = x1
        s["h2"] = _rms(x1, gffn_ref[...])

    def router(s):
        h2 = s.pop("h2")
        hi = h2.astype(BF16)
        lo = (h2 - hi.astype(F32)).astype(BF16)
        both = jnp.dot(jnp.concatenate([hi, lo], axis=1), wr_ref[...], preferred_element_type=F32)
        s["logits"] = both[:, :LANES] + both[:, LANES:]

    for stage in (att_branch, ssd_branch, mix_out, router):
        for s in subs:
            stage(s)
    logits = jnp.concatenate([s["logits"] for s in subs], axis=0) + br_ref[...]
    lane = lax.broadcasted_iota(jnp.int32, (tm, LANES), 1)
    lane_f = lane.astype(F32)
    ninf = -jnp.inf
    gl = jnp.where(lane < MOE_GROUPS, logits, ninf)
    gmax, gidx = _first_index_of_max(gl, lane_f)
    g_val = 1.0 / jnp.sum(jnp.exp(gl - gmax), axis=-1, keepdims=True)
    base = MOE_GROUPS + MOE_EPG * gidx
    el = jnp.where((lane_f >= base) & (lane_f < base + MOE_EPG), logits, ninf)
    e1, i1 = _first_index_of_max(el, lane_f)
    e2, i2 = _first_index_of_max(jnp.where(lane_f == i1, ninf, el), lane_f)
    t2 = jnp.exp(e2 - e1)
    w1 = g_val / (1.0 + t2)
    w2 = g_val * t2 / (1.0 + t2)
    a1, a2 = i1 - base, i2 - base
    lo, hi = jnp.minimum(a1, a2), jnp.maximum(a1, a2)
    c_lo = jnp.where(a1 < a2, w1, w2)
    c_hi = jnp.where(a1 < a2, w2, w1)
    pair = lo * (7.0 - lo) * 0.5 + (hi - lo - 1.0)
    cls = gidx * MOE_PAIRS + pair

    onehot = (lane_f == cls)
    oh_b = jnp.where(onehot, 1.0, 0.0).astype(BF16)
    rr = lax.broadcasted_iota(jnp.int32, (tm, tm), 0)
    cc = lax.broadcasted_iota(jnp.int32, (tm, tm), 1)
    strict = jnp.where(rr > cc, 1.0, 0.0).astype(BF16)
    prefix = jnp.dot(strict, oh_b, preferred_element_type=F32) + run_ref[...]
    rank = jnp.sum(jnp.where(onehot, prefix, 0.0), axis=-1, keepdims=True)
    run = run_ref[...] + jnp.sum(oh_b.astype(F32), axis=0, keepdims=True)
    run_ref[...] = run
    cnt_ref[...] = jnp.broadcast_to(run, cnt_ref.shape)

    meta = jnp.where(lane == 0, cls, jnp.where(lane == 1, rank, jnp.where(lane == 2, c_lo,
                     jnp.where(lane == 3, c_hi, 0.0))))
    meta_ref[...] = meta


def _post(att, u, gates, x2d, w_att, w_ssd, w_out, g_ssd, g_ffn, w_r, b_r):
    t = x2d.shape[0]
    tm = TM_POST
    row = lambda w: pl.BlockSpec((tm, w), lambda i: (i, 0))
    const = lambda a: pl.BlockSpec(a.shape, lambda i: (0,) * a.ndim)
    return pl.pallas_call(
        _post_kernel,
        grid=(t // tm,),
        in_specs=[row(ATT_OUT), row(SSD_INNER), row(2 * D_MODEL), row(D_MODEL),
                  const(w_att), const(w_ssd), const(w_out), const(g_ssd), const(g_ffn), const(w_r), const(b_r)],
        out_specs=(row(D_MODEL), row(LANES), pl.BlockSpec((8, LANES), lambda i: (0, 0))),
        out_shape=(jax.ShapeDtypeStruct((t, D_MODEL), F32),
                   jax.ShapeDtypeStruct((t, LANES), F32),
                   jax.ShapeDtypeStruct((8, LANES), F32)),
        scratch_shapes=[pltpu.VMEM((1, LANES), F32)],
        compiler_params=_cparams("arbitrary"),
    )(att, u, gates, x2d, w_att, w_ssd, w_out, g_ssd, g_ffn, w_r, b_r)


def _pack_pair(a, b):
    ua = lax.bitcast_convert_type(a.astype(BF16).astype(F32), jnp.uint32)
    ub = lax.bitcast_convert_type(b.astype(BF16).astype(F32), jnp.uint32)
    return (ua & jnp.uint32(0xFFFF0000)) | (ub >> 16)


def _unpack_pair(p):
    a = lax.bitcast_convert_type(p & jnp.uint32(0xFFFF0000), F32)
    b = lax.bitcast_convert_type(p << 16, F32)
    return a, b


def _sc_mesh():
    return plsc.VectorSubcoreMesh(core_axis_name="c", subcore_axis_name="s")


def _sc_permute_rows(data, idx, n_out, scatter):
    n, width = idx.shape[0], data.shape[1]
    mesh = _sc_mesh()
    workers = mesh.num_cores * mesh.num_subcores
    per_worker = n // SC_WINDOW // workers
    assert per_worker * workers * SC_WINDOW == n

    n_sub = SC_WINDOW // SC_SUB

    @pl.kernel(out_type=jax.ShapeDtypeStruct((n_out, width), data.dtype), mesh=mesh,
               scratch_types=[pltpu.VMEM((1, SC_WINDOW), jnp.int32), pltpu.VMEM((2, SC_SUB, width), data.dtype),
                              pltpu.SemaphoreType.DMA((2,)), pltpu.SemaphoreType.DMA((2,))])
    def permute(data_hbm, idx_hbm, out_hbm, idx_v, rows_v, sem_in, sem_out):
        worker = lax.axis_index("c") * mesh.num_subcores + lax.axis_index("s")

        @pl.loop(0, per_worker)
        def _(j):
            base = (worker * per_worker + j) * SC_WINDOW
            pltpu.sync_copy(idx_hbm.at[:, pl.ds(base, SC_WINDOW)], idx_v)

            def ends(k):
                sub_idx = idx_v.at[0, pl.ds(k * SC_SUB, SC_SUB)]
                plain = pl.ds(base + k * SC_SUB, SC_SUB)
                return (data_hbm.at[plain], out_hbm.at[sub_idx]) if scatter else (data_hbm.at[sub_idx],
                                                                                  out_hbm.at[plain])

            def read(k):
                return pltpu.make_async_copy(ends(k)[0], rows_v.at[k % 2], sem_in.at[k % 2])

            def write(k):
                return pltpu.make_async_copy(rows_v.at[k % 2], ends(k)[1], sem_out.at[k % 2])

            read(0).start()
            for k in range(n_sub):
                read(k).wait()
                if k + 1 < n_sub:
                    if k >= 1:
                        write(k - 1).wait()
                    read(k + 1).start()
                write(k).start()
            write(n_sub - 2).wait()
            write(n_sub - 1).wait()

    return permute(data, idx.reshape(1, n))


def _sc_scatter_rows(data, idx, n_out):
    return _sc_permute_rows(data, idx, n_out, scatter=True)


def _sc_gather_rows(data, idx):
    return _sc_permute_rows(data, idx, idx.shape[0], scatter=False)


def _expert_kernel(nused_ref, elo_ref, ehi_ref, valid_ref, xs_ref, g_ref, wg_lo, wu_lo, wd_lo, wg_hi, wu_hi, wd_hi,
                   ys_ref):
    del elo_ref, ehi_ref
    i = pl.program_id(0)

    @pl.when(i < nused_ref[0])
    def _():
        row = lax.broadcasted_iota(jnp.int32, (xs_ref.shape[0], 1), 0)
        xb = _rms(jnp.where(row < valid_ref[i], xs_ref[...], 0.0), g_ref[...]).astype(BF16)
        pre = [(jnp.dot(xb, wg[0], preferred_element_type=F32), jnp.dot(xb, wu[0], preferred_element_type=F32))
               for wg, wu in ((wg_lo, wu_lo), (wg_hi, wu_hi))]
        hid = [(_silu(gt) * up).astype(BF16) for gt, up in pre]
        y_lo, y_hi = (jnp.dot(h, wd[0], preferred_element_type=F32) for h, wd in zip(hid, (wd_lo, wd_hi)))
        ys_ref[...] = _pack_pair(y_lo, y_hi)

    @pl.when(i >= nused_ref[0])
    def _():
        ys_ref[...] = jnp.zeros(ys_ref.shape, jnp.uint32)


def _experts(n_used, tile_elo, tile_ehi, tile_valid, xs, g_ffn, w_gate, w_up, w_down):
    p_rows = xs.shape[0]
    tm = TM_EXP
    n_tiles = p_rows // tm

    def last_used(i, nu):
        return jnp.minimum(i, jnp.maximum(nu[0] - 1, 0))

    def row_map(i, nu, elo, ehi, valid):
        return (last_used(i, nu), 0)

    def wmap(which):
        def f(i, nu, elo, ehi, valid):
            return ((elo, ehi)[which][last_used(i, nu)], 0, 0)
        return f

    wspec_in = lambda which: pl.BlockSpec((1, D_MODEL, MOE_HIDDEN), wmap(which))
    wspec_out = lambda which: pl.BlockSpec((1, MOE_HIDDEN, D_MODEL), wmap(which))
    grid_spec = pltpu.PrefetchScalarGridSpec(
        num_scalar_prefetch=4,
        grid=(n_tiles,),
        in_specs=[pl.BlockSpec((tm, D_MODEL), row_map),
                  pl.BlockSpec((1, D_MODEL), lambda i, nu, elo, ehi, valid: (0, 0)),
                  wspec_in(0), wspec_in(0), wspec_out(0), wspec_in(1), wspec_in(1), wspec_out(1)],
        out_specs=pl.BlockSpec((tm, D_MODEL), lambda i, nu, elo, ehi, valid: (i, 0)),
    )
    return pl.pallas_call(
        _expert_kernel,
        grid_spec=grid_spec,
        out_shape=jax.ShapeDtypeStruct((p_rows, D_MODEL), jnp.uint32),
        compiler_params=_cparams("arbitrary"),
    )(n_used, tile_elo, tile_ehi, tile_valid, xs, g_ffn, w_gate, w_up, w_down, w_gate, w_up, w_down)


def _final_kernel(x1_ref, meta_ref, p_ref, yt_ref, gple_ref, wpg_ref, wpp_ref, gfin_ref, *rest):
    out_ref = rest[-1]
    tm = x1_ref.shape[0]
    nsub = 2
    subs = [{"rows": pl.ds(k * (tm // nsub), tm // nsub)} for k in range(nsub)]

    def embed(s):
        s["pp"] = jnp.dot(p_ref[s["rows"], :].astype(BF16), wpp_ref[...], preferred_element_type=F32)

    def combine(s):
        y_lo, y_hi = _unpack_pair(yt_ref[s["rows"], :])
        meta = meta_ref[s["rows"], :]
        s["x2"] = x1_ref[s["rows"], :] + meta[:, 2:3] * y_lo + meta[:, 3:4] * y_hi
        s["hn"] = _rms(s["x2"], gple_ref[...]).astype(BF16)

    def gate(s):
        s["gate"] = jnp.dot(s.pop("hn"), wpg_ref[...], preferred_element_type=F32)

    def finish(s):
        out_ref[s["rows"], :] = _rms(s.pop("x2") + _sigmoid(s.pop("gate")) * s.pop("pp"), gfin_ref[...])

    for stage in (embed, combine, gate, finish):
        for s in subs:
            stage(s)


def _final(x1, meta, p2d, ys_part, g_ple, w_pg, w_pp, g_fin, part, out_prev):
    t = x1.shape[0]
    tm = TM_FIN
    steps = ys_part.shape[0] // tm
    row = lambda w: pl.BlockSpec((tm, w), lambda i: (part * steps + i, 0))
    const = lambda a: pl.BlockSpec(a.shape, lambda i: (0,) * a.ndim)
    in_specs = [row(D_MODEL), row(LANES), row(PLE_DIM), pl.BlockSpec((tm, D_MODEL), lambda i: (i, 0)),
                const(g_ple), const(w_pg), const(w_pp), const(g_fin)]
    args = [x1, meta, p2d, ys_part, g_ple, w_pg, w_pp, g_fin]
    aliases = {}
    if out_prev is not None:
        in_specs.append(pl.BlockSpec(memory_space=pl.ANY))
        args.append(out_prev)
        aliases = {len(args) - 1: 0}
    return pl.pallas_call(
        _final_kernel,
        grid=(steps,),
        in_specs=in_specs,
        out_specs=row(D_MODEL),
        out_shape=jax.ShapeDtypeStruct((t, D_MODEL), F32),
        input_output_aliases=aliases,
        compiler_params=_cparams("arbitrary"),
    )(*args)


_PAIR_LO = np.array([0, 0, 0, 1, 1, 2], np.int32)
_PAIR_HI = np.array([1, 2, 3, 2, 3, 3], np.int32)


def _routing_tables(meta, counts_f, n_tiles):
    cls = meta[:, 0].astype(jnp.int32)
    rank = meta[:, 1].astype(jnp.int32)
    counts = counts_f[0, :MOE_CLASSES].astype(jnp.int32)
    tiles_per = (counts + TM_EXP - 1) // TM_EXP
    tile_end = jnp.cumsum(tiles_per)
    tile_start = tile_end - tiles_per
    class_ids = jnp.arange(MOE_CLASSES, dtype=jnp.int32)
    pos = jnp.sum(jnp.where(cls[:, None] == class_ids[None, :], (tile_start * TM_EXP)[None, :], 0), axis=1) + rank
    n_used = tile_end[-1:]
    tile_ids = jnp.arange(n_tiles, dtype=jnp.int32)
    tile_cls = jnp.minimum(jnp.sum((tile_end[None, :] <= tile_ids[:, None]).astype(jnp.int32), axis=1),
                           MOE_CLASSES - 1)
    grp = tile_cls // MOE_PAIRS
    pair = tile_cls % MOE_PAIRS
    tile_elo = grp * MOE_EPG + jnp.asarray(_PAIR_LO)[pair]
    tile_ehi = grp * MOE_EPG + jnp.asarray(_PAIR_HI)[pair]
    class_left = jnp.sum(jnp.where(tile_cls[:, None] == class_ids[None, :],
                                   (counts - (tile_ids[:, None] - tile_start[None, :]) * TM_EXP), 0), axis=1)
    tile_valid = jnp.clip(class_left, 0, TM_EXP)
    return (pos.astype(jnp.int32), n_used.astype(jnp.int32), tile_elo.astype(jnp.int32),
            tile_ehi.astype(jnp.int32), tile_valid.astype(jnp.int32))


def kernel(x, p, norm_mix_g, w_in, conv_w, conv_b, dt_bias, a_log, d_skip, ssd_norm_g, w_att_branch,
           w_ssd_branch, w_out, norm_ffn_g, w_router_group, b_router_group, w_router_expert,
           b_router_expert, w_exp_gate, w_exp_up, w_exp_down, norm_ple_g, w_ple_gate, w_ple_proj,
           final_norm_g):
    bsz, seq, _ = x.shape
    t = bsz * seq
    assert w_in.shape[0] == 1, "single-layer block"
    assert seq // ATT_PATTERNS[-1][1] == ATT_BLOCK and seq % TM_IN == 0
    x2d = x.reshape(t, D_MODEL)

    wi = w_in[0]
    c_dt = QKV_W + SSD_INNER + SSD_CONV_CH
    w_qkv = wi[:, :QKV_W].astype(BF16)
    w_wide = jnp.concatenate([wi[:, QKV_W:c_dt], wi[:, c_dt + SSD_HEADS:]], axis=1).astype(BF16)
    w_dt = jnp.pad(wi[:, c_dt:c_dt + SSD_HEADS], ((0, 0), (0, LANES - SSD_HEADS))).astype(BF16)
    row = lambda v: v.reshape(1, -1).astype(F32)

    qkv1, qkv2, qkv3, z, xbc, gates, dt_raw = _in_proj(x2d, row(norm_mix_g[0]), w_qkv, w_wide, w_dt,
                                                       conv_w[0], conv_b[0], bsz, seq)

    att = _attention(qkv1.reshape(bsz, seq, ATT_WIDTH), qkv2, qkv3, bsz, seq).reshape(t, ATT_OUT)

    dt_t = dt_raw[:, :SSD_HEADS].reshape(bsz, seq, SSD_GROUPS, SSD_HPG).transpose(0, 2, 3, 1)
    u = _ssd(xbc.reshape(bsz, seq, SSD_CONV_CH), z.reshape(bsz, seq, SSD_INNER), dt_t,
             dt_bias[0], a_log[0], d_skip[0], bsz, seq).reshape(t, SSD_INNER)

    w_r32 = jnp.pad(jnp.concatenate([w_router_group[0], w_router_expert[0]], axis=1),
                    ((0, 0), (0, LANES - MOE_GROUPS - MOE_EXPERTS))).astype(F32)
    w_r_hi = w_r32.astype(BF16)
    w_r_lo = (w_r32 - w_r_hi.astype(F32)).astype(BF16)
    w_r = jnp.concatenate([jnp.concatenate([w_r_hi, w_r_lo], axis=1),
                           jnp.concatenate([w_r_hi, jnp.zeros_like(w_r_lo)], axis=1)], axis=0)
    b_r = jnp.pad(jnp.concatenate([b_router_group[0], b_router_expert[0]]),
                  (0, LANES - MOE_GROUPS - MOE_EXPERTS)).reshape(1, LANES).astype(F32)
    x1, meta, counts = _post(att, u, gates, x2d,
                             w_att_branch[0].astype(BF16), w_ssd_branch[0].astype(BF16), w_out[0].astype(BF16),
                             row(ssd_norm_g[0]), row(norm_ffn_g[0]), w_r, b_r)

    n_tiles = t // TM_EXP + MOE_CLASSES
    pos, n_used, tile_elo, tile_ehi, tile_valid = _routing_tables(meta, counts, n_tiles)
    xs = _sc_scatter_rows(x1, pos, n_tiles * TM_EXP)
    ys = _experts(n_used, tile_elo, tile_ehi, tile_valid, xs, row(norm_ffn_g[0]),
                  w_exp_gate[0].astype(BF16), w_exp_up[0].astype(BF16), w_exp_down[0].astype(BF16))
    out = None
    part_len = t // FINAL_PARTS
    for part in range(FINAL_PARTS):
        ys_part = _sc_gather_rows(ys, pos[part * part_len:(part + 1) * part_len])
        out = _final(x1, meta, p[0].reshape(t, PLE_DIM), ys_part, row(norm_ple_g[0]),
                     w_ple_gate[0].astype(BF16), w_ple_proj[0].astype(BF16), row(final_norm_g), part, out)
    return out.reshape(bsz, seq, D_MODEL)
```
